```python
import jax, jax.numpy as jnp
from jax import lax
import numpy as np

D_MODEL = 1024
BATCH = 2
SEQ = 8192
DEPTH = 4
DEC_BATCH = 128
DEC_SEQ = 8
PAST_LEN = 2048
PAGE_SIZE = 128

N_A_LAYERS = DEPTH // 2
N_B_LAYERS = DEPTH - N_A_LAYERS
CONV_W = 3
D_FF = 4 * D_MODEL
N_HEADS = 16
N_KV = 4
HEAD_DIM = D_MODEL // N_HEADS
GROUP = N_HEADS // N_KV
N_BRANCH = 3
L_CMP = 32
L_SEL = 64
N_SEL = 8
WINDOW = 512
Q_BLOCK = 128
EPS = 1e-6
NEG = -1e30
FORCE = 1e4

kernel_name = 'yoco_shortconv_nsa_step'


def rmsnorm(x, g):
    xf = x.astype(jnp.float32)
    y = xf * lax.rsqrt(jnp.mean(xf * xf, axis=-1, keepdims=True) + EPS)
    return (y * g.astype(jnp.float32)).astype(x.dtype)


def masked_softmax(s, mask):
    p = jax.nn.softmax(jnp.where(mask, s.astype(jnp.float32), NEG), axis=-1)
    return jnp.where(mask, p, 0.0)


def sq_relu_mlp(h, w_up, w_down):
    return jnp.square(jax.nn.relu(h @ w_up)) @ w_down


def short_conv_mixer(h, prev, w_in, conv_w, w_out):
    bg, cg, xin = jnp.split(h @ w_in, 3, axis=-1)
    u = cg * xin
    T = u.shape[1]
    u_ext = jnp.concatenate([prev.astype(u.dtype), u], axis=1)
    z = conv_w[0] * u_ext[:, 0:T] + conv_w[1] * u_ext[:, 1:T + 1] + conv_w[2] * u_ext[:, 2:T + 2]
    return (bg * z) @ w_out, u_ext[:, -(CONV_W - 1):]


def shared_kv_rows(h, kv_norm, w_kv, ks_norm, kw_norm):
    B, T, _ = h.shape
    kv = (rmsnorm(h, kv_norm) @ w_kv).reshape(B, T, N_BRANCH, 2, N_KV, HEAD_DIM)
    cmp_rows = kv[:, :, 0]
    slc_rows = jnp.stack([rmsnorm(kv[:, :, 1, 0], ks_norm), kv[:, :, 1, 1]], axis=2)
    win_rows = jnp.stack([rmsnorm(kv[:, :, 2, 0], kw_norm), kv[:, :, 2, 1]], axis=2)
    return cmp_rows, slc_rows, win_rows


def compress(cmp_rows, kc_norm):
    B, T = cmp_rows.shape[:2]
    nbc = T // L_CMP
    blocks = cmp_rows[:, :nbc * L_CMP].reshape((B, nbc, L_CMP) + cmp_rows.shape[2:])
    blocks = blocks.astype(jnp.float32).mean(axis=2).astype(cmp_rows.dtype)
    return rmsnorm(blocks[:, :, 0], kc_norm), blocks[:, :, 1]


def nsa_attend(q, gates, kc, vc, slc_rows, win_rows, q_start, win_start):
    B, Tq = q.shape[:2]
    T = slc_rows.shape[1]
    nbc = kc.shape[1]
    nbs = -(-T // L_SEL)
    k_sel = min(N_SEL, nbs)
    n_key = k_sel * L_SEL
    qb = Q_BLOCK if Tq % Q_BLOCK == 0 else Tq
    n_chunks = Tq // qb
    scale = HEAD_DIM ** -0.5
    ks = slc_rows[:, :, 0].transpose(0, 2, 1, 3)
    vs = slc_rows[:, :, 1].transpose(0, 2, 1, 3)
    pad = jnp.zeros((B, WINDOW) + win_rows.shape[2:], win_rows.dtype)
    win_pad = jnp.concatenate([pad, win_rows], axis=1)
    cmp_end = (jnp.arange(nbc) + 1) * L_CMP - 1
    blk = jnp.arange(nbs)
    q6 = q.reshape(B, n_chunks, qb, N_KV, GROUP, HEAD_DIM).transpose(1, 0, 2, 3, 4, 5)
    g6 = gates.reshape(B, n_chunks, qb, N_KV, GROUP, N_BRANCH).transpose(1, 0, 2, 3, 4, 5)

    def chunk(args):
        qc, gc, c0 = args
        t = q_start + c0 + jnp.arange(qb)
        s_c = jnp.einsum('bqgrd,bngd->bqgrn', qc, kc) * scale
        m_c = (cmp_end[None, :] <= t[:, None])[None, :, None, None, :]
        p_c = masked_softmax(s_c, m_c)
        o_c = jnp.einsum('bqgrn,bngd->bqgrd', p_c.astype(vc.dtype), vc)
        imp = p_c.sum(axis=3)
        imp = jnp.pad(imp, ((0, 0), (0, 0), (0, 0), (0, 2 * nbs - nbc)))
        imp = imp.reshape(B, qb, N_KV, nbs, 2).sum(-1)
        cur = (t // L_SEL)[:, None]
        forced = (blk[None] == 0) | (blk[None] == cur) | (blk[None] == cur - 1)
        imp = jnp.where(forced[None, :, None, :], FORCE, imp)
        imp = jnp.where((blk[None] > cur)[None, :, None, :], NEG, imp)
        top_v, top_i = lax.top_k(imp, k_sel)
        pos5 = top_i[..., None] * L_SEL + jnp.arange(L_SEL)
        valid = (top_v > 0.5 * NEG)[..., None] & (pos5 <= t[None, :, None, None, None])
        m_s = valid.reshape(B, qb, N_KV, n_key)
        idx = jnp.minimum(pos5.reshape(B, qb, N_KV, n_key), T - 1)
        idx = idx.transpose(0, 2, 1, 3).reshape(B, N_KV, qb * n_key, 1)
        kg = jnp.take_along_axis(ks, idx, axis=2).reshape(B, N_KV, qb, n_key, HEAD_DIM)
        vg = jnp.take_along_axis(vs, idx, axis=2).reshape(B, N_KV, qb, n_key, HEAD_DIM)
        s_s = jnp.einsum('bqgrd,bgqkd->bqgrk', qc, kg) * scale
        p_s = masked_softmax(s_s, m_s[:, :, :, None, :])
        o_s = jnp.einsum('bqgrk,bgqkd->bqgrd', p_s.astype(vg.dtype), vg)
        start = q_start - win_start + c0
        wr = lax.dynamic_slice_in_dim(win_pad, start, qb + WINDOW, axis=1)
        wpos = q_start + c0 - WINDOW + jnp.arange(qb + WINDOW)
        dist = t[:, None] - wpos[None, :]
        m_w = ((wpos >= win_start)[None, :] & (dist >= 0) & (dist <= WINDOW))[None, :, None, None, :]
        s_w = jnp.einsum('bqgrd,bkgd->bqgrk', qc, wr[:, :, 0]) * scale
        p_w = masked_softmax(s_w, m_w)
        o_w = jnp.einsum('bqgrk,bkgd->bqgrd', p_w.astype(wr.dtype), wr[:, :, 1])
        return gc[..., 0:1] * o_c + gc[..., 1:2] * o_s + gc[..., 2:3] * o_w

    out = lax.map(chunk, (q6, g6, jnp.arange(n_chunks, dtype=jnp.int32) * qb))
    return out.transpose(1, 0, 2, 3, 4, 5).reshape(B, Tq, N_HEADS * HEAD_DIM)


def nsa_mixer(h, w_qg, q_norm, w_o, kc, vc, slc_rows, win_rows, q_start, win_start):
    B, T, _ = h.shape
    qg = h @ w_qg
    q = rmsnorm(qg[..., :N_HEADS * HEAD_DIM].reshape(B, T, N_HEADS, HEAD_DIM), q_norm)
    gates = jax.nn.sigmoid(qg[..., N_HEADS * HEAD_DIM:].reshape(B, T, N_HEADS, N_BRANCH))
    o = nsa_attend(q, gates, kc, vc, slc_rows, win_rows, q_start, win_start)
    return o @ w_o


def trunk(x, conv_prev, past_cmp, past_slc, past_win, win_keep,
          norm1, a_w_in, a_conv_w, a_w_out, kv_norm, w_kv, kc_norm, ks_norm, kw_norm,
          b_w_qg, b_q_norm, b_w_o, norm2, w_up, w_down):
    P = past_cmp.shape[1]
    win_start = P - past_win.shape[1]
    conv_new = []
    for l in range(DEPTH):
        h = rmsnorm(x, norm1[l])
        if l < N_A_LAYERS:
            y, st = short_conv_mixer(h, conv_prev[l], a_w_in[l], a_conv_w[l], a_w_out[l])
            conv_new.append(st)
        else:
            j = l - N_A_LAYERS
            y = nsa_mixer(h, b_w_qg[j], b_q_norm[j], b_w_o[j], kc, vc, slc_all, win_all, P, win_start)
        x = x + y
        x = x + sq_relu_mlp(rmsnorm(x, norm2[l]), w_up[l], w_down[l])
        if l == N_A_LAYERS - 1:
            cmp_new, slc_new, win_new = shared_kv_rows(x, kv_norm, w_kv, ks_norm, kw_norm)
            kc, vc = compress(jnp.concatenate([past_cmp, cmp_new], axis=1), kc_norm)
            slc_all = jnp.concatenate([past_slc, slc_new], axis=1)
            win_all = jnp.concatenate([past_win, win_new], axis=1)
    return x, cmp_new, slc_new, win_all[:, -win_keep:], jnp.stack(conv_new)


def setup_inputs(seed: int = 0) -> dict:
    key = jax.random.key(seed)
    ks = jax.random.split(key, 24)
    n_pages = PAST_LEN // PAGE_SIZE
    n_used = DEC_BATCH * n_pages
    n_pool = n_used + n_used // 4
    w_buf = min(WINDOW, PAST_LEN)
    row = (2, N_KV, HEAD_DIM)

    def w(k, shape, fan_in):
        return jax.random.normal(k, shape, jnp.float32) * fan_in ** -0.5

    def gain(k, shape):
        return 1.0 + 0.02 * jax.random.normal(k, shape, jnp.float32)

    page_table = jax.random.permutation(ks[0], n_pool)[:n_used].reshape(DEC_BATCH, n_pages).astype(jnp.int32)
    return {
        'x_prompt': jax.random.normal(ks[1], (BATCH, SEQ, D_MODEL), jnp.float32),
        'x_sample': jax.random.normal(ks[2], (DEC_BATCH, DEC_SEQ, D_MODEL), jnp.float32),
        'cache_cmp_kv': jax.random.normal(ks[3], (n_pool, PAGE_SIZE) + row, jnp.float32),
        'cache_slc_kv': jax.random.normal(ks[4], (n_pool, PAGE_SIZE) + row, jnp.float32),
        'cache_win_kv': jax.random.normal(ks[5], (DEC_BATCH, w_buf) + row, jnp.float32),
        'state_conv': jax.random.normal(ks[6], (N_A_LAYERS, DEC_BATCH, CONV_W - 1, D_MODEL), jnp.float32),
        'page_table': page_table,
        'norm1': gain(ks[7], (DEPTH, D_MODEL)),
        'a_w_in': w(ks[8], (N_A_LAYERS, D_MODEL, 3 * D_MODEL), D_MODEL),
        'a_conv_w': w(ks[9], (N_A_LAYERS, CONV_W, D_MODEL), CONV_W),
        'a_w_out': w(ks[10], (N_A_LAYERS, D_MODEL, D_MODEL), D_MODEL),
        'kv_norm': gain(ks[11], (D_MODEL,)),
        'w_kv': w(ks[12], (D_MODEL, N_BRANCH * 2 * N_KV * HEAD_DIM), D_MODEL),
        'kc_norm': gain(ks[13], (HEAD_DIM,)),
        'ks_norm': gain(ks[14], (HEAD_DIM,)),
        'kw_norm': gain(ks[15], (HEAD_DIM,)),
        'b_w_qg': w(ks[16], (N_B_LAYERS, D_MODEL, N_HEADS * HEAD_DIM + N_HEADS * N_BRANCH), D_MODEL),
        'b_q_norm': gain(ks[17], (N_B_LAYERS, HEAD_DIM)),
        'b_w_o': w(ks[18], (N_B_LAYERS, N_HEADS * HEAD_DIM, D_MODEL), N_HEADS * HEAD_DIM),
        'norm2': gain(ks[19], (DEPTH, D_MODEL)),
        'w_up': w(ks[20], (DEPTH, D_MODEL, D_FF), D_MODEL),
        'w_down': w(ks[21], (DEPTH, D_FF, D_MODEL), D_FF),
    }


def reference(x_prompt, x_sample, cache_cmp_kv, cache_slc_kv, cache_win_kv, state_conv, page_table,
              norm1, a_w_in, a_conv_w, a_w_out, kv_norm, w_kv, kc_norm, ks_norm, kw_norm,
              b_w_qg, b_q_norm, b_w_o, norm2, w_up, w_down):
    weights = (norm1, a_w_in, a_conv_w, a_w_out, kv_norm, w_kv, kc_norm, ks_norm, kw_norm,
               b_w_qg, b_q_norm, b_w_o, norm2, w_up, w_down)
    n_seq, n_pages = page_table.shape
    past_len = n_pages * PAGE_SIZE

    def gather_pages(pool):
        return pool[page_table].reshape((n_seq, past_len) + pool.shape[2:])

    b_p, t_p = x_prompt.shape[:2]
    empty = jnp.zeros((b_p, 0, 2, N_KV, HEAD_DIM), x_prompt.dtype)
    conv0 = jnp.zeros((N_A_LAYERS, b_p, CONV_W - 1, D_MODEL), x_prompt.dtype)
    y_prompt, p_cmp, p_slc, p_win, p_conv = trunk(
        x_prompt, conv0, empty, empty, empty, min(WINDOW, t_p), *weights)
    y_sample, s_cmp, s_slc, s_win, s_conv = trunk(
        x_sample, state_conv, gather_pages(cache_cmp_kv), gather_pages(cache_slc_kv),
        cache_win_kv, cache_win_kv.shape[1], *weights)
    return (y_prompt, y_sample, p_cmp, p_slc, p_win, p_conv, s_cmp, s_slc, s_win, s_conv)
```

```python
import functools

import jax
import jax.numpy as jnp
from jax import lax
from jax.experimental import pallas as pl
from jax.experimental.pallas import tpu as pltpu

F32 = jnp.float32
BF16 = jnp.bfloat16

N_HEADS = 16
N_KV = 4
HEAD_DIM = 64
GROUP = N_HEADS // N_KV
N_BRANCH = 3
L_CMP = 32
L_SEL = 64
N_SEL = 8
WINDOW = 512
Q_BLOCK = 128
CONV_W = 3
EPS = 1e-6
NEG = -1e30
FORCE = 1e4

KV_LANES = N_KV * HEAD_DIM
ROW_LANES = 2 * KV_LANES
LANE = 128
ROW_TILE = 512
FF_CHUNK = 1024
SEL_TILE = 512
VMEM_LIMIT = 56 * 1024 * 1024


def _params(*sem):
    return pltpu.CompilerParams(dimension_semantics=sem, vmem_limit_bytes=VMEM_LIMIT)


def _dot(a, b):
    return jnp.dot(a, b, preferred_element_type=F32)


def _dot_nt(a, b):
    return lax.dot_general(a, b, (((1,), (1,)), ((), ())), preferred_element_type=F32)


def _rms(x, g):
    return x * lax.rsqrt(jnp.mean(x * x, axis=-1, keepdims=True) + EPS) * g


def _split_dot(x, m):
    hi = x.astype(BF16)
    lo = (x - hi.astype(F32)).astype(BF16)
    return _dot(hi, m) + _dot(lo, m)


def _head_rms(k, ones_bd, gain):
    ss = _split_dot(k * k, ones_bd)
    return k * lax.rsqrt(ss * (1.0 / HEAD_DIM) + EPS) * gain


def _softmax_rows(s, mask):
    s = jnp.where(mask, s, NEG)
    e = jnp.exp(s - jnp.max(s, axis=-1, keepdims=True))
    return jnp.where(mask, e, 0.0) / jnp.sum(e, axis=-1, keepdims=True)


def _conv_core(x, g, win_ref, cw, wout_ref, s1_fix, s2_fix):
    d = x.shape[1]
    h = _rms(x, g).astype(BF16)
    bg = _dot(h, win_ref[:, 0:d])
    cg = _dot(h, win_ref[:, d:2 * d])
    xin = _dot(h, win_ref[:, 2 * d:3 * d])
    u = cg * xin
    s1 = s1_fix(pltpu.roll(u, 1, 0))
    s2 = s2_fix(pltpu.roll(u, 2, 0))
    z = cw[0:1] * s2 + cw[1:2] * s1 + cw[2:3] * u
    y = _dot((bg * z).astype(BF16), wout_ref[...])
    return x + y, u


def _conv_long_kernel(x_ref, g_ref, win_ref, cw_ref, wout_ref, o_ref, tail_ref, carry_ref, *, tiles_per_seq):
    tm = x_ref.shape[0]

    @pl.when(pl.program_id(0) % tiles_per_seq == 0)
    def _():
        carry_ref[...] = jnp.zeros_like(carry_ref)

    prev = carry_ref[...]
    row = lax.broadcasted_iota(jnp.int32, (tm, 1), 0)
    s1_fix = lambda r: jnp.where(row == 0, prev[7:8], r)
    s2_fix = lambda r: jnp.where(row == 0, prev[6:7], jnp.where(row == 1, prev[7:8], r))
    out, u = _conv_core(x_ref[...], g_ref[...], win_ref, cw_ref[...], wout_ref, s1_fix, s2_fix)
    o_ref[...] = out
    carry_ref[...] = u[tm - 8:tm]
    tail_ref[0] = u[tm - 8:tm]


def _conv_short_kernel(x_ref, g_ref, win_ref, cw_ref, wout_ref, pa_ref, pb_ref, o_ref, u_ref, *, seg):
    tm = x_ref.shape[0]
    pos = lax.broadcasted_iota(jnp.int32, (tm, 1), 0) % seg
    s1_fix = lambda r: jnp.where(pos < 1, pa_ref[...], r)
    s2_fix = lambda r: jnp.where(pos < 2, pb_ref[...], r)
    out, u = _conv_core(x_ref[...], g_ref[...], win_ref, cw_ref[...], wout_ref, s1_fix, s2_fix)
    o_ref[...] = out
    u_ref[...] = u


def _conv_layer(x, g, w_in, cw, w_out, *, seq_len, state=None):
    n, d = x.shape
    tm = min(ROW_TILE, n)
    assert n % tm == 0
    n_seq = n // seq_len
    row_spec = pl.BlockSpec((tm, d), lambda i: (i, 0))
    w_specs = [pl.BlockSpec((1, d), lambda i: (0, 0)),
               pl.BlockSpec((d, 3 * d), lambda i: (0, 0)),
               pl.BlockSpec((CONV_W, d), lambda i: (0, 0)),
               pl.BlockSpec((d, d), lambda i: (0, 0))]
    if state is None:
        assert seq_len % tm == 0 and seq_len >= CONV_W - 1
        tiles_per_seq = seq_len // tm
        out, tail = pl.pallas_call(
            functools.partial(_conv_long_kernel, tiles_per_seq=tiles_per_seq),
            grid=(n // tm,),
            in_specs=[row_spec] + w_specs,
            out_specs=[row_spec, pl.BlockSpec((1, 8, d), lambda i: (i, 0, 0))],
            out_shape=[jax.ShapeDtypeStruct((n, d), F32), jax.ShapeDtypeStruct((n // tm, 8, d), F32)],
            scratch_shapes=[pltpu.VMEM((8, d), F32)],
            compiler_params=_params("arbitrary"),
            name="conv_long",
        )(x, g, w_in, cw, w_out)
        new_state = tail.reshape(n_seq, tiles_per_seq, 8, d)[:, -1, 8 - (CONV_W - 1):]
        return out, new_state
    assert tm % seq_len == 0 and seq_len >= CONV_W - 1
    zeros = jnp.zeros((n_seq, seq_len, d), F32)
    pa = zeros.at[:, 0].set(state[:, 1]).reshape(n, d)
    pb = zeros.at[:, 0].set(state[:, 0]).at[:, 1].set(state[:, 1]).reshape(n, d)
    out, u = pl.pallas_call(
        functools.partial(_conv_short_kernel, seg=seq_len),
        grid=(n // tm,),
        in_specs=[row_spec] + w_specs + [row_spec, row_spec],
        out_specs=[row_spec, row_spec],
        out_shape=[jax.ShapeDtypeStruct((n, d), F32), jax.ShapeDtypeStruct((n, d), F32)],
        compiler_params=_params("arbitrary"),
        name="conv_short",
    )(x, g, w_in, cw, w_out, pa, pb)
    new_state = u.reshape(n_seq, seq_len, d)[:, seq_len - (CONV_W - 1):]
    return out, new_state


def _mlp_body(x, g_ref, wup_ref, wdn_ref, o_ref):
    h = _rms(x, g_ref[...]).astype(BF16)
    acc = x
    for c in range(wup_ref.shape[1] // FF_CHUNK):
        a = _dot(h, wup_ref[:, c * FF_CHUNK:(c + 1) * FF_CHUNK])
        a = jnp.square(jnp.maximum(a, 0.0)).astype(BF16)
        acc = acc + _dot(a, wdn_ref[c * FF_CHUNK:(c + 1) * FF_CHUNK, :])
    o_ref[...] = acc


def _mlp_kernel(x_ref, g_ref, wup_ref, wdn_ref, o_ref):
    _mlp_body(x_ref[...], g_ref, wup_ref, wdn_ref, o_ref)


def _proj_mlp_kernel(x_ref, a_ref, wo_ref, g_ref, wup_ref, wdn_ref, o_ref):
    _mlp_body(x_ref[...] + _dot(a_ref[...], wo_ref[...]), g_ref, wup_ref, wdn_ref, o_ref)


def _mlp_layer(x, g, w_up, w_down, attn=None, w_o=None):
    n, d = x.shape
    ff = w_up.shape[1]
    assert ff % FF_CHUNK == 0
    tm = min(ROW_TILE, n)
    assert n % tm == 0
    row_spec = pl.BlockSpec((tm, d), lambda i: (i, 0))
    w_specs = [pl.BlockSpec((1, d), lambda i: (0, 0)),
               pl.BlockSpec((d, ff), lambda i: (0, 0)),
               pl.BlockSpec((ff, d), lambda i: (0, 0))]
    if attn is None:
        kern, in_specs, args = _mlp_kernel, [row_spec] + w_specs, (x, g, w_up, w_down)
    else:
        kern = _proj_mlp_kernel
        in_specs = [row_spec, row_spec, pl.BlockSpec((d, d), lambda i: (0, 0))] + w_specs
        args = (x, attn, w_o, g, w_up, w_down)
    return pl.pallas_call(
        kern, grid=(n // tm,), in_specs=in_specs, out_specs=row_spec,
        out_shape=jax.ShapeDtypeStruct((n, d), F32),
        compiler_params=_params("arbitrary"),
        name="mlp" if attn is None else "proj_mlp",
    )(*args)


def _kv_kernel(x_ref, g_ref, w_ref, ones_ref, ks_ref, kw_ref, cmp_ref, slc_ref, win_ref, slcb_ref, winb_ref):
    h = _rms(x_ref[...], g_ref[...]).astype(BF16)
    kv = _dot(h, w_ref[...])
    ones_bd = ones_ref[...]
    cmp_ref[...] = kv[:, 0:ROW_LANES]
    for br, (gain_ref, f_ref, b_ref) in enumerate(((ks_ref, slc_ref, slcb_ref), (kw_ref, win_ref, winb_ref)), start=1):
        base = br * ROW_LANES
        k = _head_rms(kv[:, base:base + KV_LANES], ones_bd, gain_ref[...])
        v = kv[:, base + KV_LANES:base + ROW_LANES]
        f_ref[:, 0:KV_LANES] = k
        f_ref[:, KV_LANES:ROW_LANES] = v
        b_ref[:, 0:KV_LANES] = k.astype(BF16)
        b_ref[:, KV_LANES:ROW_LANES] = v.astype(BF16)


def _kv_rows(x, kv_norm, w_kv, ones_bd, ks_gain, kw_gain):
    n, d = x.shape
    tm = min(ROW_TILE, n)
    assert n % tm == 0
    row_spec = pl.BlockSpec((tm, d), lambda i: (i, 0))
    kv_spec = pl.BlockSpec((tm, ROW_LANES), lambda i: (i, 0))
    const = lambda shape: pl.BlockSpec(shape, lambda i: (0, 0))
    return pl.pallas_call(
        _kv_kernel, grid=(n // tm,),
        in_specs=[row_spec, const((1, d)), const(w_kv.shape), const((KV_LANES, KV_LANES)),
                  const((1, KV_LANES)), const((1, KV_LANES))],
        out_specs=[kv_spec] * 5,
        out_shape=[jax.ShapeDtypeStruct((n, ROW_LANES), F32)] * 3 + [jax.ShapeDtypeStruct((n, ROW_LANES), BF16)] * 2,
        compiler_params=_params("arbitrary"),
        name="kv_rows",
    )(x, kv_norm, w_kv, ones_bd, ks_gain, kw_gain)


def _compress_rows_kernel(x_ref, ones_ref, gain_ref, kc_ref, vc_ref):
    rows = x_ref.shape[0]
    blocks = jnp.mean(x_ref[...].reshape(rows // L_CMP, L_CMP, ROW_LANES), axis=1)
    kc_ref[...] = _head_rms(blocks[:, 0:KV_LANES], ones_ref[...], gain_ref[...])
    vc_ref[...] = blocks[:, KV_LANES:ROW_LANES]


def _compress_rows(cmp_rows, ones_bd, kc_gain):
    n = cmp_rows.shape[0]
    tm = min(ROW_TILE, n)
    assert n % tm == 0 and tm % (8 * L_CMP) == 0
    nb = tm // L_CMP
    const = lambda shape: pl.BlockSpec(shape, lambda i: (0, 0))
    out_spec = pl.BlockSpec((nb, KV_LANES), lambda i: (i, 0))
    return pl.pallas_call(
        _compress_rows_kernel, grid=(n // tm,),
        in_specs=[pl.BlockSpec((tm, ROW_LANES), lambda i: (i, 0)), const((KV_LANES, KV_LANES)), const((1, KV_LANES))],
        out_specs=[out_spec, out_spec],
        out_shape=[jax.ShapeDtypeStruct((n // L_CMP, KV_LANES), F32)] * 2,
        compiler_params=_params("arbitrary"),
        name="compress_rows",
    )(cmp_rows, ones_bd, kc_gain)


def _compress_pages_kernel(pt_ref, *refs, n_pages):
    page_refs = refs[:n_pages]
    ones_ref, gain_ref, kc_ref, vc_ref, blk_ref = refs[n_pages:]
    page = page_refs[0].shape[1]
    per = page // L_CMP
    for p in range(n_pages):
        blk_ref[p * per:(p + 1) * per, :] = jnp.mean(page_refs[p][0].reshape(per, L_CMP, ROW_LANES), axis=1)
    blocks = blk_ref[...]
    kc_ref[0] = _head_rms(blocks[:, 0:KV_LANES], ones_ref[...], gain_ref[...])
    vc_ref[0] = blocks[:, KV_LANES:ROW_LANES]


def _page_specs(n_pages, page):
    return [pl.BlockSpec((1, page, ROW_LANES), lambda b, pt, p=p: (pt[b, p], 0, 0)) for p in range(n_pages)]


def _compress_pages(pool, page_table, ones_bd, kc_gain):
    n_seq, n_pages = page_table.shape
    page = pool.shape[1]
    assert page % L_CMP == 0
    nbc = n_pages * page // L_CMP
    const = lambda shape: pl.BlockSpec(shape, lambda b, pt: (0, 0))
    out_spec = pl.BlockSpec((1, nbc, KV_LANES), lambda b, pt: (b, 0, 0))
    return pl.pallas_call(
        functools.partial(_compress_pages_kernel, n_pages=n_pages),
        grid_spec=pltpu.PrefetchScalarGridSpec(
            num_scalar_prefetch=1, grid=(n_seq,),
            in_specs=_page_specs(n_pages, page) + [const((KV_LANES, KV_LANES)), const((1, KV_LANES))],
            out_specs=[out_spec, out_spec],
            scratch_shapes=[pltpu.VMEM((nbc, ROW_LANES), F32)]),
        out_shape=[jax.ShapeDtypeStruct((n_seq, nbc, KV_LANES), F32)] * 2,
        compiler_params=_params("arbitrary"),
        name="compress_pages",
    )(page_table, *([pool] * n_pages), ones_bd, kc_gain)


def _even_odd(blocks, half):
    b, nbc, w = blocks.shape
    out = jnp.zeros((b, 2 * half, w), BF16)
    out = out.at[:, 0:(nbc + 1) // 2].set(blocks[:, 0::2].astype(BF16))
    return out.at[:, half:half + nbc // 2].set(blocks[:, 1::2].astype(BF16))


def _qg_kernel(x_ref, g_ref, wq_ref, wg_ref, ones_ref, qgain_ref, q_ref, gate_ref):
    h = _rms(x_ref[...], g_ref[...]).astype(BF16)
    q = _dot(h, wq_ref[...])
    ones_bd = ones_ref[...]
    gain = qgain_ref[...] * (HEAD_DIM ** -0.5)
    for s in range(q.shape[1] // KV_LANES):
        sl = slice(s * KV_LANES, (s + 1) * KV_LANES)
        q_ref[:, sl] = _head_rms(q[:, sl], ones_bd, gain).astype(BF16)
    gate_ref[...] = jax.nn.sigmoid(_dot(h, wg_ref[...]))


def _qg_proj(x, g, w_q, w_g, ones_bd, q_gain):
    n, d = x.shape
    tm = min(ROW_TILE, n)
    assert n % tm == 0
    nq = w_q.shape[1]
    const = lambda shape: pl.BlockSpec(shape, lambda i: (0, 0))
    return pl.pallas_call(
        _qg_kernel, grid=(n // tm,),
        in_specs=[pl.BlockSpec((tm, d), lambda i: (i, 0)), const((1, d)), const(w_q.shape), const(w_g.shape),
                  const((KV_LANES, KV_LANES)), const((1, KV_LANES))],
        out_specs=[pl.BlockSpec((tm, nq), lambda i: (i, 0)), pl.BlockSpec((tm, LANE), lambda i: (i, 0))],
        out_shape=[jax.ShapeDtypeStruct((n, nq), BF16), jax.ShapeDtypeStruct((n, LANE), F32)],
        compiler_params=_params("arbitrary"),
        name="qg_proj",
    )(x, g, w_q, w_g, ones_bd, q_gain)


def _group_queries(q_slabs, lane_group, g):
    return jnp.concatenate([jnp.where(lane_group == g, s, jnp.zeros_like(s)) for s in q_slabs], axis=0)


def _cmp_mask(t, half, nbc, n_cols):
    col = lax.broadcasted_iota(jnp.int32, (1, n_cols), 1)
    blk = 2 * (col % half) + col // half
    return ((blk + 1) * L_CMP - 1 <= t) & (blk < nbc)


def _select_blocks(imp, t, k_sel):
    blk = lax.broadcasted_iota(jnp.int32, imp.shape, 1)
    cur = t // L_SEL
    forced = (blk == 0) | (blk == cur) | (blk == cur - 1)
    work = jnp.where(forced, FORCE, imp)
    work = jnp.where(blk > cur, NEG, work)
    sel = jnp.zeros(imp.shape, jnp.bool_)
    blk_f = blk.astype(F32)
    for _ in range(k_sel):
        m = jnp.max(work, axis=-1, keepdims=True)
        first = jnp.min(jnp.where(work == m, blk_f, float(imp.shape[1])), axis=-1, keepdims=True)
        pick = blk_f == first
        sel = sel | (pick & (m > 0.5 * NEG))
        work = jnp.where(pick, -jnp.inf, work)
    return sel


def _merge_heads(parts, gate_slabs, lane_group, o_ref):
    tq = gate_slabs[0][0].shape[0]
    for r in range(GROUP):
        acc = None
        for br in range(N_BRANCH):
            o = jnp.zeros((tq, KV_LANES), F32)
            for g in range(N_KV):
                o = jnp.where(lane_group == g, parts[g][br][r * tq:(r + 1) * tq], o)
            term = gate_slabs[br][r] * o
            acc = term if acc is None else acc + term
        o_ref[0, :, r * KV_LANES:(r + 1) * KV_LANES] = acc.astype(o_ref.dtype)


def _expand_gates(gates, ge_ref):
    out = []
    for br in range(N_BRANCH):
        full = _split_dot(gates, ge_ref[br])
        out.append([full[:, r * KV_LANES:(r + 1) * KV_LANES] for r in range(GROUP)])
    return out


def _attn_long_kernel(q_ref, gate_ref, kc_ref, vc_ref, slc_ref, win_ref, e_ref, ge_ref, o_ref,
                      m_ref, l_ref, acc_ref, *, nbc, half, k_sel):
    tq = q_ref.shape[1]
    c0 = pl.program_id(1) * tq
    rows = GROUP * tq
    t1 = c0 + lax.broadcasted_iota(jnp.int32, (tq, 1), 0)
    t = jnp.concatenate([t1] * GROUP, axis=0)
    lane_group = lax.broadcasted_iota(jnp.int32, (1, KV_LANES), 1) // HEAD_DIM
    q_slabs = [q_ref[0, :, r * KV_LANES:(r + 1) * KV_LANES] for r in range(GROUP)]
    qe = [_group_queries(q_slabs, lane_group, g) for g in range(N_KV)]

    kc = kc_ref[0]
    vc = vc_ref[0]
    cmask = _cmp_mask(t, half, nbc, 2 * half)
    parts = [[None] * N_BRANCH for _ in range(N_KV)]
    imps = []
    for g in range(N_KV):
        p = _softmax_rows(_dot_nt(qe[g], kc), cmask)
        parts[g][0] = _dot(p.astype(BF16), vc)
        ph = p[0:tq]
        for r in range(1, GROUP):
            ph = ph + p[r * tq:(r + 1) * tq]
        imps.append(ph[:, 0:half] + ph[:, half:2 * half])
    sel = _select_blocks(jnp.concatenate(imps, axis=0), jnp.concatenate([t1] * N_KV, axis=0), k_sel)
    sel_bias = jnp.where(sel, 0.0, NEG).astype(BF16)

    last = (c0 + tq - 1) // SEL_TILE
    for g in range(N_KV):
        bias_g = sel_bias[g * tq:(g + 1) * tq]
        m_ref[...] = jnp.full(m_ref.shape, NEG, F32)
        l_ref[...] = jnp.zeros(l_ref.shape, F32)
        acc_ref[...] = jnp.zeros(acc_ref.shape, F32)

        def sweep(kt, causal, qe_g=qe[g], bias_g=bias_g):
            k0 = pl.multiple_of(kt * SEL_TILE, SEL_TILE)
            kv = slc_ref[0, pl.ds(k0, SEL_TILE), :]
            s = _dot_nt(qe_g, kv[:, 0:KV_LANES])
            bias = _dot(bias_g, e_ref[kt])
            s = (s.reshape(GROUP, tq, SEL_TILE) + bias[None]).reshape(rows, SEL_TILE)
            if causal:
                kpos = k0 + lax.broadcasted_iota(jnp.int32, (1, SEL_TILE), 1)
                s = jnp.where(kpos <= t, s, NEG)
            m_old = m_ref[...]
            m_new = jnp.maximum(m_old, jnp.max(s, axis=-1, keepdims=True))
            alpha = jnp.exp(m_old - m_new)
            p = jnp.exp(s - m_new)
            l_ref[...] = alpha * l_ref[...] + jnp.sum(p, axis=-1, keepdims=True)
            acc_ref[...] = alpha * acc_ref[...] + _dot(p.astype(BF16), kv[:, KV_LANES:ROW_LANES])
            m_ref[...] = m_new

        def body(kt, carry):
            sweep(kt, False)
            return carry

        lax.fori_loop(0, last, body, 0)
        sweep(last, True)
        parts[g][1] = acc_ref[...] / l_ref[...]

    band = tq + WINDOW
    w0 = pl.multiple_of(jnp.maximum(c0 - WINDOW, 0), tq)
    wkv = win_ref[0, pl.ds(w0, band), :]
    wpos = w0 + lax.broadcasted_iota(jnp.int32, (1, band), 1)
    dist = t - wpos
    wmask = (dist >= 0) & (dist <= WINDOW)
    for g in range(N_KV):
        p = _softmax_rows(_dot_nt(qe[g], wkv[:, 0:KV_LANES]), wmask)
        parts[g][2] = _dot(p.astype(BF16), wkv[:, KV_LANES:ROW_LANES])

    _merge_heads(parts, _expand_gates(gate_ref[0], ge_ref), lane_group, o_ref)


def _attn_long(q, gates, kc, vc, slc_b, win_b, e_sel, ge, *, nbc, half, k_sel):
    b, t_len, nq = q.shape
    tq = Q_BLOCK
    assert t_len % SEL_TILE == 0 and SEL_TILE % tq == 0 and t_len >= tq + WINDOW and WINDOW % tq == 0
    rows = GROUP * tq
    per_b = lambda shape: pl.BlockSpec((1,) + shape, lambda i, c: (i, 0, 0))
    const3 = lambda shape: pl.BlockSpec(shape, lambda i, c: (0, 0, 0))
    chunk = lambda w: pl.BlockSpec((1, tq, w), lambda i, c: (i, c, 0))
    return pl.pallas_call(
        functools.partial(_attn_long_kernel, nbc=nbc, half=half, k_sel=k_sel),
        grid=(b, t_len // tq),
        in_specs=[chunk(nq), chunk(LANE), per_b((2 * half, KV_LANES)), per_b((2 * half, KV_LANES)),
                  per_b((t_len, ROW_LANES)), per_b((t_len, ROW_LANES)), const3(e_sel.shape), const3(ge.shape)],
        out_specs=chunk(nq),
        out_shape=jax.ShapeDtypeStruct((b, t_len, nq), BF16),
        scratch_shapes=[pltpu.VMEM((rows, 1), F32), pltpu.VMEM((rows, 1), F32), pltpu.VMEM((rows, KV_LANES), F32)],
        compiler_params=_params("arbitrary", "arbitrary"),
        name="attn_long",
    )(q, gates, kc, vc, slc_b, win_b, e_sel, ge)


def _attn_paged_kernel(pt_ref, *refs, n_pages, past, w_buf, nbc, half, k_sel):
    page_refs = refs[:n_pages]
    (q_ref, gate_ref, kc_ref, vc_ref, snew_ref, wold_ref, wnew_ref, e_ref, ge_ref, o_ref) = refs[n_pages:]
    tq = q_ref.shape[1]
    page = page_refs[0].shape[1]
    tok = lax.broadcasted_iota(jnp.int32, (tq, 1), 0)
    t1 = past + tok
    t = jnp.concatenate([t1] * (N_KV * GROUP), axis=0)
    lane_group = lax.broadcasted_iota(jnp.int32, (1, KV_LANES), 1) // HEAD_DIM
    q_slabs = [q_ref[0, :, r * KV_LANES:(r + 1) * KV_LANES].astype(F32) for r in range(GROUP)]
    qe = jnp.concatenate([_group_queries(q_slabs, lane_group, g) for g in range(N_KV)], axis=0).astype(BF16)
    rows_g = GROUP * tq

    def pad_rows(x):
        return jnp.concatenate([x, jnp.zeros((LANE - x.shape[0], x.shape[1]), x.dtype)], axis=0)

    p = _softmax_rows(_dot_nt(qe, kc_ref[0]), _cmp_mask(t, half, nbc, 2 * half))
    o_cmp = _dot(p.astype(BF16), vc_ref[0])
    imps = []
    for g in range(N_KV):
        ph = p[g * rows_g:g * rows_g + tq]
        for r in range(1, GROUP):
            ph = ph + p[g * rows_g + r * tq:g * rows_g + (r + 1) * tq]
        imps.append(ph[:, 0:half] + ph[:, half:2 * half])
    sel = _select_blocks(jnp.concatenate(imps, axis=0), jnp.concatenate([t1] * N_KV, axis=0), k_sel)
    sel_bias = jnp.where(sel, 0.0, NEG)
    sel_bias = jnp.concatenate([sel_bias[g * tq:(g + 1) * tq] for g in range(N_KV) for _ in range(GROUP)], axis=0)

    snew = pad_rows(snew_ref[0])
    keys = [page_refs[i][0, :, 0:KV_LANES].astype(BF16) for i in range(n_pages)] + [snew[:, 0:KV_LANES].astype(BF16)]
    s = jnp.concatenate([_dot_nt(qe, k) for k in keys], axis=1) + _dot(sel_bias.astype(BF16), e_ref[...])
    kpos = lax.broadcasted_iota(jnp.int32, (1, s.shape[1]), 1)
    p = _softmax_rows(s, kpos <= t).astype(BF16)
    o_sel = _dot(p[:, n_pages * page:], snew[:, KV_LANES:ROW_LANES].astype(BF16))
    for i in range(n_pages):
        o_sel = o_sel + _dot(p[:, i * page:(i + 1) * page], page_refs[i][0, :, KV_LANES:ROW_LANES].astype(BF16))

    wold = wold_ref[0]
    wnew = pad_rows(wnew_ref[0])
    s = jnp.concatenate([_dot_nt(qe, wold[:, 0:KV_LANES].astype(BF16)),
                         _dot_nt(qe, wnew[:, 0:KV_LANES].astype(BF16))], axis=1)
    j = lax.broadcasted_iota(jnp.int32, (1, s.shape[1]), 1)
    dist = t - (past - w_buf + j)
    p = _softmax_rows(s, (dist >= 0) & (dist <= WINDOW) & (j < w_buf + tq)).astype(BF16)
    o_win = (_dot(p[:, 0:w_buf], wold[:, KV_LANES:ROW_LANES].astype(BF16))
             + _dot(p[:, w_buf:], wnew[:, KV_LANES:ROW_LANES].astype(BF16)))

    parts = [[o[g * rows_g:(g + 1) * rows_g] for o in (o_cmp, o_sel, o_win)] for g in range(N_KV)]
    _merge_heads(parts, _expand_gates(gate_ref[0], ge_ref), lane_group, o_ref)


def _attn_paged(q, gates, kc, vc, pool, page_table, slc_new, win_old, win_new, e_sel, ge, *, nbc, half, k_sel):
    n_seq, tq, nq = q.shape
    n_pages = page_table.shape[1]
    page = pool.shape[1]
    past = n_pages * page
    w_buf = win_old.shape[1]
    assert page == LANE and w_buf % LANE == 0 and tq % 8 == 0 and tq <= LANE
    per_seq = lambda shape: pl.BlockSpec((1,) + shape, lambda b, pt: (b, 0, 0))
    const = lambda shape: pl.BlockSpec(shape, lambda b, pt: (0,) * len(shape))
    return pl.pallas_call(
        functools.partial(_attn_paged_kernel, n_pages=n_pages, past=past, w_buf=w_buf, nbc=nbc, half=half, k_sel=k_sel),
        grid_spec=pltpu.PrefetchScalarGridSpec(
            num_scalar_prefetch=1, grid=(n_seq,),
            in_specs=_page_specs(n_pages, page) + [
                per_seq((tq, nq)), per_seq((tq, LANE)), per_seq((2 * half, KV_LANES)), per_seq((2 * half, KV_LANES)),
                per_seq((tq, ROW_LANES)), per_seq((w_buf, ROW_LANES)), per_seq((tq, ROW_LANES)),
                const(e_sel.shape), const(ge.shape)],
            out_specs=per_seq((tq, nq))),
        out_shape=jax.ShapeDtypeStruct((n_seq, tq, nq), BF16),
        compiler_params=_params("arbitrary"),
        name="attn_paged",
    )(page_table, *([pool] * n_pages), q, gates, kc, vc, slc_new, win_old, win_new, e_sel, ge)


def _round_up(x, m):
    return -(-x // m) * m


def _block_onehot(n_rows, n_keys, t_valid):
    key = jnp.arange(n_keys)[None, :]
    return ((key // L_SEL == jnp.arange(n_rows)[:, None]) & (key < t_valid)).astype(BF16)


def kernel(x_prompt, x_sample, cache_cmp_kv, cache_slc_kv, cache_win_kv, state_conv, page_table, norm1, a_w_in, a_conv_w, a_w_out, kv_norm, w_kv, kc_norm, ks_norm, kw_norm, b_w_qg, b_q_norm, b_w_o, norm2, w_up, w_down):
    bp, tp, d = x_prompt.shape
    bs, ts, _ = x_sample.shape
    depth = norm1.shape[0]
    n_a = a_w_in.shape[0]
    n_pool, page = cache_cmp_kv.shape[:2]
    n_pages = page_table.shape[1]
    past = n_pages * page
    w_buf = cache_win_kv.shape[1]
    row = cache_cmp_kv.shape[2:]
    nq = N_HEADS * HEAD_DIM
    assert d == nq and past % L_CMP == 0 and ts < L_CMP and tp % L_CMP == 0

    hperm = jnp.array([(g * GROUP + r) * HEAD_DIM + dd for r in range(GROUP) for g in range(N_KV) for dd in range(HEAD_DIM)])
    lane = jnp.arange(KV_LANES)
    ones_bd = (lane[:, None] // HEAD_DIM == lane[None, :] // HEAD_DIM).astype(BF16)
    tile4 = lambda v: jnp.tile(v.astype(F32), N_KV)[None, :]
    head_of_lane = (jnp.arange(nq) // HEAD_DIM % N_KV) * GROUP + jnp.arange(nq) // KV_LANES
    gcol = jnp.arange(LANE)
    ge = jnp.stack([(gcol[:, None] == head_of_lane[None, :] * N_BRANCH + br) for br in range(N_BRANCH)]).astype(BF16)
    w_in_b = a_w_in.astype(BF16)
    w_out_b = a_w_out.astype(BF16)
    w_up_b = w_up.astype(BF16)
    w_down_b = w_down.astype(BF16)
    w_kv_b = w_kv.astype(BF16)
    w_q_b = b_w_qg[:, :, :nq][:, :, hperm].astype(BF16)
    w_g_b = jnp.pad(b_w_qg[:, :, nq:], ((0, 0), (0, 0), (0, LANE - N_HEADS * N_BRANCH))).astype(BF16)
    w_o_b = b_w_o[:, hperm, :].astype(BF16)
    g1 = norm1.astype(F32)[:, None, :]
    g2 = norm2.astype(F32)[:, None, :]
    kvn = kv_norm.astype(F32)[None, :]

    xp = x_prompt.reshape(bp * tp, d)
    xs = x_sample.reshape(bs * ts, d)
    p_conv, s_conv = [], []
    for l in range(n_a):
        xp, st = _conv_layer(xp, g1[l], w_in_b[l], a_conv_w[l], w_out_b[l], seq_len=tp)
        p_conv.append(st)
        xs, st = _conv_layer(xs, g1[l], w_in_b[l], a_conv_w[l], w_out_b[l], seq_len=ts, state=state_conv[l])
        s_conv.append(st)
        xp = _mlp_layer(xp, g2[l], w_up_b[l], w_down_b[l])
        xs = _mlp_layer(xs, g2[l], w_up_b[l], w_down_b[l])

    ks_g, kw_g, kc_g = tile4(ks_norm), tile4(kw_norm), tile4(kc_norm)
    p_cmp, p_slc, p_win, p_slc_b, p_win_b = _kv_rows(xp, kvn, w_kv_b, ones_bd, ks_g, kw_g)
    s_cmp, s_slc, s_win, _, _ = _kv_rows(xs, kvn, w_kv_b, ones_bd, ks_g, kw_g)

    nbc_p = tp // L_CMP
    nbs_p = -(-tp // L_SEL)
    half_p = _round_up(nbs_p, LANE)
    kc_p, vc_p = _compress_rows(p_cmp, ones_bd, kc_g)
    kc_p = _even_odd(kc_p.reshape(bp, nbc_p, KV_LANES), half_p)
    vc_p = _even_odd(vc_p.reshape(bp, nbc_p, KV_LANES), half_p)
    e_p = _block_onehot(half_p, tp, tp).reshape(half_p, tp // SEL_TILE, SEL_TILE).transpose(1, 0, 2)

    t_all = past + ts
    nbc_s = t_all // L_CMP
    assert nbc_s * L_CMP == past
    nbs_s = -(-t_all // L_SEL)
    half_s = _round_up(nbs_s, LANE)
    pool_cmp = cache_cmp_kv.reshape(n_pool, page, ROW_LANES)
    pool_slc = cache_slc_kv.reshape(n_pool, page, ROW_LANES)
    kc_s, vc_s = _compress_pages(pool_cmp, page_table, ones_bd, kc_g)
    kc_s = _even_odd(kc_s, half_s)
    vc_s = _even_odd(vc_s, half_s)
    e_s = _block_onehot(half_s, past + LANE, t_all)
    win_old = cache_win_kv.reshape(bs, w_buf, ROW_LANES)

    for j in range(depth - n_a):
        l = n_a + j
        qg = tile4(b_q_norm[j])
        q, gates = _qg_proj(xp, g1[l], w_q_b[j], w_g_b[j], ones_bd, qg)
        o = _attn_long(q.reshape(bp, tp, nq), gates.reshape(bp, tp, LANE), kc_p, vc_p,
                       p_slc_b.reshape(bp, tp, ROW_LANES), p_win_b.reshape(bp, tp, ROW_LANES), e_p, ge,
                       nbc=nbc_p, half=half_p, k_sel=min(N_SEL, nbs_p))
        xp = _mlp_layer(xp, g2[l], w_up_b[l], w_down_b[l], attn=o.reshape(bp * tp, nq), w_o=w_o_b[j])

        q, gates = _qg_proj(xs, g1[l], w_q_b[j], w_g_b[j], ones_bd, qg)
        o = _attn_paged(q.reshape(bs, ts, nq), gates.reshape(bs, ts, LANE), kc_s, vc_s, pool_slc, page_table,
                        s_slc.reshape(bs, ts, ROW_LANES), win_old, s_win.reshape(bs, ts, ROW_LANES), e_s, ge,
                        nbc=nbc_s, half=half_s, k_sel=min(N_SEL, nbs_s))
        xs = _mlp_layer(xs, g2[l], w_up_b[l], w_down_b[l], attn=o.reshape(bs * ts, nq), w_o=w_o_b[j])

    rows5 = lambda a, b_, t_: a.reshape((b_, t_) + row)
    p_win_rows = rows5(p_win, bp, tp)
    s_win_all = jnp.concatenate([cache_win_kv, rows5(s_win, bs, ts)], axis=1)
    return (xp.reshape(bp, tp, d), xs.reshape(bs, ts, d),
            rows5(p_cmp, bp, tp), rows5(p_slc, bp, tp), p_win_rows[:, -min(WINDOW, tp):], jnp.stack(p_conv),
            rows5(s_cmp, bs, ts), rows5(s_slc, bs, ts), s_win_all[:, -w_buf:], jnp.stack(s_conv))
```

```python
import functools

import jax
import jax.numpy as jnp
from jax import lax
from jax.experimental import pallas as pl
from jax.experimental.pallas import tpu as pltpu

F32 = jnp.float32
BF16 = jnp.bfloat16

N_HEADS = 16
N_KV = 4
HEAD_DIM = 64
GROUP = N_HEADS // N_KV
N_BRANCH = 3
L_CMP = 32
L_SEL = 64
N_SEL = 8
WINDOW = 512
Q_BLOCK = 128
CONV_W = 3
EPS = 1e-6
NEG = -1e30
FORCE = 1e4

KV_LANES = N_KV * HEAD_DIM
ROW_LANES = 2 * KV_LANES
LANE = 128
ROW_TILE = 512
FF_CHUNK = 1024
SEL_TILE = 512
VMEM_LIMIT = 56 * 1024 * 1024


def _params(*sem):
    return pltpu.CompilerParams(dimension_semantics=sem, vmem_limit_bytes=VMEM_LIMIT)


def _dot(a, b):
    return jnp.dot(a, b, preferred_element_type=F32)


def _dot_nt(a, b):
    return lax.dot_general(a, b, (((1,), (1,)), ((), ())), preferred_element_type=F32)


def _rms(x, g):
    return x * lax.rsqrt(jnp.mean(x * x, axis=-1, keepdims=True) + EPS) * g


def _split_dot(x, m):
    hi = x.astype(BF16)
    lo = (x - hi.astype(F32)).astype(BF16)
    return _dot(hi, m) + _dot(lo, m)


def _head_rms(k, ones_bd, gain):
    ss = _split_dot(k * k, ones_bd)
    return k * lax.rsqrt(ss * (1.0 / HEAD_DIM) + EPS) * gain


def _softmax_rows(s, mask):
    s = jnp.where(mask, s, NEG)
    e = jnp.exp(s - jnp.max(s, axis=-1, keepdims=True))
    return jnp.where(mask, e, 0.0) / jnp.sum(e, axis=-1, keepdims=True)


def _conv_core(x, g, win_ref, cw, wout_ref, s1_fix, s2_fix):
    d = x.shape[1]
    h = _rms(x, g).astype(BF16)
    bg = _dot(h, win_ref[:, 0:d])
    cg = _dot(h, win_ref[:, d:2 * d])
    xin = _dot(h, win_ref[:, 2 * d:3 * d])
    u = cg * xin
    s1 = s1_fix(pltpu.roll(u, 1, 0))
    s2 = s2_fix(pltpu.roll(u, 2, 0))
    z = cw[0:1] * s2 + cw[1:2] * s1 + cw[2:3] * u
    y = _dot((bg * z).astype(BF16), wout_ref[...])
    return x + y, u


def _conv_long_kernel(x_ref, g_ref, win_ref, cw_ref, wout_ref, o_ref, tail_ref, carry_ref, *, tiles_per_seq):
    tm = x_ref.shape[0]

    @pl.when(pl.program_id(0) % tiles_per_seq == 0)
    def _():
        carry_ref[...] = jnp.zeros_like(carry_ref)

    prev = carry_ref[...]
    row = lax.broadcasted_iota(jnp.int32, (tm, 1), 0)
    s1_fix = lambda r: jnp.where(row == 0, prev[7:8], r)
    s2_fix = lambda r: jnp.where(row == 0, prev[6:7], jnp.where(row == 1, prev[7:8], r))
    out, u = _conv_core(x_ref[...], g_ref[...], win_ref, cw_ref[...], wout_ref, s1_fix, s2_fix)
    o_ref[...] = out
    carry_ref[...] = u[tm - 8:tm]
    tail_ref[0] = u[tm - 8:tm]


def _conv_short_kernel(x_ref, g_ref, win_ref, cw_ref, wout_ref, pa_ref, pb_ref, o_ref, u_ref, *, seg):
    tm = x_ref.shape[0]
    pos = lax.broadcasted_iota(jnp.int32, (tm, 1), 0) % seg
    s1_fix = lambda r: jnp.where(pos < 1, pa_ref[...], r)
    s2_fix = lambda r: jnp.where(pos < 2, pb_ref[...], r)
    out, u = _conv_core(x_ref[...], g_ref[...], win_ref, cw_ref[...], wout_ref, s1_fix, s2_fix)
    o_ref[...] = out
    u_ref[...] = u


def _conv_layer(x, g, w_in, cw, w_out, *, seq_len, state=None):
    n, d = x.shape
    tm = min(ROW_TILE, n)
    assert n % tm == 0
    n_seq = n // seq_len
    row_spec = pl.BlockSpec((tm, d), lambda i: (i, 0))
    w_specs = [pl.BlockSpec((1, d), lambda i: (0, 0)),
               pl.BlockSpec((d, 3 * d), lambda i: (0, 0)),
               pl.BlockSpec((CONV_W, d), lambda i: (0, 0)),
               pl.BlockSpec((d, d), lambda i: (0, 0))]
    if state is None:
        assert seq_len % tm == 0 and seq_len >= CONV_W - 1
        tiles_per_seq = seq_len // tm
        out, tail = pl.pallas_call(
            functools.partial(_conv_long_kernel, tiles_per_seq=tiles_per_seq),
            grid=(n // tm,),
            in_specs=[row_spec] + w_specs,
            out_specs=[row_spec, pl.BlockSpec((1, 8, d), lambda i: (i, 0, 0))],
            out_shape=[jax.ShapeDtypeStruct((n, d), F32), jax.ShapeDtypeStruct((n // tm, 8, d), F32)],
            scratch_shapes=[pltpu.VMEM((8, d), F32)],
            compiler_params=_params("arbitrary"),
            name="conv_long",
        )(x, g, w_in, cw, w_out)
        new_state = tail.reshape(n_seq, tiles_per_seq, 8, d)[:, -1, 8 - (CONV_W - 1):]
        return out, new_state
    assert tm % seq_len == 0 and seq_len >= CONV_W - 1
    zeros = jnp.zeros((n_seq, seq_len, d), F32)
    pa = zeros.at[:, 0].set(state[:, 1]).reshape(n, d)
    pb = zeros.at[:, 0].set(state[:, 0]).at[:, 1].set(state[:, 1]).reshape(n, d)
    out, u = pl.pallas_call(
        functools.partial(_conv_short_kernel, seg=seq_len),
        grid=(n // tm,),
        in_specs=[row_spec] + w_specs + [row_spec, row_spec],
        out_specs=[row_spec, row_spec],
        out_shape=[jax.ShapeDtypeStruct((n, d), F32), jax.ShapeDtypeStruct((n, d), F32)],
        compiler_params=_params("arbitrary"),
        name="conv_short",
    )(x, g, w_in, cw, w_out, pa, pb)
    new_state = u.reshape(n_seq, seq_len, d)[:, seq_len - (CONV_W - 1):]
    return out, new_state


def _mlp_body(x, g_ref, wup_ref, wdn_ref, o_ref):
    h = _rms(x, g_ref[...]).astype(BF16)
    acc = x
    for c in range(wup_ref.shape[1] // FF_CHUNK):
        a = _dot(h, wup_ref[:, c * FF_CHUNK:(c + 1) * FF_CHUNK])
        a = jnp.square(jnp.maximum(a, 0.0)).astype(BF16)
        acc = acc + _dot(a, wdn_ref[c * FF_CHUNK:(c + 1) * FF_CHUNK, :])
    o_ref[...] = acc


def _mlp_kernel(x_ref, g_ref, wup_ref, wdn_ref, o_ref):
    _mlp_body(x_ref[...], g_ref, wup_ref, wdn_ref, o_ref)


def _proj_mlp_kernel(x_ref, a_ref, wo_ref, g_ref, wup_ref, wdn_ref, o_ref):
    _mlp_body(x_ref[...] + _dot(a_ref[...], wo_ref[...]), g_ref, wup_ref, wdn_ref, o_ref)


def _mlp_layer(x, g, w_up, w_down, attn=None, w_o=None):
    n, d = x.shape
    ff = w_up.shape[1]
    assert ff % FF_CHUNK == 0
    tm = min(ROW_TILE, n)
    assert n % tm == 0
    row_spec = pl.BlockSpec((tm, d), lambda i: (i, 0))
    w_specs = [pl.BlockSpec((1, d), lambda i: (0, 0)),
               pl.BlockSpec((d, ff), lambda i: (0, 0)),
               pl.BlockSpec((ff, d), lambda i: (0, 0))]
    if attn is None:
        kern, in_specs, args = _mlp_kernel, [row_spec] + w_specs, (x, g, w_up, w_down)
    else:
        kern = _proj_mlp_kernel
        in_specs = [row_spec, row_spec, pl.BlockSpec((d, d), lambda i: (0, 0))] + w_specs
        args = (x, attn, w_o, g, w_up, w_down)
    return pl.pallas_call(
        kern, grid=(n // tm,), in_specs=in_specs, out_specs=row_spec,
        out_shape=jax.ShapeDtypeStruct((n, d), F32),
        compiler_params=_params("arbitrary"),
        name="mlp" if attn is None else "proj_mlp",
    )(*args)


def _kv_kernel(x_ref, g_ref, w_ref, ones_ref, ks_ref, kw_ref, cmp_ref, slc_ref, win_ref):
    h = _rms(x_ref[...], g_ref[...]).astype(BF16)
    kv = _dot(h, w_ref[...])
    ones_bd = ones_ref[...]
    cmp_ref[...] = kv[:, 0:ROW_LANES]
    for br, (gain_ref, f_ref) in enumerate(((ks_ref, slc_ref), (kw_ref, win_ref)), start=1):
        base = br * ROW_LANES
        f_ref[:, 0:KV_LANES] = _head_rms(kv[:, base:base + KV_LANES], ones_bd, gain_ref[...])
        f_ref[:, KV_LANES:ROW_LANES] = kv[:, base + KV_LANES:base + ROW_LANES]


def _kv_long_kernel(x_ref, g_ref, w_ref, ones_ref, ks_ref, kw_ref, pk_ref, pw_ref,
                    cmp_ref, slc_ref, win_ref, sk_ref, svt_ref, wk_ref, wvt_ref, *, tiles_per_seq):
    tm = x_ref.shape[0]
    pos0 = (pl.program_id(0) % tiles_per_seq) * tm
    h = _rms(x_ref[...], g_ref[...]).astype(BF16)
    kv = _dot(h, w_ref[...])
    ones_bd = ones_ref[...]
    cmp_ref[...] = kv[:, 0:ROW_LANES]
    blk = (pos0 + lax.broadcasted_iota(jnp.int32, (tm, 1), 0)) // L_SEL
    onehot = lax.broadcasted_iota(jnp.int32, (1, KV_LANES), 1) - HEAD_DIM == blk
    ones_rows = jnp.ones((HEAD_DIM, tm), F32)
    for br, (gain_ref, f_ref) in enumerate(((ks_ref, slc_ref), (kw_ref, win_ref)), start=1):
        base = br * ROW_LANES
        k = _head_rms(kv[:, base:base + KV_LANES], ones_bd, gain_ref[...])
        v = kv[:, base + KV_LANES:base + ROW_LANES]
        f_ref[:, 0:KV_LANES] = k
        f_ref[:, KV_LANES:ROW_LANES] = v
        kb = k.astype(BF16)
        vt = v.T
        for g in range(N_KV):
            vt_g = jnp.concatenate([vt[g * HEAD_DIM:(g + 1) * HEAD_DIM], ones_rows], axis=0).astype(BF16)
            if br == 1:
                sk_ref[0, g] = jnp.where(onehot, 1.0, _dot(kb, pk_ref[g])).astype(BF16)
                svt_ref[0, g, 0] = vt_g
            else:
                wk_ref[0, g] = _dot(kb, pw_ref[g]).astype(BF16)
                for j in range(tm // LANE):
                    wvt_ref[0, g, j] = vt_g[:, j * LANE:(j + 1) * LANE]


def _kv_rows_long(x, kv_norm, w_kv, ones_bd, ks_gain, kw_gain, pk, pw, *, seq_len):
    n, d = x.shape
    tm = SEL_TILE
    assert seq_len % tm == 0 and -(-seq_len // L_SEL) <= LANE
    n_seq, tps = n // seq_len, seq_len // tm
    row_spec = pl.BlockSpec((tm, d), lambda i: (i, 0))
    kv_spec = pl.BlockSpec((tm, ROW_LANES), lambda i: (i, 0))
    const = lambda shape: pl.BlockSpec(shape, lambda i: (0,) * len(shape))
    grp = lambda *tail: pl.BlockSpec((1, N_KV) + tail, lambda i: (i // tps, 0, i % tps) + (0,) * (len(tail) - 1))
    sds = jax.ShapeDtypeStruct
    return pl.pallas_call(
        functools.partial(_kv_long_kernel, tiles_per_seq=tps), grid=(n // tm,),
        in_specs=[row_spec, const((1, d)), const(w_kv.shape), const((KV_LANES, KV_LANES)),
                  const((1, KV_LANES)), const((1, KV_LANES)), const(pk.shape), const(pw.shape)],
        out_specs=[kv_spec] * 3 + [grp(tm, KV_LANES), grp(1, LANE, tm), grp(tm, LANE), grp(tm // LANE, LANE, LANE)],
        out_shape=[sds((n, ROW_LANES), F32)] * 3 + [
            sds((n_seq, N_KV, seq_len, KV_LANES), BF16), sds((n_seq, N_KV, tps, LANE, tm), BF16),
            sds((n_seq, N_KV, seq_len, LANE), BF16), sds((n_seq, N_KV, seq_len // LANE, LANE, LANE), BF16)],
        compiler_params=_params("arbitrary"),
        name="kv_rows_long",
    )(x, kv_norm, w_kv, ones_bd, ks_gain, kw_gain, pk, pw)


def _kv_rows(x, kv_norm, w_kv, ones_bd, ks_gain, kw_gain):
    n, d = x.shape
    tm = min(ROW_TILE, n)
    assert n % tm == 0
    row_spec = pl.BlockSpec((tm, d), lambda i: (i, 0))
    kv_spec = pl.BlockSpec((tm, ROW_LANES), lambda i: (i, 0))
    const = lambda shape: pl.BlockSpec(shape, lambda i: (0, 0))
    return pl.pallas_call(
        _kv_kernel, grid=(n // tm,),
        in_specs=[row_spec, const((1, d)), const(w_kv.shape), const((KV_LANES, KV_LANES)),
                  const((1, KV_LANES)), const((1, KV_LANES))],
        out_specs=[kv_spec] * 3,
        out_shape=[jax.ShapeDtypeStruct((n, ROW_LANES), F32)] * 3,
        compiler_params=_params("arbitrary"),
        name="kv_rows",
    )(x, kv_norm, w_kv, ones_bd, ks_gain, kw_gain)


def _compress_rows_kernel(x_ref, ones_ref, gain_ref, kc_ref, vc_ref):
    rows = x_ref.shape[0]
    blocks = jnp.mean(x_ref[...].reshape(rows // L_CMP, L_CMP, ROW_LANES), axis=1)
    kc_ref[...] = _head_rms(blocks[:, 0:KV_LANES], ones_ref[...], gain_ref[...])
    vc_ref[...] = blocks[:, KV_LANES:ROW_LANES]


def _compress_rows(cmp_rows, ones_bd, kc_gain):
    n = cmp_rows.shape[0]
    tm = min(ROW_TILE, n)
    assert n % tm == 0 and tm % (8 * L_CMP) == 0
    nb = tm // L_CMP
    const = lambda shape: pl.BlockSpec(shape, lambda i: (0, 0))
    out_spec = pl.BlockSpec((nb, KV_LANES), lambda i: (i, 0))
    return pl.pallas_call(
        _compress_rows_kernel, grid=(n // tm,),
        in_specs=[pl.BlockSpec((tm, ROW_LANES), lambda i: (i, 0)), const((KV_LANES, KV_LANES)), const((1, KV_LANES))],
        out_specs=[out_spec, out_spec],
        out_shape=[jax.ShapeDtypeStruct((n // L_CMP, KV_LANES), F32)] * 2,
        compiler_params=_params("arbitrary"),
        name="compress_rows",
    )(cmp_rows, ones_bd, kc_gain)


def _compress_pages_kernel(pt_ref, *refs, n_pages):
    page_refs = refs[:n_pages]
    ones_ref, gain_ref, kc_ref, vc_ref, blk_ref = refs[n_pages:]
    page = page_refs[0].shape[1]
    per = page // L_CMP
    for p in range(n_pages):
        blk_ref[p * per:(p + 1) * per, :] = jnp.mean(page_refs[p][0].reshape(per, L_CMP, ROW_LANES), axis=1)
    blocks = blk_ref[...]
    kc_ref[0] = _head_rms(blocks[:, 0:KV_LANES], ones_ref[...], gain_ref[...])
    vc_ref[0] = blocks[:, KV_LANES:ROW_LANES]


def _page_specs(n_pages, page):
    return [pl.BlockSpec((1, page, ROW_LANES), lambda b, pt, p=p: (pt[b, p], 0, 0)) for p in range(n_pages)]


def _compress_pages(pool, page_table, ones_bd, kc_gain):
    n_seq, n_pages = page_table.shape
    page = pool.shape[1]
    assert page % L_CMP == 0
    nbc = n_pages * page // L_CMP
    const = lambda shape: pl.BlockSpec(shape, lambda b, pt: (0, 0))
    out_spec = pl.BlockSpec((1, nbc, KV_LANES), lambda b, pt: (b, 0, 0))
    return pl.pallas_call(
        functools.partial(_compress_pages_kernel, n_pages=n_pages),
        grid_spec=pltpu.PrefetchScalarGridSpec(
            num_scalar_prefetch=1, grid=(n_seq,),
            in_specs=_page_specs(n_pages, page) + [const((KV_LANES, KV_LANES)), const((1, KV_LANES))],
            out_specs=[out_spec, out_spec],
            scratch_shapes=[pltpu.VMEM((nbc, ROW_LANES), F32)]),
        out_shape=[jax.ShapeDtypeStruct((n_seq, nbc, KV_LANES), F32)] * 2,
        compiler_params=_params("arbitrary"),
        name="compress_pages",
    )(page_table, *([pool] * n_pages), ones_bd, kc_gain)


def _even_odd(blocks, half):
    b, nbc, w = blocks.shape
    out = jnp.zeros((b, 2 * half, w), BF16)
    out = out.at[:, 0:(nbc + 1) // 2].set(blocks[:, 0::2].astype(BF16))
    return out.at[:, half:half + nbc // 2].set(blocks[:, 1::2].astype(BF16))


def _qg_kernel(x_ref, g_ref, wq_ref, wg_ref, ones_ref, qgain_ref, q_ref, gate_ref):
    h = _rms(x_ref[...], g_ref[...]).astype(BF16)
    q = _dot(h, wq_ref[...])
    ones_bd = ones_ref[...]
    gain = qgain_ref[...] * (HEAD_DIM ** -0.5)
    for s in range(q.shape[1] // KV_LANES):
        sl = slice(s * KV_LANES, (s + 1) * KV_LANES)
        q_ref[:, sl] = _head_rms(q[:, sl], ones_bd, gain).astype(BF16)
    gate_ref[...] = jax.nn.sigmoid(_dot(h, wg_ref[...]))


def _qg_proj(x, g, w_q, w_g, ones_bd, q_gain):
    n, d = x.shape
    tm = min(ROW_TILE, n)
    assert n % tm == 0
    nq = w_q.shape[1]
    const = lambda shape: pl.BlockSpec(shape, lambda i: (0, 0))
    return pl.pallas_call(
        _qg_kernel, grid=(n // tm,),
        in_specs=[pl.BlockSpec((tm, d), lambda i: (i, 0)), const((1, d)), const(w_q.shape), const(w_g.shape),
                  const((KV_LANES, KV_LANES)), const((1, KV_LANES))],
        out_specs=[pl.BlockSpec((tm, nq), lambda i: (i, 0)), pl.BlockSpec((tm, w_g.shape[1]), lambda i: (i, 0))],
        out_shape=[jax.ShapeDtypeStruct((n, nq), BF16), jax.ShapeDtypeStruct((n, w_g.shape[1]), F32)],
        compiler_params=_params("arbitrary"),
        name="qg_proj",
    )(x, g, w_q, w_g, ones_bd, q_gain)


def _group_queries(q_slabs, lane_group, g):
    return jnp.concatenate([jnp.where(lane_group == g, s, jnp.zeros_like(s)) for s in q_slabs], axis=0)


def _cmp_mask(t, half, nbc, n_cols):
    col = lax.broadcasted_iota(jnp.int32, (1, n_cols), 1)
    blk = 2 * (col % half) + col // half
    return ((blk + 1) * L_CMP - 1 <= t) & (blk < nbc)


def _select_blocks(imp, t, k_sel, axis=1):
    blk = lax.broadcasted_iota(jnp.int32, imp.shape, axis)
    cur = t // L_SEL
    forced = (blk == 0) | (blk == cur) | (blk == cur - 1)
    work = jnp.where(forced, FORCE, imp)
    work = jnp.where(blk > cur, NEG, work)
    sel = jnp.zeros(imp.shape, jnp.bool_)
    blk_f = blk.astype(F32)
    for _ in range(k_sel):
        m = jnp.max(work, axis=axis, keepdims=True)
        first = jnp.min(jnp.where(work == m, blk_f, float(imp.shape[axis])), axis=axis, keepdims=True)
        pick = blk_f == first
        sel = sel | (pick & (m > 0.5 * NEG))
        work = jnp.where(pick, -jnp.inf, work)
    return sel


def _merge_heads(parts, gate_slabs, lane_group, o_ref):
    tq = gate_slabs[0][0].shape[0]
    for r in range(GROUP):
        acc = None
        for br in range(N_BRANCH):
            o = jnp.zeros((tq, KV_LANES), F32)
            for g in range(N_KV):
                o = jnp.where(lane_group == g, parts[g][br][r * tq:(r + 1) * tq], o)
            term = gate_slabs[br][r] * o
            acc = term if acc is None else acc + term
        o_ref[0, :, r * KV_LANES:(r + 1) * KV_LANES] = acc.astype(o_ref.dtype)


def _expand_gates(gates, ge_ref):
    out = []
    for br in range(N_BRANCH):
        full = _split_dot(gates, ge_ref[br])
        out.append([full[:, r * KV_LANES:(r + 1) * KV_LANES] for r in range(GROUP)])
    return out


def _softmax_cols(s, mask):
    s = jnp.where(mask, s, NEG)
    e = jnp.exp(s - jnp.max(s, axis=0, keepdims=True))
    return jnp.where(mask, e, 0.0) / jnp.sum(e, axis=0, keepdims=True)


def _attn_seq_kernel(q_ref, gate_ref, kc_ref, vct_ref, sk_ref, svt_ref, wk_ref, wvt_ref, o_ref,
                     m_ref, acc_ref, sa_ref, sb_ref, *, nbc, half, k_sel):
    tq = q_ref.shape[1]
    c0 = pl.program_id(2) * tq
    cols = GROUP * tq
    t = c0 + lax.broadcasted_iota(jnp.int32, (1, cols), 1) % tq
    t1 = t[:, 0:tq]

    qt = q_ref[0].astype(F32).T
    q64 = jnp.concatenate([qt[r * HEAD_DIM:(r + 1) * HEAD_DIM] for r in range(GROUP)], axis=1)
    zeros64 = jnp.zeros((HEAD_DIM, cols), F32)
    q_pad = jnp.concatenate([q64, zeros64], axis=0).astype(BF16)

    row = lax.broadcasted_iota(jnp.int32, (2 * half, 1), 0)
    blk = 2 * (row % half) + row // half
    p = _softmax_cols(_dot(kc_ref[0, 0], q_pad), ((blk + 1) * L_CMP - 1 <= t) & (blk < nbc))
    o_cmp = _dot(vct_ref[0, 0], p.astype(BF16))[0:HEAD_DIM]
    ph = p[:, 0:tq]
    for r in range(1, GROUP):
        ph = ph + p[:, r * tq:(r + 1) * tq]
    sel = _select_blocks(ph[0:half] + ph[half:2 * half], t1, k_sel, axis=0)
    bias = jnp.where(sel, 0.0, NEG)
    q_sel = jnp.concatenate([q64, jnp.concatenate([bias] * GROUP, axis=1), zeros64], axis=0).astype(BF16)

    m_ref[...] = jnp.full(m_ref.shape, NEG, F32)
    acc_ref[...] = jnp.zeros(acc_ref.shape, F32)

    hk = SEL_TILE // 2

    def scores(kt, h):
        k0 = pl.multiple_of(kt * SEL_TILE + h * hk, hk)
        return _dot(sk_ref[0, 0, pl.ds(k0, hk), :], q_sel)

    def update(s, kt, h, causal):
        if causal:
            s = jnp.where(kt * SEL_TILE + h * hk + lax.broadcasted_iota(jnp.int32, (hk, 1), 0) <= t, s, NEG)
        m_old = m_ref[...]
        m_new = jnp.maximum(m_old, jnp.max(s, axis=0, keepdims=True))
        p = jnp.exp(s - m_new).astype(BF16)
        vt = svt_ref[0, 0, kt][:, h * hk:(h + 1) * hk]
        acc_ref[...] = jnp.exp(m_old - m_new) * acc_ref[...] + _dot(vt, p)
        m_ref[...] = m_new

    last = (c0 + tq - 1) // SEL_TILE
    sa_ref[...] = scores(0, 0)

    def body(kt, carry):
        sb_ref[...] = scores(kt, 1)
        update(sa_ref[...], kt, 0, False)
        sa_ref[...] = scores(kt + 1, 0)
        update(sb_ref[...], kt, 1, False)
        return carry

    lax.fori_loop(0, last, body, 0)
    sb_ref[...] = scores(last, 1)
    update(sa_ref[...], last, 0, True)
    update(sb_ref[...], last, 1, True)
    acc = acc_ref[...]
    o_sel = acc[0:HEAD_DIM] / acc[HEAD_DIM:HEAD_DIM + 1]

    band = tq + WINDOW
    wb = jnp.maximum(c0 - WINDOW, 0) // LANE
    w0 = pl.multiple_of(wb * LANE, LANE)
    s = _dot(wk_ref[0, 0, pl.ds(w0, band), :], q_pad)
    dist = t - (w0 + lax.broadcasted_iota(jnp.int32, (band, 1), 0))
    s = jnp.where((dist >= 0) & (dist <= WINDOW), s, NEG)
    p = jnp.exp(s - jnp.max(s, axis=0, keepdims=True)).astype(BF16)
    wvt = jnp.concatenate([wvt_ref[0, 0, wb + j] for j in range(band // LANE)], axis=1)
    o_win = _dot(wvt, p)
    o_win = o_win[0:HEAD_DIM] / o_win[HEAD_DIM:HEAD_DIM + 1]

    gt = gate_ref[0].T
    gate = lambda br: jnp.concatenate([gt[r * N_BRANCH + br:r * N_BRANCH + br + 1] for r in range(GROUP)], axis=1)
    o = gate(0) * o_cmp + gate(1) * o_sel + gate(2) * o_win
    o = jnp.concatenate([o[:, r * tq:(r + 1) * tq] for r in range(GROUP)], axis=0)
    o_ref[0] = o.T.astype(o_ref.dtype)


def _attn_seq(q, gates, kc, vct, sk, svt, wk, wvt, *, nbc, half, k_sel):
    b, t_len, nq = q.shape
    tq = Q_BLOCK
    assert tq == LANE and t_len % SEL_TILE == 0 and t_len >= tq + WINDOW and WINDOW % LANE == 0
    cols = GROUP * tq
    grp = lambda *shape: pl.BlockSpec((1, 1) + shape, lambda i, g, c: (i, g) + (0,) * len(shape))
    chunk = lambda w: pl.BlockSpec((1, tq, w), lambda i, g, c: (i, c, g))
    return pl.pallas_call(
        functools.partial(_attn_seq_kernel, nbc=nbc, half=half, k_sel=k_sel),
        grid=(b, N_KV, t_len // tq),
        in_specs=[chunk(KV_LANES), chunk(LANE), grp(2 * half, LANE), grp(LANE, 2 * half),
                  grp(t_len, KV_LANES), grp(t_len // SEL_TILE, LANE, SEL_TILE),
                  grp(t_len, LANE), grp(t_len // LANE, LANE, LANE)],
        out_specs=chunk(KV_LANES),
        out_shape=jax.ShapeDtypeStruct((b, t_len, nq), BF16),
        scratch_shapes=[pltpu.VMEM((1, cols), F32), pltpu.VMEM((LANE, cols), F32),
                        pltpu.VMEM((SEL_TILE // 2, cols), F32), pltpu.VMEM((SEL_TILE // 2, cols), F32)],
        compiler_params=_params("arbitrary", "arbitrary", "arbitrary"),
        name="attn_seq",
    )(q, gates, kc, vct, sk, svt, wk, wvt)


def _attn_paged_kernel(pt_ref, *refs, n_pages, past, w_buf, nbc, half, k_sel):
    page_refs = refs[:n_pages]
    (q_ref, gate_ref, kc_ref, vc_ref, snew_ref, wold_ref, wnew_ref, e_ref, ge_ref, o_ref) = refs[n_pages:]
    tq = q_ref.shape[1]
    page = page_refs[0].shape[1]
    tok = lax.broadcasted_iota(jnp.int32, (tq, 1), 0)
    t1 = past + tok
    t = jnp.concatenate([t1] * (N_KV * GROUP), axis=0)
    lane_group = lax.broadcasted_iota(jnp.int32, (1, KV_LANES), 1) // HEAD_DIM
    q_slabs = [q_ref[0, :, r * KV_LANES:(r + 1) * KV_LANES].astype(F32) for r in range(GROUP)]
    qe = jnp.concatenate([_group_queries(q_slabs, lane_group, g) for g in range(N_KV)], axis=0).astype(BF16)
    rows_g = GROUP * tq

    def pad_rows(x):
        return jnp.concatenate([x, jnp.zeros((LANE - x.shape[0], x.shape[1]), x.dtype)], axis=0)

    p = _softmax_rows(_dot_nt(qe, kc_ref[0]), _cmp_mask(t, half, nbc, 2 * half))
    o_cmp = _dot(p.astype(BF16), vc_ref[0])
    imps = []
    for g in range(N_KV):
        ph = p[g * rows_g:g * rows_g + tq]
        for r in range(1, GROUP):
            ph = ph + p[g * rows_g + r * tq:g * rows_g + (r + 1) * tq]
        imps.append(ph[:, 0:half] + ph[:, half:2 * half])
    sel = _select_blocks(jnp.concatenate(imps, axis=0), jnp.concatenate([t1] * N_KV, axis=0), k_sel)
    sel_bias = jnp.where(sel, 0.0, NEG)
    sel_bias = jnp.concatenate([sel_bias[g * tq:(g + 1) * tq] for g in range(N_KV) for _ in range(GROUP)], axis=0)

    snew = pad_rows(snew_ref[0])
    keys = [page_refs[i][0, :, 0:KV_LANES].astype(BF16) for i in range(n_pages)] + [snew[:, 0:KV_LANES].astype(BF16)]
    s = jnp.concatenate([_dot_nt(qe, k) for k in keys], axis=1) + _dot(sel_bias.astype(BF16), e_ref[...])
    kpos = lax.broadcasted_iota(jnp.int32, (1, s.shape[1]), 1)
    p = _softmax_rows(s, kpos <= t).astype(BF16)
    o_sel = _dot(p[:, n_pages * page:], snew[:, KV_LANES:ROW_LANES].astype(BF16))
    for i in range(n_pages):
        o_sel = o_sel + _dot(p[:, i * page:(i + 1) * page], page_refs[i][0, :, KV_LANES:ROW_LANES].astype(BF16))

    wold = wold_ref[0]
    wnew = pad_rows(wnew_ref[0])
    s = jnp.concatenate([_dot_nt(qe, wold[:, 0:KV_LANES].astype(BF16)),
                         _dot_nt(qe, wnew[:, 0:KV_LANES].astype(BF16))], axis=1)
    j = lax.broadcasted_iota(jnp.int32, (1, s.shape[1]), 1)
    dist = t - (past - w_buf + j)
    p = _softmax_rows(s, (dist >= 0) & (dist <= WINDOW) & (j < w_buf + tq)).astype(BF16)
    o_win = (_dot(p[:, 0:w_buf], wold[:, KV_LANES:ROW_LANES].astype(BF16))
             + _dot(p[:, w_buf:], wnew[:, KV_LANES:ROW_LANES].astype(BF16)))

    parts = [[o[g * rows_g:(g + 1) * rows_g] for o in (o_cmp, o_sel, o_win)] for g in range(N_KV)]
    _merge_heads(parts, _expand_gates(gate_ref[0], ge_ref), lane_group, o_ref)


def _attn_paged(q, gates, kc, vc, pool, page_table, slc_new, win_old, win_new, e_sel, ge, *, nbc, half, k_sel):
    n_seq, tq, nq = q.shape
    n_pages = page_table.shape[1]
    page = pool.shape[1]
    past = n_pages * page
    w_buf = win_old.shape[1]
    assert page == LANE and w_buf % LANE == 0 and tq % 8 == 0 and tq <= LANE
    per_seq = lambda shape: pl.BlockSpec((1,) + shape, lambda b, pt: (b, 0, 0))
    const = lambda shape: pl.BlockSpec(shape, lambda b, pt: (0,) * len(shape))
    return pl.pallas_call(
        functools.partial(_attn_paged_kernel, n_pages=n_pages, past=past, w_buf=w_buf, nbc=nbc, half=half, k_sel=k_sel),
        grid_spec=pltpu.PrefetchScalarGridSpec(
            num_scalar_prefetch=1, grid=(n_seq,),
            in_specs=_page_specs(n_pages, page) + [
                per_seq((tq, nq)), per_seq((tq, LANE)), per_seq((2 * half, KV_LANES)), per_seq((2 * half, KV_LANES)),
                per_seq((tq, ROW_LANES)), per_seq((w_buf, ROW_LANES)), per_seq((tq, ROW_LANES)),
                const(e_sel.shape), const(ge.shape)],
            out_specs=per_seq((tq, nq))),
        out_shape=jax.ShapeDtypeStruct((n_seq, tq, nq), BF16),
        compiler_params=_params("arbitrary"),
        name="attn_paged",
    )(page_table, *([pool] * n_pages), q, gates, kc, vc, slc_new, win_old, win_new, e_sel, ge)


def _round_up(x, m):
    return -(-x // m) * m


def _block_onehot(n_rows, n_keys, t_valid):
    key = jnp.arange(n_keys)[None, :]
    return ((key // L_SEL == jnp.arange(n_rows)[:, None]) & (key < t_valid)).astype(BF16)


def kernel(x_prompt, x_sample, cache_cmp_kv, cache_slc_kv, cache_win_kv, state_conv, page_table, norm1, a_w_in, a_conv_w, a_w_out, kv_norm, w_kv, kc_norm, ks_norm, kw_norm, b_w_qg, b_q_norm, b_w_o, norm2, w_up, w_down):
    bp, tp, d = x_prompt.shape
    bs, ts, _ = x_sample.shape
    depth = norm1.shape[0]
    n_a = a_w_in.shape[0]
    n_pool, page = cache_cmp_kv.shape[:2]
    n_pages = page_table.shape[1]
    past = n_pages * page
    w_buf = cache_win_kv.shape[1]
    row = cache_cmp_kv.shape[2:]
    nq = N_HEADS * HEAD_DIM
    assert d == nq and past % L_CMP == 0 and ts < L_CMP and tp % L_CMP == 0

    hperm = jnp.array([(g * GROUP + r) * HEAD_DIM + dd for r in range(GROUP) for g in range(N_KV) for dd in range(HEAD_DIM)])
    lane = jnp.arange(KV_LANES)
    ones_bd = (lane[:, None] // HEAD_DIM == lane[None, :] // HEAD_DIM).astype(BF16)
    tile4 = lambda v: jnp.tile(v.astype(F32), N_KV)[None, :]
    head_of_lane = (jnp.arange(nq) // HEAD_DIM % N_KV) * GROUP + jnp.arange(nq) // KV_LANES
    gcol = jnp.arange(LANE)
    ge = jnp.stack([(gcol[:, None] == head_of_lane[None, :] * N_BRANCH + br) for br in range(N_BRANCH)]).astype(BF16)
    w_in_b = a_w_in.astype(BF16)
    w_out_b = a_w_out.astype(BF16)
    w_up_b = w_up.astype(BF16)
    w_down_b = w_down.astype(BF16)
    w_kv_b = w_kv.astype(BF16)
    w_q_nat = b_w_qg[:, :, :nq].astype(BF16)
    w_q_b = w_q_nat[:, :, hperm]
    w_g_b = jnp.pad(b_w_qg[:, :, nq:], ((0, 0), (0, 0), (0, LANE - N_HEADS * N_BRANCH))).astype(BF16)
    w_g_grp = b_w_qg[:, :, nq:].reshape(b_w_qg.shape[0], d, N_KV, GROUP * N_BRANCH)
    w_g_grp = jnp.pad(w_g_grp, ((0, 0), (0, 0), (0, 0), (0, LANE - GROUP * N_BRANCH))).reshape(-1, d, N_KV * LANE).astype(BF16)
    w_o_nat = b_w_o.astype(BF16)
    w_o_b = w_o_nat[:, hperm, :]
    src = jnp.arange(KV_LANES)[None, :, None]
    grp_id = jnp.arange(N_KV)[:, None, None]
    pk = (src == grp_id * HEAD_DIM + jnp.arange(KV_LANES)[None, None, :]) & (jnp.arange(KV_LANES)[None, None, :] < HEAD_DIM)
    pk = pk.astype(BF16)
    pw = pk[:, :, 0:LANE]
    g1 = norm1.astype(F32)[:, None, :]
    g2 = norm2.astype(F32)[:, None, :]
    kvn = kv_norm.astype(F32)[None, :]

    xp = x_prompt.reshape(bp * tp, d)
    xs = x_sample.reshape(bs * ts, d)
    p_conv, s_conv = [], []
    for l in range(n_a):
        xp, st = _conv_layer(xp, g1[l], w_in_b[l], a_conv_w[l], w_out_b[l], seq_len=tp)
        p_conv.append(st)
        xs, st = _conv_layer(xs, g1[l], w_in_b[l], a_conv_w[l], w_out_b[l], seq_len=ts, state=state_conv[l])
        s_conv.append(st)
        xp = _mlp_layer(xp, g2[l], w_up_b[l], w_down_b[l])
        xs = _mlp_layer(xs, g2[l], w_up_b[l], w_down_b[l])

    ks_g, kw_g, kc_g = tile4(ks_norm), tile4(kw_norm), tile4(kc_norm)
    p_cmp, p_slc, p_win, p_sk, p_svt, p_wk, p_wvt = _kv_rows_long(xp, kvn, w_kv_b, ones_bd, ks_g, kw_g, pk, pw, seq_len=tp)
    s_cmp, s_slc, s_win = _kv_rows(xs, kvn, w_kv_b, ones_bd, ks_g, kw_g)

    nbc_p = tp // L_CMP
    nbs_p = -(-tp // L_SEL)
    half_p = _round_up(nbs_p, LANE)
    assert half_p == LANE
    kc_p, vc_p = _compress_rows(p_cmp, ones_bd, kc_g)
    per_group = lambda a: _even_odd(a.reshape(bp, nbc_p, KV_LANES), half_p).reshape(bp, 2 * half_p, N_KV, HEAD_DIM)
    kc_p = jnp.pad(per_group(kc_p).transpose(0, 2, 1, 3), ((0, 0), (0, 0), (0, 0), (0, LANE - HEAD_DIM)))
    vct_p = jnp.pad(per_group(vc_p).transpose(0, 2, 3, 1), ((0, 0), (0, 0), (0, LANE - HEAD_DIM), (0, 0)))

    t_all = past + ts
    nbc_s = t_all // L_CMP
    assert nbc_s * L_CMP == past
    nbs_s = -(-t_all // L_SEL)
    half_s = _round_up(nbs_s, LANE)
    pool_cmp = cache_cmp_kv.reshape(n_pool, page, ROW_LANES)
    pool_slc = cache_slc_kv.reshape(n_pool, page, ROW_LANES)
    kc_s, vc_s = _compress_pages(pool_cmp, page_table, ones_bd, kc_g)
    kc_s = _even_odd(kc_s, half_s)
    vc_s = _even_odd(vc_s, half_s)
    e_s = _block_onehot(half_s, past + LANE, t_all)
    win_old = cache_win_kv.reshape(bs, w_buf, ROW_LANES)

    for j in range(depth - n_a):
        l = n_a + j
        qg = tile4(b_q_norm[j])
        q, gates = _qg_proj(xp, g1[l], w_q_nat[j], w_g_grp[j], ones_bd, qg)
        o = _attn_seq(q.reshape(bp, tp, nq), gates.reshape(bp, tp, N_KV * LANE), kc_p, vct_p, p_sk, p_svt, p_wk, p_wvt,
                      nbc=nbc_p, half=half_p, k_sel=min(N_SEL, nbs_p))
        xp = _mlp_layer(xp, g2[l], w_up_b[l], w_down_b[l], attn=o.reshape(bp * tp, nq), w_o=w_o_nat[j])

        q, gates = _qg_proj(xs, g1[l], w_q_b[j], w_g_b[j], ones_bd, qg)
        o = _attn_paged(q.reshape(bs, ts, nq), gates.reshape(bs, ts, LANE), kc_s, vc_s, pool_slc, page_table,
                        s_slc.reshape(bs, ts, ROW_LANES), win_old, s_win.reshape(bs, ts, ROW_LANES), e_s, ge,
                        nbc=nbc_s, half=half_s, k_sel=min(N_SEL, nbs_s))
        xs = _mlp_layer(xs, g2[l], w_up_b[l], w_down_b[l], attn=o.reshape(bs * ts, nq), w_o=w_o_b[j])

    rows5 = lambda a, b_, t_: a.reshape((b_, t_) + row)
    p_win_rows = rows5(p_win, bp, tp)
    s_win_all = jnp.concatenate([cache_win_kv, rows5(s_win, bs, ts)], axis=1)
    return (xp.reshape(bp, tp, d), xs.reshape(bs, ts, d),
            rows5(p_cmp, bp, tp), rows5(p_slc, bp, tp), p_win_rows[:, -min(WINDOW, tp):], jnp.stack(p_conv),
            rows5(s_cmp, bs, ts), rows5(s_slc, bs, ts), s_win_all[:, -w_buf:], jnp.stack(s_conv))
```

```python
import functools
import math

import jax
import jax.numpy as jnp
from jax import lax
from jax.experimental import pallas as pl
from jax.experimental.pallas import tpu as pltpu

F32 = jnp.float32
BF16 = jnp.bfloat16

N_HEADS = 16
N_KV = 4
HEAD_DIM = 64
GROUP = N_HEADS // N_KV
N_BRANCH = 3
L_CMP = 32
L_SEL = 64
N_SEL = 8
WINDOW = 512
Q_BLOCK = 128
CONV_W = 3
EPS = 1e-6
NEG = -1e30
FORCE = 1e4
LOG2E = math.log2(math.e)

KV_LANES = N_KV * HEAD_DIM
ROW_LANES = 2 * KV_LANES
LANE = 128
ROW_TILE = 512
FF_CHUNK = 1024
SEL_TILE = 512
VMEM_LIMIT = 56 * 1024 * 1024


def _params(*sem):
    return pltpu.CompilerParams(dimension_semantics=sem, vmem_limit_bytes=VMEM_LIMIT)


def _dot(a, b):
    return jnp.dot(a, b, preferred_element_type=F32)


def _dot_nt(a, b):
    return lax.dot_general(a, b, (((1,), (1,)), ((), ())), preferred_element_type=F32)


def _rms(x, g):
    return x * lax.rsqrt(jnp.mean(x * x, axis=-1, keepdims=True) + EPS) * g


def _split_dot(x, m):
    hi = x.astype(BF16)
    lo = (x - hi.astype(F32)).astype(BF16)
    return _dot(hi, m) + _dot(lo, m)


def _head_rms(k, ones_bd, gain):
    ss = _split_dot(k * k, ones_bd)
    return k * lax.rsqrt(ss * (1.0 / HEAD_DIM) + EPS) * gain


def _softmax2(s, mask, axis):
    s = jnp.where(mask, s, NEG)
    e = jnp.exp2(s - jnp.max(s, axis=axis, keepdims=True))
    return jnp.where(mask, e, 0.0) * (1.0 / jnp.sum(e, axis=axis, keepdims=True))


def _conv_core(x, g, win_ref, cw, wout_ref, s1_fix, s2_fix):
    d = x.shape[1]
    h = _rms(x, g).astype(BF16)
    bg = _dot(h, win_ref[:, 0:d])
    cg = _dot(h, win_ref[:, d:2 * d])
    xin = _dot(h, win_ref[:, 2 * d:3 * d])
    u = cg * xin
    s1 = s1_fix(pltpu.roll(u, 1, 0))
    s2 = s2_fix(pltpu.roll(u, 2, 0))
    z = cw[0:1] * s2 + cw[1:2] * s1 + cw[2:3] * u
    y = _dot((bg * z).astype(BF16), wout_ref[...])
    return x + y, u


def _conv_long_kernel(x_ref, g_ref, win_ref, cw_ref, wout_ref, o_ref, tail_ref, carry_ref, *, tiles_per_seq):
    tm = x_ref.shape[0]

    @pl.when(pl.program_id(0) % tiles_per_seq == 0)
    def _():
        carry_ref[...] = jnp.zeros_like(carry_ref)

    prev = carry_ref[...]
    row = lax.broadcasted_iota(jnp.int32, (tm, 1), 0)
    s1_fix = lambda r: jnp.where(row == 0, prev[7:8], r)
    s2_fix = lambda r: jnp.where(row == 0, prev[6:7], jnp.where(row == 1, prev[7:8], r))
    out, u = _conv_core(x_ref[...], g_ref[...], win_ref, cw_ref[...], wout_ref, s1_fix, s2_fix)
    o_ref[...] = out
    carry_ref[...] = u[tm - 8:tm]
    tail_ref[0] = u[tm - 8:tm]


def _conv_short_kernel(x_ref, g_ref, win_ref, cw_ref, wout_ref, pa_ref, pb_ref, o_ref, u_ref, *, seg):
    tm = x_ref.shape[0]
    pos = lax.broadcasted_iota(jnp.int32, (tm, 1), 0) % seg
    s1_fix = lambda r: jnp.where(pos < 1, pa_ref[...], r)
    s2_fix = lambda r: jnp.where(pos < 2, pb_ref[...], r)
    out, u = _conv_core(x_ref[...], g_ref[...], win_ref, cw_ref[...], wout_ref, s1_fix, s2_fix)
    o_ref[...] = out
    u_ref[...] = u


def _conv_layer(x, g, w_in, cw, w_out, *, seq_len, state=None):
    n, d = x.shape
    tm = min(ROW_TILE, n)
    assert n % tm == 0
    n_seq = n // seq_len
    row_spec = pl.BlockSpec((tm, d), lambda i: (i, 0))
    w_specs = [pl.BlockSpec((1, d), lambda i: (0, 0)),
               pl.BlockSpec((d, 3 * d), lambda i: (0, 0)),
               pl.BlockSpec((CONV_W, d), lambda i: (0, 0)),
               pl.BlockSpec((d, d), lambda i: (0, 0))]
    if state is None:
        assert seq_len % tm == 0 and seq_len >= CONV_W - 1
        tiles_per_seq = seq_len // tm
        out, tail = pl.pallas_call(
            functools.partial(_conv_long_kernel, tiles_per_seq=tiles_per_seq),
            grid=(n // tm,),
            in_specs=[row_spec] + w_specs,
            out_specs=[row_spec, pl.BlockSpec((1, 8, d), lambda i: (i, 0, 0))],
            out_shape=[jax.ShapeDtypeStruct((n, d), F32), jax.ShapeDtypeStruct((n // tm, 8, d), F32)],
            scratch_shapes=[pltpu.VMEM((8, d), F32)],
            compiler_params=_params("arbitrary"),
            name="conv_long",
        )(x, g, w_in, cw, w_out)
        new_state = tail.reshape(n_seq, tiles_per_seq, 8, d)[:, -1, 8 - (CONV_W - 1):]
        return out, new_state
    assert tm % seq_len == 0 and seq_len >= CONV_W - 1
    zeros = jnp.zeros((n_seq, seq_len, d), F32)
    pa = zeros.at[:, 0].set(state[:, 1]).reshape(n, d)
    pb = zeros.at[:, 0].set(state[:, 0]).at[:, 1].set(state[:, 1]).reshape(n, d)
    out, u = pl.pallas_call(
        functools.partial(_conv_short_kernel, seg=seq_len),
        grid=(n // tm,),
        in_specs=[row_spec] + w_specs + [row_spec, row_spec],
        out_specs=[row_spec, row_spec],
        out_shape=[jax.ShapeDtypeStruct((n, d), F32), jax.ShapeDtypeStruct((n, d), F32)],
        compiler_params=_params("arbitrary"),
        name="conv_short",
    )(x, g, w_in, cw, w_out, pa, pb)
    new_state = u.reshape(n_seq, seq_len, d)[:, seq_len - (CONV_W - 1):]
    return out, new_state


def _mlp_body(x, g_ref, wup_ref, wdn_ref, o_ref):
    h = _rms(x, g_ref[...]).astype(BF16)
    acc = x
    for c in range(wup_ref.shape[1] // FF_CHUNK):
        a = _dot(h, wup_ref[:, c * FF_CHUNK:(c + 1) * FF_CHUNK])
        a = jnp.square(jnp.maximum(a, 0.0)).astype(BF16)
        acc = acc + _dot(a, wdn_ref[c * FF_CHUNK:(c + 1) * FF_CHUNK, :])
    o_ref[...] = acc


def _mlp_kernel(x_ref, g_ref, wup_ref, wdn_ref, o_ref):
    _mlp_body(x_ref[...], g_ref, wup_ref, wdn_ref, o_ref)


def _proj_mlp_kernel(x_ref, a_ref, wo_ref, g_ref, wup_ref, wdn_ref, o_ref):
    _mlp_body(x_ref[...] + _dot(a_ref[...], wo_ref[...]), g_ref, wup_ref, wdn_ref, o_ref)


def _mlp_layer(x, g, w_up, w_down, attn=None, w_o=None):
    n, d = x.shape
    ff = w_up.shape[1]
    assert ff % FF_CHUNK == 0
    tm = min(ROW_TILE, n)
    assert n % tm == 0
    row_spec = pl.BlockSpec((tm, d), lambda i: (i, 0))
    w_specs = [pl.BlockSpec((1, d), lambda i: (0, 0)),
               pl.BlockSpec((d, ff), lambda i: (0, 0)),
               pl.BlockSpec((ff, d), lambda i: (0, 0))]
    if attn is None:
        kern, in_specs, args = _mlp_kernel, [row_spec] + w_specs, (x, g, w_up, w_down)
    else:
        kern = _proj_mlp_kernel
        in_specs = [row_spec, row_spec, pl.BlockSpec((d, d), lambda i: (0, 0))] + w_specs
        args = (x, attn, w_o, g, w_up, w_down)
    return pl.pallas_call(
        kern, grid=(n // tm,), in_specs=in_specs, out_specs=row_spec,
        out_shape=jax.ShapeDtypeStruct((n, d), F32),
        compiler_params=_params("arbitrary"),
        name="mlp" if attn is None else "proj_mlp",
    )(*args)


def _kv_kernel(x_ref, g_ref, w_ref, ones_ref, ks_ref, kw_ref, cmp_ref, slc_ref, win_ref):
    h = _rms(x_ref[...], g_ref[...]).astype(BF16)
    kv = _dot(h, w_ref[...])
    ones_bd = ones_ref[...]
    cmp_ref[...] = kv[:, 0:ROW_LANES]
    for br, (gain_ref, f_ref) in enumerate(((ks_ref, slc_ref), (kw_ref, win_ref)), start=1):
        base = br * ROW_LANES
        f_ref[:, 0:KV_LANES] = _head_rms(kv[:, base:base + KV_LANES], ones_bd, gain_ref[...])
        f_ref[:, KV_LANES:ROW_LANES] = kv[:, base + KV_LANES:base + ROW_LANES]


def _kv_rows(x, kv_norm, w_kv, ones_bd, ks_gain, kw_gain):
    n, d = x.shape
    tm = min(ROW_TILE, n)
    assert n % tm == 0
    row_spec = pl.BlockSpec((tm, d), lambda i: (i, 0))
    kv_spec = pl.BlockSpec((tm, ROW_LANES), lambda i: (i, 0))
    const = lambda shape: pl.BlockSpec(shape, lambda i: (0, 0))
    return pl.pallas_call(
        _kv_kernel, grid=(n // tm,),
        in_specs=[row_spec, const((1, d)), const(w_kv.shape), const((KV_LANES, KV_LANES)),
                  const((1, KV_LANES)), const((1, KV_LANES))],
        out_specs=[kv_spec] * 3,
        out_shape=[jax.ShapeDtypeStruct((n, ROW_LANES), F32)] * 3,
        compiler_params=_params("arbitrary"),
        name="kv_rows",
    )(x, kv_norm, w_kv, ones_bd, ks_gain, kw_gain)


def _kv_long_kernel(x_ref, g_ref, w_ref, ones_ref, ks_ref, kw_ref, kc_gain_ref, pk_ref, pw_ref,
                    cmp_ref, slc_ref, win_ref, kc_ref, vc_ref, sk_ref, svt_ref, wk_ref, wvt_ref, *, tiles_per_seq):
    tm = x_ref.shape[0]
    pos0 = (pl.program_id(0) % tiles_per_seq) * tm
    h = _rms(x_ref[...], g_ref[...]).astype(BF16)
    kv = _dot(h, w_ref[...])
    ones_bd = ones_ref[...]

    cmp_k, cmp_v = kv[:, 0:KV_LANES], kv[:, KV_LANES:ROW_LANES]
    cmp_ref[0, 0] = cmp_k.T
    cmp_ref[0, 1] = cmp_v.T
    kc_ref[...] = _head_rms(jnp.mean(cmp_k.reshape(tm // L_CMP, L_CMP, KV_LANES), axis=1), ones_bd, kc_gain_ref[...])
    vc_ref[...] = jnp.mean(cmp_v.reshape(tm // L_CMP, L_CMP, KV_LANES), axis=1)

    blk = (pos0 + lax.broadcasted_iota(jnp.int32, (tm, 1), 0)) // L_SEL
    onehot = lax.broadcasted_iota(jnp.int32, (1, KV_LANES), 1) - HEAD_DIM == blk
    ones_rows = jnp.ones((HEAD_DIM, tm), F32)
    for br, (gain_ref, f_ref) in enumerate(((ks_ref, slc_ref), (kw_ref, win_ref)), start=1):
        base = br * ROW_LANES
        k = _head_rms(kv[:, base:base + KV_LANES], ones_bd, gain_ref[...])
        vt = kv[:, base + KV_LANES:base + ROW_LANES].T
        f_ref[0, 0] = k.T
        f_ref[0, 1] = vt
        kb = k.astype(BF16)
        for g in range(N_KV):
            vt_g = jnp.concatenate([vt[g * HEAD_DIM:(g + 1) * HEAD_DIM], ones_rows], axis=0).astype(BF16)
            if br == 1:
                sk_ref[0, g] = jnp.where(onehot, 1.0, _dot(kb, pk_ref[g])).astype(BF16)
                svt_ref[0, g, 0] = vt_g
            else:
                wk_ref[0, g] = _dot(kb, pw_ref[g]).astype(BF16)
                for j in range(tm // LANE):
                    wvt_ref[0, g, j] = vt_g[:, j * LANE:(j + 1) * LANE]


def _kv_rows_long(x, kv_norm, w_kv, ones_bd, ks_gain, kw_gain, kc_gain, pk, pw, *, seq_len):
    n, d = x.shape
    tm = SEL_TILE
    assert seq_len % tm == 0 and -(-seq_len // L_SEL) <= LANE and tm % (8 * L_CMP) == 0
    n_seq, tps = n // seq_len, seq_len // tm
    nb = tm // L_CMP
    row_spec = pl.BlockSpec((tm, d), lambda i: (i, 0))
    t_spec = pl.BlockSpec((1, 2, KV_LANES, tm), lambda i: (i // tps, 0, 0, i % tps))
    blk_spec = pl.BlockSpec((nb, KV_LANES), lambda i: (i, 0))
    const = lambda shape: pl.BlockSpec(shape, lambda i: (0,) * len(shape))
    grp = lambda *tail: pl.BlockSpec((1, N_KV) + tail, lambda i: (i // tps, 0, i % tps) + (0,) * (len(tail) - 1))
    sds = jax.ShapeDtypeStruct
    return pl.pallas_call(
        functools.partial(_kv_long_kernel, tiles_per_seq=tps), grid=(n // tm,),
        in_specs=[row_spec, const((1, d)), const(w_kv.shape), const((KV_LANES, KV_LANES)),
                  const((1, KV_LANES)), const((1, KV_LANES)), const((1, KV_LANES)), const(pk.shape), const(pw.shape)],
        out_specs=[t_spec] * 3 + [blk_spec] * 2 + [
            grp(tm, KV_LANES), grp(1, LANE, tm), grp(tm, LANE), grp(tm // LANE, LANE, LANE)],
        out_shape=[sds((n_seq, 2, KV_LANES, seq_len), F32)] * 3 + [sds((n // L_CMP, KV_LANES), F32)] * 2 + [
            sds((n_seq, N_KV, seq_len, KV_LANES), BF16), sds((n_seq, N_KV, tps, LANE, tm), BF16),
            sds((n_seq, N_KV, seq_len, LANE), BF16), sds((n_seq, N_KV, seq_len // LANE, LANE, LANE), BF16)],
        compiler_params=_params("arbitrary"),
        name="kv_rows_long",
    )(x, kv_norm, w_kv, ones_bd, ks_gain, kw_gain, kc_gain, pk, pw)


def _page_specs(n_pages, page):
    return [pl.BlockSpec((1, 2, KV_LANES, page), lambda b, pt, p=p: (pt[b, p], 0, 0, 0)) for p in range(n_pages)]


def _compress_pages_kernel(pt_ref, *refs, n_pages):
    page_refs = refs[:n_pages]
    pool_ref, gain_ref, kc_ref, vc_ref = refs[n_pages:]
    page = page_refs[0].shape[3]
    acc = None
    for p in range(n_pages):
        part = _dot(page_refs[p][0].reshape(ROW_LANES, page).astype(BF16), pool_ref[p])
        acc = part if acc is None else acc + part
    n_col = acc.shape[1]
    k = acc[0:KV_LANES].reshape(N_KV, HEAD_DIM, n_col)
    inv = lax.rsqrt(jnp.mean(k * k, axis=1, keepdims=True) + EPS)
    kc_ref[0] = ((k * inv).reshape(KV_LANES, n_col) * gain_ref[...]).astype(kc_ref.dtype)
    vc_ref[0] = acc[KV_LANES:ROW_LANES].astype(vc_ref.dtype)


def _compress_pages(pool, page_table, pool_mat, kc_gain_col):
    n_seq, n_pages = page_table.shape
    page = pool.shape[3]
    n_col = pool_mat.shape[2]
    const = lambda shape: pl.BlockSpec(shape, lambda b, pt: (0,) * len(shape))
    out_spec = pl.BlockSpec((1, KV_LANES, n_col), lambda b, pt: (b, 0, 0))
    return pl.pallas_call(
        functools.partial(_compress_pages_kernel, n_pages=n_pages),
        grid_spec=pltpu.PrefetchScalarGridSpec(
            num_scalar_prefetch=1, grid=(n_seq,),
            in_specs=_page_specs(n_pages, page) + [const(pool_mat.shape), const((KV_LANES, 1))],
            out_specs=[out_spec, out_spec]),
        out_shape=[jax.ShapeDtypeStruct((n_seq, KV_LANES, n_col), BF16)] * 2,
        compiler_params=_params("arbitrary"),
        name="compress_pages",
    )(page_table, *([pool] * n_pages), pool_mat, kc_gain_col)


def _even_odd(blocks, half):
    b, nbc, w = blocks.shape
    out = jnp.zeros((b, 2 * half, w), BF16)
    out = out.at[:, 0:(nbc + 1) // 2].set(blocks[:, 0::2].astype(BF16))
    return out.at[:, half:half + nbc // 2].set(blocks[:, 1::2].astype(BF16))


def _qg_kernel(x_ref, g_ref, wq_ref, wg_ref, ones_ref, qgain_ref, q_ref, gate_ref):
    h = _rms(x_ref[...], g_ref[...]).astype(BF16)
    q = _dot(h, wq_ref[...])
    ones_bd = ones_ref[...]
    gain = qgain_ref[...] * (HEAD_DIM ** -0.5 * LOG2E)
    for s in range(q.shape[1] // KV_LANES):
        sl = slice(s * KV_LANES, (s + 1) * KV_LANES)
        q_ref[:, sl] = _head_rms(q[:, sl], ones_bd, gain).astype(BF16)
    gate_ref[...] = jax.nn.sigmoid(_dot(h, wg_ref[...]))


def _qg_proj(x, g, w_q, w_g, ones_bd, q_gain):
    n, d = x.shape
    tm = min(ROW_TILE, n)
    assert n % tm == 0
    nq = w_q.shape[1]
    const = lambda shape: pl.BlockSpec(shape, lambda i: (0, 0))
    return pl.pallas_call(
        _qg_kernel, grid=(n // tm,),
        in_specs=[pl.BlockSpec((tm, d), lambda i: (i, 0)), const((1, d)), const(w_q.shape), const(w_g.shape),
                  const((KV_LANES, KV_LANES)), const((1, KV_LANES))],
        out_specs=[pl.BlockSpec((tm, nq), lambda i: (i, 0)), pl.BlockSpec((tm, w_g.shape[1]), lambda i: (i, 0))],
        out_shape=[jax.ShapeDtypeStruct((n, nq), BF16), jax.ShapeDtypeStruct((n, w_g.shape[1]), F32)],
        compiler_params=_params("arbitrary"),
        name="qg_proj",
    )(x, g, w_q, w_g, ones_bd, q_gain)


def _select_blocks(imp, t, k_sel, axis):
    blk = lax.broadcasted_iota(jnp.int32, imp.shape, axis)
    cur = t // L_SEL
    forced = (blk == 0) | (blk == cur) | (blk == cur - 1)
    work = jnp.where(forced, FORCE, imp)
    work = jnp.where(blk > cur, NEG, work)
    sel = jnp.zeros(imp.shape, jnp.bool_)
    blk_f = blk.astype(F32)
    for _ in range(k_sel):
        m = jnp.max(work, axis=axis, keepdims=True)
        first = jnp.min(jnp.where(work == m, blk_f, float(imp.shape[axis])), axis=axis, keepdims=True)
        pick = blk_f == first
        sel = sel | (pick & (m > 0.5 * NEG))
        work = jnp.where(pick, -jnp.inf, work)
    return sel


def _attn_seq_kernel(q_ref, gate_ref, kc_ref, vct_ref, sk_ref, svt_ref, wk_ref, wvt_ref, o_ref,
                     ma_ref, mb_ref, acca_ref, accb_ref, xa_ref, xb_ref, ya_ref, yb_ref, *, nbc, half, k_sel):
    tq = q_ref.shape[1]
    c0 = pl.program_id(2) * tq
    cols = GROUP * tq
    t = c0 + lax.broadcasted_iota(jnp.int32, (1, cols), 1) % tq
    t1 = t[:, 0:tq]

    qt = q_ref[0].astype(F32).T
    q64 = jnp.concatenate([qt[r * HEAD_DIM:(r + 1) * HEAD_DIM] for r in range(GROUP)], axis=1)
    zeros64 = jnp.zeros((HEAD_DIM, cols), F32)
    q_pad = jnp.concatenate([q64, zeros64], axis=0).astype(BF16)

    row = lax.broadcasted_iota(jnp.int32, (2 * half, 1), 0)
    blk = 2 * (row % half) + row // half
    p = _softmax2(_dot(kc_ref[0, 0], q_pad), ((blk + 1) * L_CMP - 1 <= t) & (blk < nbc), 0)
    o_cmp = _dot(vct_ref[0, 0], p.astype(BF16))[0:HEAD_DIM]
    ph = p[:, 0:tq]
    for r in range(1, GROUP):
        ph = ph + p[:, r * tq:(r + 1) * tq]
    sel = _select_blocks(ph[0:half] + ph[half:2 * half], t1, k_sel, 0)
    bias = jnp.where(sel, 0.0, NEG)
    q_sel = jnp.concatenate([q64, jnp.concatenate([bias] * GROUP, axis=1), zeros64], axis=0).astype(BF16)

    band = tq + WINDOW
    wb = jnp.maximum(c0 - WINDOW, 0) // LANE
    w0 = pl.multiple_of(wb * LANE, LANE)
    s = _dot(wk_ref[0, 0, pl.ds(w0, band), :], q_pad)
    dist = t - (w0 + lax.broadcasted_iota(jnp.int32, (band, 1), 0))
    s = jnp.where((dist >= 0) & (dist <= WINDOW), s, NEG)
    p = jnp.exp2(s - jnp.max(s, axis=0, keepdims=True)).astype(BF16)
    wvt = jnp.concatenate([wvt_ref[0, 0, wb + j] for j in range(band // LANE)], axis=1)
    o_win = _dot(wvt, p)
    o_win = o_win[0:HEAD_DIM] * (1.0 / o_win[HEAD_DIM:HEAD_DIM + 1])

    hk = SEL_TILE // 2
    for m_ref, acc_ref in ((ma_ref, acca_ref), (mb_ref, accb_ref)):
        m_ref[...] = jnp.full(m_ref.shape, NEG, F32)
        acc_ref[...] = jnp.zeros(acc_ref.shape, F32)

    def scores(kt, h):
        k0 = pl.multiple_of(kt * SEL_TILE + h * hk, hk)
        return _dot(sk_ref[0, 0, pl.ds(k0, hk), :], q_sel)

    def update(s, kt, h, causal):
        m_ref, acc_ref = ((ma_ref, acca_ref), (mb_ref, accb_ref))[h]
        if causal:
            s = jnp.where(kt * SEL_TILE + h * hk + lax.broadcasted_iota(jnp.int32, (hk, 1), 0) <= t, s, NEG)
        m_old = m_ref[...]
        m_new = jnp.maximum(m_old, jnp.max(s, axis=0, keepdims=True))
        p = jnp.exp2(s - m_new).astype(BF16)
        vt = svt_ref[0, 0, kt][:, h * hk:(h + 1) * hk]
        acc_ref[...] = jnp.exp2(m_old - m_new) * acc_ref[...] + _dot(vt, p)
        m_ref[...] = m_new

    def fill(kt, buf):
        buf[0][...] = scores(kt, 0)
        buf[1][...] = scores(kt, 1)

    def drain(kt, buf, causal):
        update(buf[0][...], kt, 0, causal)
        update(buf[1][...], kt, 1, causal)

    last = (c0 + tq - 1) // SEL_TILE
    buf_x, buf_y = (xa_ref, xb_ref), (ya_ref, yb_ref)
    fill(0, buf_x)

    def pair(i, carry):
        kt = 2 * i
        fill(kt + 1, buf_y)
        drain(kt, buf_x, False)
        fill(kt + 2, buf_x)
        drain(kt + 1, buf_y, False)
        return carry

    lax.fori_loop(0, last // 2, pair, 0)

    @pl.when(last % 2 == 0)
    def _():
        drain(last, buf_x, True)

    @pl.when(last % 2 == 1)
    def _():
        fill(last, buf_y)
        drain(last - 1, buf_x, False)
        drain(last, buf_y, True)

    m = jnp.maximum(ma_ref[...], mb_ref[...])
    acc = jnp.exp2(ma_ref[...] - m) * acca_ref[...] + jnp.exp2(mb_ref[...] - m) * accb_ref[...]
    o_sel = acc[0:HEAD_DIM] * (1.0 / acc[HEAD_DIM:HEAD_DIM + 1])

    gt = gate_ref[0].T
    gate = lambda br: jnp.concatenate([gt[r * N_BRANCH + br:r * N_BRANCH + br + 1] for r in range(GROUP)], axis=1)
    o = gate(0) * o_cmp + gate(1) * o_sel + gate(2) * o_win
    o = jnp.concatenate([o[:, r * tq:(r + 1) * tq] for r in range(GROUP)], axis=0)
    o_ref[0] = o.T.astype(o_ref.dtype)


def _attn_seq(q, gates, kc, vct, sk, svt, wk, wvt, *, nbc, half, k_sel):
    b, t_len, nq = q.shape
    tq = Q_BLOCK
    assert tq == LANE and half == LANE and t_len % SEL_TILE == 0 and t_len >= tq + WINDOW and WINDOW % LANE == 0
    cols = GROUP * tq
    grp = lambda *shape: pl.BlockSpec((1, 1) + shape, lambda i, g, c: (i, g) + (0,) * len(shape))
    chunk = lambda w: pl.BlockSpec((1, tq, w), lambda i, g, c: (i, c, g))
    stat = pltpu.VMEM((1, cols), F32)
    acc = pltpu.VMEM((LANE, cols), F32)
    return pl.pallas_call(
        functools.partial(_attn_seq_kernel, nbc=nbc, half=half, k_sel=k_sel),
        grid=(b, N_KV, t_len // tq),
        in_specs=[chunk(KV_LANES), chunk(LANE), grp(2 * half, LANE), grp(LANE, 2 * half),
                  grp(t_len, KV_LANES), grp(t_len // SEL_TILE, LANE, SEL_TILE),
                  grp(t_len, LANE), grp(t_len // LANE, LANE, LANE)],
        out_specs=chunk(KV_LANES),
        out_shape=jax.ShapeDtypeStruct((b, t_len, nq), BF16),
        scratch_shapes=[stat, stat, acc, acc] + [pltpu.VMEM((SEL_TILE // 2, cols), F32)] * 4,
        compiler_params=_params("arbitrary", "arbitrary", "arbitrary"),
        name="attn_seq",
    )(q, gates, kc, vct, sk, svt, wk, wvt)


def _attn_paged_kernel(pt_ref, *refs, n_pages, past, nbc, half, k_sel):
    page_refs = refs[:n_pages]
    (q_ref, gate_ref, kc_ref, vc_ref, snew_ref, wold_ref, wnew_ref, e_ref, ge_ref, o_ref) = refs[n_pages:]
    tq = q_ref.shape[1]
    page = page_refs[0].shape[3]
    w_buf = wold_ref.shape[3]
    rows_g = GROUP * tq
    tok = lax.broadcasted_iota(jnp.int32, (tq, 1), 0)
    t1 = past + tok
    t = jnp.concatenate([t1] * (N_KV * GROUP), axis=0)
    lane_group = lax.broadcasted_iota(jnp.int32, (1, KV_LANES), 1) // HEAD_DIM
    q_slabs = [q_ref[0, :, r * KV_LANES:(r + 1) * KV_LANES].astype(F32) for r in range(GROUP)]
    qe = jnp.concatenate([jnp.where(lane_group == g, s, 0.0) for g in range(N_KV) for s in q_slabs], axis=0).astype(BF16)

    def pad_rows(x):
        return jnp.concatenate([x, jnp.zeros((LANE - x.shape[0], x.shape[1]), x.dtype)], axis=0).astype(BF16)

    col = lax.broadcasted_iota(jnp.int32, (1, 2 * half), 1)
    blk = 2 * (col % half) + col // half
    p = _softmax2(_dot(qe, kc_ref[0]), ((blk + 1) * L_CMP - 1 <= t) & (blk < nbc), 1)
    o_cmp = _dot_nt(p.astype(BF16), vc_ref[0])
    imps = []
    for g in range(N_KV):
        ph = p[g * rows_g:g * rows_g + tq]
        for r in range(1, GROUP):
            ph = ph + p[g * rows_g + r * tq:g * rows_g + (r + 1) * tq]
        imps.append(ph[:, 0:half] + ph[:, half:2 * half])
    sel = _select_blocks(jnp.concatenate(imps, axis=0), jnp.concatenate([t1] * N_KV, axis=0), k_sel, 1)
    bias = jnp.where(sel, 0.0, NEG)
    bias = jnp.concatenate([bias[g * tq:(g + 1) * tq] for g in range(N_KV) for _ in range(GROUP)], axis=0)

    snew = pad_rows(snew_ref[0])
    s = jnp.concatenate([_dot(qe, page_refs[i][0, 0].astype(BF16)) for i in range(n_pages)]
                        + [_dot_nt(qe, snew[:, 0:KV_LANES])], axis=1) + _dot(bias.astype(BF16), e_ref[...])
    kpos = lax.broadcasted_iota(jnp.int32, (1, s.shape[1]), 1)
    p = _softmax2(s, kpos <= t, 1).astype(BF16)
    o_sel = _dot(p[:, n_pages * page:], snew[:, KV_LANES:ROW_LANES])
    for i in range(n_pages):
        o_sel = o_sel + _dot_nt(p[:, i * page:(i + 1) * page], page_refs[i][0, 1].astype(BF16))

    wnew = pad_rows(wnew_ref[0])
    s = jnp.concatenate([_dot(qe, wold_ref[0, 0].astype(BF16)), _dot_nt(qe, wnew[:, 0:KV_LANES])], axis=1)
    j = lax.broadcasted_iota(jnp.int32, (1, s.shape[1]), 1)
    dist = t - (past - w_buf + j)
    p = _softmax2(s, (dist >= 0) & (dist <= WINDOW) & (j < w_buf + tq), 1).astype(BF16)
    o_win = _dot_nt(p[:, 0:w_buf], wold_ref[0, 1].astype(BF16)) + _dot(p[:, w_buf:], wnew[:, KV_LANES:ROW_LANES])

    gates = [_split_dot(gate_ref[0], ge_ref[br]) for br in range(N_BRANCH)]
    for r in range(GROUP):
        acc = None
        for br, o_br in enumerate((o_cmp, o_sel, o_win)):
            o = jnp.zeros((tq, KV_LANES), F32)
            for g in range(N_KV):
                o = jnp.where(lane_group == g, o_br[g * rows_g + r * tq:g * rows_g + (r + 1) * tq], o)
            term = gates[br][:, r * KV_LANES:(r + 1) * KV_LANES] * o
            acc = term if acc is None else acc + term
        o_ref[0, :, r * KV_LANES:(r + 1) * KV_LANES] = acc.astype(o_ref.dtype)


def _attn_paged(q, gates, kc, vc, pool, page_table, slc_new, win_old, win_new, e_sel, ge, *, nbc, half, k_sel):
    n_seq, tq, nq = q.shape
    n_pages = page_table.shape[1]
    page = pool.shape[3]
    past = n_pages * page
    w_buf = win_old.shape[3]
    assert page == LANE and w_buf % LANE == 0 and tq % 8 == 0 and tq <= LANE
    per_seq = lambda *shape: pl.BlockSpec((1,) + shape, lambda b, pt: (b,) + (0,) * len(shape))
    const = lambda shape: pl.BlockSpec(shape, lambda b, pt: (0,) * len(shape))
    return pl.pallas_call(
        functools.partial(_attn_paged_kernel, n_pages=n_pages, past=past, nbc=nbc, half=half, k_sel=k_sel),
        grid_spec=pltpu.PrefetchScalarGridSpec(
            num_scalar_prefetch=1, grid=(n_seq,),
            in_specs=_page_specs(n_pages, page) + [
                per_seq(tq, nq), per_seq(tq, LANE), per_seq(KV_LANES, 2 * half), per_seq(KV_LANES, 2 * half),
                per_seq(tq, ROW_LANES), per_seq(2, KV_LANES, w_buf), per_seq(tq, ROW_LANES),
                const(e_sel.shape), const(ge.shape)],
            out_specs=per_seq(tq, nq)),
        out_shape=jax.ShapeDtypeStruct((n_seq, tq, nq), BF16),
        compiler_params=_params("arbitrary"),
        name="attn_paged",
    )(page_table, *([pool] * n_pages), q, gates, kc, vc, slc_new, win_old, win_new, e_sel, ge)


def _round_up(x, m):
    return -(-x // m) * m


def _block_onehot(n_rows, n_keys, t_valid):
    key = jnp.arange(n_keys)[None, :]
    return ((key // L_SEL == jnp.arange(n_rows)[:, None]) & (key < t_valid)).astype(BF16)


def _keys_minor(rows5):
    b, t = rows5.shape[:2]
    return jnp.transpose(rows5, (0, 2, 3, 4, 1)).reshape(b, 2, KV_LANES, t)


def _rows_major(kt, row):
    b, t = kt.shape[0], kt.shape[3]
    return jnp.transpose(kt.reshape((b,) + row + (t,)), (0, 4, 1, 2, 3))


def kernel(x_prompt, x_sample, cache_cmp_kv, cache_slc_kv, cache_win_kv, state_conv, page_table, norm1, a_w_in, a_conv_w, a_w_out, kv_norm, w_kv, kc_norm, ks_norm, kw_norm, b_w_qg, b_q_norm, b_w_o, norm2, w_up, w_down):
    bp, tp, d = x_prompt.shape
    bs, ts, _ = x_sample.shape
    depth = norm1.shape[0]
    n_a = a_w_in.shape[0]
    n_pool, page = cache_cmp_kv.shape[:2]
    n_pages = page_table.shape[1]
    past = n_pages * page
    w_buf = cache_win_kv.shape[1]
    row = cache_cmp_kv.shape[2:]
    nq = N_HEADS * HEAD_DIM
    assert d == nq and row == (2, N_KV, HEAD_DIM) and past % L_CMP == 0 and page % L_CMP == 0 and ts < L_CMP

    hperm = jnp.array([(g * GROUP + r) * HEAD_DIM + dd for r in range(GROUP) for g in range(N_KV) for dd in range(HEAD_DIM)])
    lane = jnp.arange(KV_LANES)
    ones_bd = (lane[:, None] // HEAD_DIM == lane[None, :] // HEAD_DIM).astype(BF16)
    tile4 = lambda v: jnp.tile(v.astype(F32), N_KV)[None, :]
    head_of_lane = (jnp.arange(nq) // HEAD_DIM % N_KV) * GROUP + jnp.arange(nq) // KV_LANES
    gcol = jnp.arange(LANE)
    ge = jnp.stack([(gcol[:, None] == head_of_lane[None, :] * N_BRANCH + br) for br in range(N_BRANCH)]).astype(BF16)
    w_in_b = a_w_in.astype(BF16)
    w_out_b = a_w_out.astype(BF16)
    w_up_b = w_up.astype(BF16)
    w_down_b = w_down.astype(BF16)
    w_kv_b = w_kv.astype(BF16)
    w_q_nat = b_w_qg[:, :, :nq].astype(BF16)
    w_q_b = w_q_nat[:, :, hperm]
    w_g_b = jnp.pad(b_w_qg[:, :, nq:], ((0, 0), (0, 0), (0, LANE - N_HEADS * N_BRANCH))).astype(BF16)
    w_g_grp = b_w_qg[:, :, nq:].reshape(b_w_qg.shape[0], d, N_KV, GROUP * N_BRANCH)
    w_g_grp = jnp.pad(w_g_grp, ((0, 0), (0, 0), (0, 0), (0, LANE - GROUP * N_BRANCH))).reshape(-1, d, N_KV * LANE).astype(BF16)
    w_o_nat = b_w_o.astype(BF16)
    w_o_b = w_o_nat[:, hperm, :]
    src = jnp.arange(KV_LANES)[None, :, None]
    dst = jnp.arange(KV_LANES)[None, None, :]
    pk = ((src == jnp.arange(N_KV)[:, None, None] * HEAD_DIM + dst) & (dst < HEAD_DIM)).astype(BF16)
    pw = pk[:, :, 0:LANE]
    g1 = norm1.astype(F32)[:, None, :]
    g2 = norm2.astype(F32)[:, None, :]
    kvn = kv_norm.astype(F32)[None, :]

    xp = x_prompt.reshape(bp * tp, d)
    xs = x_sample.reshape(bs * ts, d)
    p_conv, s_conv = [], []
    for l in range(n_a):
        xp, st = _conv_layer(xp, g1[l], w_in_b[l], a_conv_w[l], w_out_b[l], seq_len=tp)
        p_conv.append(st)
        xs, st = _conv_layer(xs, g1[l], w_in_b[l], a_conv_w[l], w_out_b[l], seq_len=ts, state=state_conv[l])
        s_conv.append(st)
        xp = _mlp_layer(xp, g2[l], w_up_b[l], w_down_b[l])
        xs = _mlp_layer(xs, g2[l], w_up_b[l], w_down_b[l])

    ks_g, kw_g, kc_g = tile4(ks_norm), tile4(kw_norm), tile4(kc_norm)
    (p_cmp_t, p_slc_t, p_win_t, kc_p, vc_p, p_sk, p_svt, p_wk, p_wvt) = _kv_rows_long(
        xp, kvn, w_kv_b, ones_bd, ks_g, kw_g, kc_g, pk, pw, seq_len=tp)
    s_cmp, s_slc, s_win = _kv_rows(xs, kvn, w_kv_b, ones_bd, ks_g, kw_g)

    nbc_p = tp // L_CMP
    nbs_p = -(-tp // L_SEL)
    half_p = _round_up(nbs_p, LANE)
    per_group = lambda a: _even_odd(a.reshape(bp, nbc_p, KV_LANES), half_p).reshape(bp, 2 * half_p, N_KV, HEAD_DIM)
    kc_p = jnp.pad(per_group(kc_p).transpose(0, 2, 1, 3), ((0, 0), (0, 0), (0, 0), (0, LANE - HEAD_DIM)))
    vct_p = jnp.pad(per_group(vc_p).transpose(0, 2, 3, 1), ((0, 0), (0, 0), (0, LANE - HEAD_DIM), (0, 0)))

    t_all = past + ts
    nbc_s = t_all // L_CMP
    assert nbc_s * L_CMP == past
    nbs_s = -(-t_all // L_SEL)
    half_s = _round_up(nbs_s, LANE)
    pool_cmp = _keys_minor(cache_cmp_kv)
    pool_slc = _keys_minor(cache_slc_kv)
    win_old = _keys_minor(cache_win_kv)
    blk_of_key = jnp.arange(past) // L_CMP
    col_of_key = blk_of_key // 2 + half_s * (blk_of_key % 2)
    pool_mat = (col_of_key[:, None] == jnp.arange(2 * half_s)[None, :]).astype(F32) * (1.0 / L_CMP)
    pool_mat = pool_mat.reshape(n_pages, page, 2 * half_s).astype(BF16)
    kc_s, vc_s = _compress_pages(pool_cmp, page_table, pool_mat, jnp.tile(kc_norm.astype(F32), N_KV)[:, None])
    e_s = _block_onehot(half_s, past + LANE, t_all)

    for j in range(depth - n_a):
        l = n_a + j
        qg = tile4(b_q_norm[j])
        q, gates = _qg_proj(xp, g1[l], w_q_nat[j], w_g_grp[j], ones_bd, qg)
        o = _attn_seq(q.reshape(bp, tp, nq), gates.reshape(bp, tp, N_KV * LANE), kc_p, vct_p, p_sk, p_svt, p_wk, p_wvt,
                      nbc=nbc_p, half=half_p, k_sel=min(N_SEL, nbs_p))
        xp = _mlp_layer(xp, g2[l], w_up_b[l], w_down_b[l], attn=o.reshape(bp * tp, nq), w_o=w_o_nat[j])

        q, gates = _qg_proj(xs, g1[l], w_q_b[j], w_g_b[j], ones_bd, qg)
        o = _attn_paged(q.reshape(bs, ts, nq), gates.reshape(bs, ts, LANE), kc_s, vc_s, pool_slc, page_table,
                        s_slc.reshape(bs, ts, ROW_LANES), win_old, s_win.reshape(bs, ts, ROW_LANES), e_s, ge,
                        nbc=nbc_s, half=half_s, k_sel=min(N_SEL, nbs_s))
        xs = _mlp_layer(xs, g2[l], w_up_b[l], w_down_b[l], attn=o.reshape(bs * ts, nq), w_o=w_o_b[j])

    rows5 = lambda a, b_, t_: a.reshape((b_, t_) + row)
    s_win_all = jnp.concatenate([cache_win_kv, rows5(s_win, bs, ts)], axis=1)
    return (xp.reshape(bp, tp, d), xs.reshape(bs, ts, d),
            _rows_major(p_cmp_t, row), _rows_major(p_slc_t, row), _rows_major(p_win_t[..., tp - min(WINDOW, tp):], row),
            jnp.stack(p_conv),
            rows5(s_cmp, bs, ts), rows5(s_slc, bs, ts), s_win_all[:, -w_buf:], jnp.stack(s_conv))
```

```python
import functools
import math

import jax
import jax.numpy as jnp
from jax import lax
from jax.experimental import pallas as pl
from jax.experimental.pallas import tpu as pltpu

F32 = jnp.float32
BF16 = jnp.bfloat16

N_HEADS = 16
N_KV = 4
HEAD_DIM = 64
GROUP = N_HEADS // N_KV
N_BRANCH = 3
L_CMP = 32
L_SEL = 64
N_SEL = 8
WINDOW = 512
Q_BLOCK = 128
CONV_W = 3
EPS = 1e-6
NEG = -1e30
FORCE = 1e4
LOG2E = math.log2(math.e)

KV_LANES = N_KV * HEAD_DIM
ROW_LANES = 2 * KV_LANES
LANE = 128
ROW_TILE = 512
FF_CHUNK = 1024
SEL_TILE = 512
ATTN_CHUNK = 256
VMEM_LIMIT = 56 * 1024 * 1024


def _params(*sem):
    return pltpu.CompilerParams(dimension_semantics=sem, vmem_limit_bytes=VMEM_LIMIT)


def _dot(a, b):
    return jnp.dot(a, b, preferred_element_type=F32)


def _dot_nt(a, b):
    return lax.dot_general(a, b, (((1,), (1,)), ((), ())), preferred_element_type=F32)


def _rms(x, g):
    return x * lax.rsqrt(jnp.mean(x * x, axis=-1, keepdims=True) + EPS) * g


def _split_dot(x, m):
    hi = x.astype(BF16)
    lo = (x - hi.astype(F32)).astype(BF16)
    return _dot(hi, m) + _dot(lo, m)


def _head_rms(k, ones_bd, gain):
    ss = _split_dot(k * k, ones_bd)
    return k * lax.rsqrt(ss * (1.0 / HEAD_DIM) + EPS) * gain


def _softmax2(s, mask, axis):
    s = jnp.where(mask, s, NEG)
    e = jnp.exp2(s - jnp.max(s, axis=axis, keepdims=True))
    return jnp.where(mask, e, 0.0) * (1.0 / jnp.sum(e, axis=axis, keepdims=True))


def _conv_core(x, g, win_ref, cw, wout_ref, s1_fix, s2_fix):
    d = x.shape[1]
    h = _rms(x, g).astype(BF16)
    bg = _dot(h, win_ref[:, 0:d])
    cg = _dot(h, win_ref[:, d:2 * d])
    xin = _dot(h, win_ref[:, 2 * d:3 * d])
    u = cg * xin
    s1 = s1_fix(pltpu.roll(u, 1, 0))
    s2 = s2_fix(pltpu.roll(u, 2, 0))
    z = cw[0:1] * s2 + cw[1:2] * s1 + cw[2:3] * u
    y = _dot((bg * z).astype(BF16), wout_ref[...])
    return x + y, u


def _conv_long_kernel(x_ref, g_ref, win_ref, cw_ref, wout_ref, o_ref, tail_ref, carry_ref, *, tiles_per_seq):
    tm = x_ref.shape[0]

    @pl.when(pl.program_id(0) % tiles_per_seq == 0)
    def _():
        carry_ref[...] = jnp.zeros_like(carry_ref)

    prev = carry_ref[...]
    row = lax.broadcasted_iota(jnp.int32, (tm, 1), 0)
    s1_fix = lambda r: jnp.where(row == 0, prev[7:8], r)
    s2_fix = lambda r: jnp.where(row == 0, prev[6:7], jnp.where(row == 1, prev[7:8], r))
    out, u = _conv_core(x_ref[...], g_ref[...], win_ref, cw_ref[...], wout_ref, s1_fix, s2_fix)
    o_ref[...] = out
    carry_ref[...] = u[tm - 8:tm]
    tail_ref[0] = u[tm - 8:tm]


def _conv_short_kernel(x_ref, g_ref, win_ref, cw_ref, wout_ref, pa_ref, pb_ref, o_ref, u_ref, *, seg):
    tm = x_ref.shape[0]
    pos = lax.broadcasted_iota(jnp.int32, (tm, 1), 0) % seg
    s1_fix = lambda r: jnp.where(pos < 1, pa_ref[...], r)
    s2_fix = lambda r: jnp.where(pos < 2, pb_ref[...], r)
    out, u = _conv_core(x_ref[...], g_ref[...], win_ref, cw_ref[...], wout_ref, s1_fix, s2_fix)
    o_ref[...] = out
    u_ref[...] = u


def _conv_layer(x, g, w_in, cw, w_out, *, seq_len, state=None):
    n, d = x.shape
    tm = min(ROW_TILE, n)
    assert n % tm == 0
    n_seq = n // seq_len
    row_spec = pl.BlockSpec((tm, d), lambda i: (i, 0))
    w_specs = [pl.BlockSpec((1, d), lambda i: (0, 0)),
               pl.BlockSpec((d, 3 * d), lambda i: (0, 0)),
               pl.BlockSpec((CONV_W, d), lambda i: (0, 0)),
               pl.BlockSpec((d, d), lambda i: (0, 0))]
    if state is None:
        assert seq_len % tm == 0 and seq_len >= CONV_W - 1
        tiles_per_seq = seq_len // tm
        out, tail = pl.pallas_call(
            functools.partial(_conv_long_kernel, tiles_per_seq=tiles_per_seq),
            grid=(n // tm,),
            in_specs=[row_spec] + w_specs,
            out_specs=[row_spec, pl.BlockSpec((1, 8, d), lambda i: (i, 0, 0))],
            out_shape=[jax.ShapeDtypeStruct((n, d), F32), jax.ShapeDtypeStruct((n // tm, 8, d), F32)],
            scratch_shapes=[pltpu.VMEM((8, d), F32)],
            compiler_params=_params("arbitrary"),
            name="conv_long",
        )(x, g, w_in, cw, w_out)
        new_state = tail.reshape(n_seq, tiles_per_seq, 8, d)[:, -1, 8 - (CONV_W - 1):]
        return out, new_state
    assert tm % seq_len == 0 and seq_len >= CONV_W - 1
    zeros = jnp.zeros((n_seq, seq_len, d), F32)
    pa = zeros.at[:, 0].set(state[:, 1]).reshape(n, d)
    pb = zeros.at[:, 0].set(state[:, 0]).at[:, 1].set(state[:, 1]).reshape(n, d)
    out, u = pl.pallas_call(
        functools.partial(_conv_short_kernel, seg=seq_len),
        grid=(n // tm,),
        in_specs=[row_spec] + w_specs + [row_spec, row_spec],
        out_specs=[row_spec, row_spec],
        out_shape=[jax.ShapeDtypeStruct((n, d), F32), jax.ShapeDtypeStruct((n, d), F32)],
        compiler_params=_params("arbitrary"),
        name="conv_short",
    )(x, g, w_in, cw, w_out, pa, pb)
    new_state = u.reshape(n_seq, seq_len, d)[:, seq_len - (CONV_W - 1):]
    return out, new_state


def _mlp_body(x, g_ref, wup_ref, wdn_ref, o_ref):
    h = _rms(x, g_ref[...]).astype(BF16)
    acc = x
    for c in range(wup_ref.shape[1] // FF_CHUNK):
        a = _dot(h, wup_ref[:, c * FF_CHUNK:(c + 1) * FF_CHUNK])
        a = jnp.square(jnp.maximum(a, 0.0)).astype(BF16)
        acc = acc + _dot(a, wdn_ref[c * FF_CHUNK:(c + 1) * FF_CHUNK, :])
    o_ref[...] = acc


def _mlp_kernel(x_ref, g_ref, wup_ref, wdn_ref, o_ref):
    _mlp_body(x_ref[...], g_ref, wup_ref, wdn_ref, o_ref)


def _proj_mlp_kernel(x_ref, a_ref, wo_ref, g_ref, wup_ref, wdn_ref, o_ref):
    _mlp_body(x_ref[...] + _dot(a_ref[...], wo_ref[...]), g_ref, wup_ref, wdn_ref, o_ref)


def _mlp_layer(x, g, w_up, w_down, attn=None, w_o=None):
    n, d = x.shape
    ff = w_up.shape[1]
    assert ff % FF_CHUNK == 0
    tm = min(ROW_TILE, n)
    assert n % tm == 0
    row_spec = pl.BlockSpec((tm, d), lambda i: (i, 0))
    w_specs = [pl.BlockSpec((1, d), lambda i: (0, 0)),
               pl.BlockSpec((d, ff), lambda i: (0, 0)),
               pl.BlockSpec((ff, d), lambda i: (0, 0))]
    if attn is None:
        kern, in_specs, args = _mlp_kernel, [row_spec] + w_specs, (x, g, w_up, w_down)
    else:
        kern = _proj_mlp_kernel
        in_specs = [row_spec, row_spec, pl.BlockSpec((d, d), lambda i: (0, 0))] + w_specs
        args = (x, attn, w_o, g, w_up, w_down)
    return pl.pallas_call(
        kern, grid=(n // tm,), in_specs=in_specs, out_specs=row_spec,
        out_shape=jax.ShapeDtypeStruct((n, d), F32),
        compiler_params=_params("arbitrary"),
        name="mlp" if attn is None else "proj_mlp",
    )(*args)


def _kv_kernel(x_ref, g_ref, w_ref, ones_ref, ks_ref, kw_ref, cmp_ref, slc_ref, win_ref):
    h = _rms(x_ref[...], g_ref[...]).astype(BF16)
    kv = _dot(h, w_ref[...])
    ones_bd = ones_ref[...]
    cmp_ref[...] = kv[:, 0:ROW_LANES]
    for br, (gain_ref, f_ref) in enumerate(((ks_ref, slc_ref), (kw_ref, win_ref)), start=1):
        base = br * ROW_LANES
        f_ref[:, 0:KV_LANES] = _head_rms(kv[:, base:base + KV_LANES], ones_bd, gain_ref[...])
        f_ref[:, KV_LANES:ROW_LANES] = kv[:, base + KV_LANES:base + ROW_LANES]


def _kv_rows(x, kv_norm, w_kv, ones_bd, ks_gain, kw_gain):
    n, d = x.shape
    tm = min(ROW_TILE, n)
    assert n % tm == 0
    row_spec = pl.BlockSpec((tm, d), lambda i: (i, 0))
    kv_spec = pl.BlockSpec((tm, ROW_LANES), lambda i: (i, 0))
    const = lambda shape: pl.BlockSpec(shape, lambda i: (0, 0))
    return pl.pallas_call(
        _kv_kernel, grid=(n // tm,),
        in_specs=[row_spec, const((1, d)), const(w_kv.shape), const((KV_LANES, KV_LANES)),
                  const((1, KV_LANES)), const((1, KV_LANES))],
        out_specs=[kv_spec] * 3,
        out_shape=[jax.ShapeDtypeStruct((n, ROW_LANES), F32)] * 3,
        compiler_params=_params("arbitrary"),
        name="kv_rows",
    )(x, kv_norm, w_kv, ones_bd, ks_gain, kw_gain)


def _kv_long_kernel(x_ref, g_ref, w_ref, ones_ref, ks_ref, kw_ref, kc_gain_ref, pk_ref, pw_ref,
                    cmp_ref, slc_ref, win_ref, kc_ref, vc_ref, sk_ref, svt_ref, wk_ref, wvt_ref, *, tiles_per_seq):
    tm = x_ref.shape[0]
    pos0 = (pl.program_id(0) % tiles_per_seq) * tm
    h = _rms(x_ref[...], g_ref[...]).astype(BF16)
    kv = _dot(h, w_ref[...])
    ones_bd = ones_ref[...]

    cmp_k, cmp_v = kv[:, 0:KV_LANES], kv[:, KV_LANES:ROW_LANES]
    cmp_ref[0, 0] = cmp_k.T
    cmp_ref[0, 1] = cmp_v.T
    kc_ref[...] = _head_rms(jnp.mean(cmp_k.reshape(tm // L_CMP, L_CMP, KV_LANES), axis=1), ones_bd, kc_gain_ref[...])
    vc_ref[...] = jnp.mean(cmp_v.reshape(tm // L_CMP, L_CMP, KV_LANES), axis=1)

    blk = (pos0 + lax.broadcasted_iota(jnp.int32, (tm, 1), 0)) // L_SEL
    onehot = lax.broadcasted_iota(jnp.int32, (1, KV_LANES), 1) - HEAD_DIM == blk
    ones_rows = jnp.ones((HEAD_DIM, tm), F32)
    for br, (gain_ref, f_ref) in enumerate(((ks_ref, slc_ref), (kw_ref, win_ref)), start=1):
        base = br * ROW_LANES
        k = _head_rms(kv[:, base:base + KV_LANES], ones_bd, gain_ref[...])
        vt = kv[:, base + KV_LANES:base + ROW_LANES].T
        f_ref[0, 0] = k.T
        f_ref[0, 1] = vt
        kb = k.astype(BF16)
        for g in range(N_KV):
            vt_g = jnp.concatenate([vt[g * HEAD_DIM:(g + 1) * HEAD_DIM], ones_rows], axis=0).astype(BF16)
            if br == 1:
                sk_ref[0, g] = jnp.where(onehot, 1.0, _dot(kb, pk_ref[g])).astype(BF16)
                svt_ref[0, g, 0] = vt_g
            else:
                wk_ref[0, g] = _dot(kb, pw_ref[g]).astype(BF16)
                for j in range(tm // LANE):
                    wvt_ref[0, g, j] = vt_g[:, j * LANE:(j + 1) * LANE]


def _kv_rows_long(x, kv_norm, w_kv, ones_bd, ks_gain, kw_gain, kc_gain, pk, pw, *, seq_len):
    n, d = x.shape
    tm = SEL_TILE
    assert seq_len % tm == 0 and -(-seq_len // L_SEL) <= LANE and tm % (8 * L_CMP) == 0
    n_seq, tps = n // seq_len, seq_len // tm
    nb = tm // L_CMP
    row_spec = pl.BlockSpec((tm, d), lambda i: (i, 0))
    t_spec = pl.BlockSpec((1, 2, KV_LANES, tm), lambda i: (i // tps, 0, 0, i % tps))
    blk_spec = pl.BlockSpec((nb, KV_LANES), lambda i: (i, 0))
    const = lambda shape: pl.BlockSpec(shape, lambda i: (0,) * len(shape))
    grp = lambda *tail: pl.BlockSpec((1, N_KV) + tail, lambda i: (i // tps, 0, i % tps) + (0,) * (len(tail) - 1))
    sds = jax.ShapeDtypeStruct
    return pl.pallas_call(
        functools.partial(_kv_long_kernel, tiles_per_seq=tps), grid=(n // tm,),
        in_specs=[row_spec, const((1, d)), const(w_kv.shape), const((KV_LANES, KV_LANES)),
                  const((1, KV_LANES)), const((1, KV_LANES)), const((1, KV_LANES)), const(pk.shape), const(pw.shape)],
        out_specs=[t_spec] * 3 + [blk_spec] * 2 + [
            grp(tm, KV_LANES), grp(1, LANE, tm), grp(tm, LANE), grp(tm // LANE, LANE, LANE)],
        out_shape=[sds((n_seq, 2, KV_LANES, seq_len), F32)] * 3 + [sds((n // L_CMP, KV_LANES), F32)] * 2 + [
            sds((n_seq, N_KV, seq_len, KV_LANES), BF16), sds((n_seq, N_KV, tps, LANE, tm), BF16),
            sds((n_seq, N_KV, seq_len, LANE), BF16), sds((n_seq, N_KV, seq_len // LANE, LANE, LANE), BF16)],
        compiler_params=_params("arbitrary"),
        name="kv_rows_long",
    )(x, kv_norm, w_kv, ones_bd, ks_gain, kw_gain, kc_gain, pk, pw)


def _page_specs(n_pages, page):
    return [pl.BlockSpec((1, 2, KV_LANES, page), lambda b, pt, p=p: (pt[b, p], 0, 0, 0)) for p in range(n_pages)]


def _compress_pages_kernel(pt_ref, *refs, n_pages):
    page_refs = refs[:n_pages]
    pool_ref, gain_ref, kc_ref, vc_ref = refs[n_pages:]
    page = page_refs[0].shape[3]
    acc = None
    for p in range(n_pages):
        part = _dot(page_refs[p][0].reshape(ROW_LANES, page).astype(BF16), pool_ref[p])
        acc = part if acc is None else acc + part
    n_col = acc.shape[1]
    k = acc[0:KV_LANES].reshape(N_KV, HEAD_DIM, n_col)
    inv = lax.rsqrt(jnp.mean(k * k, axis=1, keepdims=True) + EPS)
    kc_ref[0] = ((k * inv).reshape(KV_LANES, n_col) * gain_ref[...]).astype(kc_ref.dtype)
    vc_ref[0] = acc[KV_LANES:ROW_LANES].astype(vc_ref.dtype)


def _compress_pages(pool, page_table, pool_mat, kc_gain_col):
    n_seq, n_pages = page_table.shape
    page = pool.shape[3]
    n_col = pool_mat.shape[2]
    const = lambda shape: pl.BlockSpec(shape, lambda b, pt: (0,) * len(shape))
    out_spec = pl.BlockSpec((1, KV_LANES, n_col), lambda b, pt: (b, 0, 0))
    return pl.pallas_call(
        functools.partial(_compress_pages_kernel, n_pages=n_pages),
        grid_spec=pltpu.PrefetchScalarGridSpec(
            num_scalar_prefetch=1, grid=(n_seq,),
            in_specs=_page_specs(n_pages, page) + [const(pool_mat.shape), const((KV_LANES, 1))],
            out_specs=[out_spec, out_spec]),
        out_shape=[jax.ShapeDtypeStruct((n_seq, KV_LANES, n_col), BF16)] * 2,
        compiler_params=_params("arbitrary"),
        name="compress_pages",
    )(page_table, *([pool] * n_pages), pool_mat, kc_gain_col)


def _even_odd(blocks, half):
    b, nbc, w = blocks.shape
    out = jnp.zeros((b, 2 * half, w), BF16)
    out = out.at[:, 0:(nbc + 1) // 2].set(blocks[:, 0::2].astype(BF16))
    return out.at[:, half:half + nbc // 2].set(blocks[:, 1::2].astype(BF16))


def _qg_kernel(x_ref, g_ref, wq_ref, wg_ref, ones_ref, qgain_ref, q_ref, gate_ref):
    h = _rms(x_ref[...], g_ref[...]).astype(BF16)
    q = _dot(h, wq_ref[...])
    ones_bd = ones_ref[...]
    gain = qgain_ref[...] * (HEAD_DIM ** -0.5 * LOG2E)
    for s in range(q.shape[1] // KV_LANES):
        sl = slice(s * KV_LANES, (s + 1) * KV_LANES)
        q_ref[:, sl] = _head_rms(q[:, sl], ones_bd, gain).astype(BF16)
    gate_ref[...] = jax.nn.sigmoid(_dot(h, wg_ref[...]))


def _qg_proj(x, g, w_q, w_g, ones_bd, q_gain):
    n, d = x.shape
    tm = min(ROW_TILE, n)
    assert n % tm == 0
    nq = w_q.shape[1]
    const = lambda shape: pl.BlockSpec(shape, lambda i: (0, 0))
    return pl.pallas_call(
        _qg_kernel, grid=(n // tm,),
        in_specs=[pl.BlockSpec((tm, d), lambda i: (i, 0)), const((1, d)), const(w_q.shape), const(w_g.shape),
                  const((KV_LANES, KV_LANES)), const((1, KV_LANES))],
        out_specs=[pl.BlockSpec((tm, nq), lambda i: (i, 0)), pl.BlockSpec((tm, w_g.shape[1]), lambda i: (i, 0))],
        out_shape=[jax.ShapeDtypeStruct((n, nq), BF16), jax.ShapeDtypeStruct((n, w_g.shape[1]), F32)],
        compiler_params=_params("arbitrary"),
        name="qg_proj",
    )(x, g, w_q, w_g, ones_bd, q_gain)


def _select_blocks(imp, t, k_sel, axis):
    blk = lax.broadcasted_iota(jnp.int32, imp.shape, axis)
    cur = t // L_SEL
    forced = (blk == 0) | (blk == cur) | (blk == cur - 1)
    n_forced = 3
    if k_sel < n_forced:
        work = jnp.where(blk > cur, NEG, jnp.where(forced, FORCE, imp))
        sel = jnp.zeros(imp.shape, jnp.bool_)
        rounds = k_sel
    else:
        work = jnp.where(forced | (blk > cur), NEG, imp)
        sel = forced & (blk <= cur)
        rounds = k_sel - n_forced
    blk_f = blk.astype(F32)
    for _ in range(rounds):
        m = jnp.max(work, axis=axis, keepdims=True)
        first = jnp.min(jnp.where(work == m, blk_f, float(imp.shape[axis])), axis=axis, keepdims=True)
        pick = blk_f == first
        sel = sel | (pick & (m > 0.5 * NEG))
        work = jnp.where(pick, -jnp.inf, work)
    return sel


def _attn_seq_kernel(q_ref, gate_ref, kc_ref, vct_ref, sk_ref, svt_ref, wk_ref, wvt_ref, wbias_ref, o_ref,
                     ma_ref, mb_ref, acca_ref, accb_ref, xa_ref, xb_ref, ya_ref, yb_ref, *, nbc, half, k_sel):
    tq = q_ref.shape[1]
    c0 = pl.program_id(2) * tq
    cols = GROUP * tq
    t = c0 + lax.broadcasted_iota(jnp.int32, (1, cols), 1) % tq
    t1 = t[:, 0:tq]

    qt = q_ref[0].astype(F32).T
    q64 = jnp.concatenate([qt[r * HEAD_DIM:(r + 1) * HEAD_DIM] for r in range(GROUP)], axis=1)
    zeros64 = jnp.zeros((HEAD_DIM, cols), F32)
    q_pad = jnp.concatenate([q64, zeros64], axis=0).astype(BF16)

    row = lax.broadcasted_iota(jnp.int32, (2 * half, 1), 0)
    blk = 2 * (row % half) + row // half
    p = _softmax2(_dot(kc_ref[0, 0], q_pad), ((blk + 1) * L_CMP - 1 <= t) & (blk < nbc), 0)
    o_cmp = _dot(vct_ref[0, 0], p.astype(BF16))[0:HEAD_DIM]
    ph = p[:, 0:tq]
    for r in range(1, GROUP):
        ph = ph + p[:, r * tq:(r + 1) * tq]
    sel = _select_blocks(ph[0:half] + ph[half:2 * half], t1, k_sel, 0)
    bias = jnp.where(sel, 0.0, NEG)
    q_sel = jnp.concatenate([q64, jnp.concatenate([bias] * GROUP, axis=1), zeros64], axis=0).astype(BF16)

    band = tq + WINDOW
    wb = jnp.maximum(c0 - WINDOW, 0) // LANE
    w0 = pl.multiple_of(wb * LANE, LANE)
    s = _dot(wk_ref[0, 0, pl.ds(w0, band), :], q_pad) + wbias_ref[0]
    p = jnp.exp2(s - jnp.max(s, axis=0, keepdims=True)).astype(BF16)
    wvt = jnp.concatenate([wvt_ref[0, 0, wb + j] for j in range(band // LANE)], axis=1)
    o_win = _dot(wvt, p)
    o_win = o_win[0:HEAD_DIM] * (1.0 / o_win[HEAD_DIM:HEAD_DIM + 1])

    hk = SEL_TILE // 2
    for m_ref, acc_ref in ((ma_ref, acca_ref), (mb_ref, accb_ref)):
        m_ref[...] = jnp.full(m_ref.shape, NEG, F32)
        acc_ref[...] = jnp.zeros(acc_ref.shape, F32)

    def scores(kt, h):
        k0 = pl.multiple_of(kt * SEL_TILE + h * hk, hk)
        return _dot(sk_ref[0, 0, pl.ds(k0, hk), :], q_sel)

    def update(s, kt, h, causal):
        m_ref, acc_ref = ((ma_ref, acca_ref), (mb_ref, accb_ref))[h]
        if causal:
            s = jnp.where(kt * SEL_TILE + h * hk + lax.broadcasted_iota(jnp.int32, (hk, 1), 0) <= t, s, NEG)
        m_old = m_ref[...]
        m_new = jnp.maximum(m_old, jnp.max(s, axis=0, keepdims=True))
        p = jnp.exp2(s - m_new).astype(BF16)
        vt = svt_ref[0, 0, kt][:, h * hk:(h + 1) * hk]
        acc_ref[...] = jnp.exp2(m_old - m_new) * acc_ref[...] + _dot(vt, p)
        m_ref[...] = m_new

    def fill(kt, buf):
        buf[0][...] = scores(kt, 0)
        buf[1][...] = scores(kt, 1)

    def drain(kt, buf, causal):
        update(buf[0][...], kt, 0, causal)
        update(buf[1][...], kt, 1, causal)

    last = (c0 + tq - 1) // SEL_TILE
    buf_x, buf_y = (xa_ref, xb_ref), (ya_ref, yb_ref)
    fill(0, buf_x)

    def pair(kt):
        fill(kt + 1, buf_y)
        drain(kt, buf_x, False)
        fill(kt + 2, buf_x)
        drain(kt + 1, buf_y, False)

    def quad(i, carry):
        pair(4 * i)
        pair(4 * i + 2)
        return carry

    def one_pair(i, carry):
        pair(2 * i)
        return carry

    lax.fori_loop(0, last // 4, quad, 0)
    lax.fori_loop(2 * (last // 4), last // 2, one_pair, 0)

    @pl.when(last % 2 == 0)
    def _():
        drain(last, buf_x, True)

    @pl.when(last % 2 == 1)
    def _():
        fill(last, buf_y)
        drain(last - 1, buf_x, False)
        drain(last, buf_y, True)

    m = jnp.maximum(ma_ref[...], mb_ref[...])
    acc = jnp.exp2(ma_ref[...] - m) * acca_ref[...] + jnp.exp2(mb_ref[...] - m) * accb_ref[...]
    o_sel = acc[0:HEAD_DIM] * (1.0 / acc[HEAD_DIM:HEAD_DIM + 1])

    gt = gate_ref[0].T
    gate = lambda br: jnp.concatenate([gt[r * N_BRANCH + br:r * N_BRANCH + br + 1] for r in range(GROUP)], axis=1)
    o = gate(0) * o_cmp + gate(1) * o_sel + gate(2) * o_win
    o = jnp.concatenate([o[:, r * tq:(r + 1) * tq] for r in range(GROUP)], axis=0)
    o_ref[0] = o.T.astype(o_ref.dtype)


def _window_bias(tq):
    v = jnp.arange(WINDOW // tq + 1)[:, None, None]
    j = jnp.arange(tq + WINDOW)[None, :, None]
    i = (jnp.arange(GROUP * tq) % tq)[None, None, :]
    dist = (v * tq + i) - (jnp.maximum(v * tq - WINDOW, 0) + j)
    return jnp.where((dist >= 0) & (dist <= WINDOW), 0.0, NEG).astype(F32)


def _attn_seq(q, gates, kc, vct, sk, svt, wk, wvt, *, nbc, half, k_sel):
    b, t_len, nq = q.shape
    tq = ATTN_CHUNK
    assert tq % LANE == 0 and half == LANE and t_len % SEL_TILE == 0 and t_len >= tq + WINDOW and WINDOW % tq == 0
    cols = GROUP * tq
    band = tq + WINDOW
    n_var = WINDOW // tq
    grp = lambda *shape: pl.BlockSpec((1, 1) + shape, lambda i, g, c: (i, g) + (0,) * len(shape))
    chunk = lambda w: pl.BlockSpec((1, tq, w), lambda i, g, c: (i, c, g))
    wbias_spec = pl.BlockSpec((1, band, cols), lambda i, g, c: (jnp.minimum(c, n_var), 0, 0))
    stat = pltpu.VMEM((1, cols), F32)
    acc = pltpu.VMEM((LANE, cols), F32)
    return pl.pallas_call(
        functools.partial(_attn_seq_kernel, nbc=nbc, half=half, k_sel=k_sel),
        grid=(b, N_KV, t_len // tq),
        in_specs=[chunk(KV_LANES), chunk(LANE), grp(2 * half, LANE), grp(LANE, 2 * half),
                  grp(t_len, KV_LANES), grp(t_len // SEL_TILE, LANE, SEL_TILE),
                  grp(t_len, LANE), grp(t_len // LANE, LANE, LANE), wbias_spec],
        out_specs=chunk(KV_LANES),
        out_shape=jax.ShapeDtypeStruct((b, t_len, nq), BF16),
        scratch_shapes=[stat, stat, acc, acc] + [pltpu.VMEM((SEL_TILE // 2, cols), F32)] * 4,
        compiler_params=_params("arbitrary", "arbitrary", "arbitrary"),
        name="attn_seq",
    )(q, gates, kc, vct, sk, svt, wk, wvt, _window_bias(tq))


def _attn_paged_kernel(pt_ref, *refs, n_pages, past, nbc, half, k_sel):
    page_refs = refs[:n_pages]
    (q_ref, gate_ref, kc_ref, vc_ref, snew_ref, wold_ref, wnew_ref, e_ref, ge_ref, o_ref) = refs[n_pages:]
    tq = q_ref.shape[1]
    page = page_refs[0].shape[3]
    w_buf = wold_ref.shape[3]
    rows_g = GROUP * tq
    tok = lax.broadcasted_iota(jnp.int32, (tq, 1), 0)
    t1 = past + tok
    t = jnp.concatenate([t1] * (N_KV * GROUP), axis=0)
    lane_group = lax.broadcasted_iota(jnp.int32, (1, KV_LANES), 1) // HEAD_DIM
    q_slabs = [q_ref[0, :, r * KV_LANES:(r + 1) * KV_LANES].astype(F32) for r in range(GROUP)]
    qe = jnp.concatenate([jnp.where(lane_group == g, s, 0.0) for g in range(N_KV) for s in q_slabs], axis=0).astype(BF16)

    def pad_rows(x):
        return jnp.concatenate([x, jnp.zeros((LANE - x.shape[0], x.shape[1]), x.dtype)], axis=0).astype(BF16)

    col = lax.broadcasted_iota(jnp.int32, (1, 2 * half), 1)
    blk = 2 * (col % half) + col // half
    p = _softmax2(_dot(qe, kc_ref[0]), ((blk + 1) * L_CMP - 1 <= t) & (blk < nbc), 1)
    o_cmp = _dot_nt(p.astype(BF16), vc_ref[0])
    imps = []
    for g in range(N_KV):
        ph = p[g * rows_g:g * rows_g + tq]
        for r in range(1, GROUP):
            ph = ph + p[g * rows_g + r * tq:g * rows_g + (r + 1) * tq]
        imps.append(ph[:, 0:half] + ph[:, half:2 * half])
    sel = _select_blocks(jnp.concatenate(imps, axis=0), jnp.concatenate([t1] * N_KV, axis=0), k_sel, 1)
    bias = jnp.where(sel, 0.0, NEG)
    bias = jnp.concatenate([bias[g * tq:(g + 1) * tq] for g in range(N_KV) for _ in range(GROUP)], axis=0)

    snew = pad_rows(snew_ref[0])
    s = jnp.concatenate([_dot(qe, page_refs[i][0, 0].astype(BF16)) for i in range(n_pages)]
                        + [_dot_nt(qe, snew[:, 0:KV_LANES])], axis=1) + _dot(bias.astype(BF16), e_ref[...])
    kpos = lax.broadcasted_iota(jnp.int32, (1, s.shape[1]), 1)
    p = _softmax2(s, kpos <= t, 1).astype(BF16)
    o_sel = _dot(p[:, n_pages * page:], snew[:, KV_LANES:ROW_LANES])
    for i in range(n_pages):
        o_sel = o_sel + _dot_nt(p[:, i * page:(i + 1) * page], page_refs[i][0, 1].astype(BF16))

    wnew = pad_rows(wnew_ref[0])
    s = jnp.concatenate([_dot(qe, wold_ref[0, 0].astype(BF16)), _dot_nt(qe, wnew[:, 0:KV_LANES])], axis=1)
    j = lax.broadcasted_iota(jnp.int32, (1, s.shape[1]), 1)
    dist = t - (past - w_buf + j)
    p = _softmax2(s, (dist >= 0) & (dist <= WINDOW) & (j < w_buf + tq), 1).astype(BF16)
    o_win = _dot_nt(p[:, 0:w_buf], wold_ref[0, 1].astype(BF16)) + _dot(p[:, w_buf:], wnew[:, KV_LANES:ROW_LANES])

    gates = [_split_dot(gate_ref[0], ge_ref[br]) for br in range(N_BRANCH)]
    for r in range(GROUP):
        acc = None
        for br, o_br in enumerate((o_cmp, o_sel, o_win)):
            o = jnp.zeros((tq, KV_LANES), F32)
            for g in range(N_KV):
                o = jnp.where(lane_group == g, o_br[g * rows_g + r * tq:g * rows_g + (r + 1) * tq], o)
            term = gates[br][:, r * KV_LANES:(r + 1) * KV_LANES] * o
            acc = term if acc is None else acc + term
        o_ref[0, :, r * KV_LANES:(r + 1) * KV_LANES] = acc.astype(o_ref.dtype)


def _attn_paged(q, gates, kc, vc, pool, page_table, slc_new, win_old, win_new, e_sel, ge, *, nbc, half, k_sel):
    n_seq, tq, nq = q.shape
    n_pages = page_table.shape[1]
    page = pool.shape[3]
    past = n_pages * page
    w_buf = win_old.shape[3]
    assert page == LANE and w_buf % LANE == 0 and tq % 8 == 0 and tq <= LANE
    per_seq = lambda *shape: pl.BlockSpec((1,) + shape, lambda b, pt: (b,) + (0,) * len(shape))
    const = lambda shape: pl.BlockSpec(shape, lambda b, pt: (0,) * len(shape))
    return pl.pallas_call(
        functools.partial(_attn_paged_kernel, n_pages=n_pages, past=past, nbc=nbc, half=half, k_sel=k_sel),
        grid_spec=pltpu.PrefetchScalarGridSpec(
            num_scalar_prefetch=1, grid=(n_seq,),
            in_specs=_page_specs(n_pages, page) + [
                per_seq(tq, nq), per_seq(tq, LANE), per_seq(KV_LANES, 2 * half), per_seq(KV_LANES, 2 * half),
                per_seq(tq, ROW_LANES), per_seq(2, KV_LANES, w_buf), per_seq(tq, ROW_LANES),
                const(e_sel.shape), const(ge.shape)],
            out_specs=per_seq(tq, nq)),
        out_shape=jax.ShapeDtypeStruct((n_seq, tq, nq), BF16),
        compiler_params=_params("arbitrary"),
        name="attn_paged",
    )(page_table, *([pool] * n_pages), q, gates, kc, vc, slc_new, win_old, win_new, e_sel, ge)


def _round_up(x, m):
    return -(-x // m) * m


def _block_onehot(n_rows, n_keys, t_valid):
    key = jnp.arange(n_keys)[None, :]
    return ((key // L_SEL == jnp.arange(n_rows)[:, None]) & (key < t_valid)).astype(BF16)


def _keys_minor(rows5):
    b, t = rows5.shape[:2]
    return jnp.transpose(rows5, (0, 2, 3, 4, 1)).reshape(b, 2, KV_LANES, t)


def _rows_major(kt, row):
    b, t = kt.shape[0], kt.shape[3]
    return jnp.transpose(kt.reshape((b,) + row + (t,)), (0, 4, 1, 2, 3))


def kernel(x_prompt, x_sample, cache_cmp_kv, cache_slc_kv, cache_win_kv, state_conv, page_table, norm1, a_w_in, a_conv_w, a_w_out, kv_norm, w_kv, kc_norm, ks_norm, kw_norm, b_w_qg, b_q_norm, b_w_o, norm2, w_up, w_down):
    bp, tp, d = x_prompt.shape
    bs, ts, _ = x_sample.shape
    depth = norm1.shape[0]
    n_a = a_w_in.shape[0]
    n_pool, page = cache_cmp_kv.shape[:2]
    n_pages = page_table.shape[1]
    past = n_pages * page
    w_buf = cache_win_kv.shape[1]
    row = cache_cmp_kv.shape[2:]
    nq = N_HEADS * HEAD_DIM
    assert d == nq and row == (2, N_KV, HEAD_DIM) and past % L_CMP == 0 and page % L_CMP == 0 and ts < L_CMP

    hperm = jnp.array([(g * GROUP + r) * HEAD_DIM + dd for r in range(GROUP) for g in range(N_KV) for dd in range(HEAD_DIM)])
    lane = jnp.arange(KV_LANES)
    ones_bd = (lane[:, None] // HEAD_DIM == lane[None, :] // HEAD_DIM).astype(BF16)
    tile4 = lambda v: jnp.tile(v.astype(F32), N_KV)[None, :]
    head_of_lane = (jnp.arange(nq) // HEAD_DIM % N_KV) * GROUP + jnp.arange(nq) // KV_LANES
    gcol = jnp.arange(LANE)
    ge = jnp.stack([(gcol[:, None] == head_of_lane[None, :] * N_BRANCH + br) for br in range(N_BRANCH)]).astype(BF16)
    w_in_b = a_w_in.astype(BF16)
    w_out_b = a_w_out.astype(BF16)
    w_up_b = w_up.astype(BF16)
    w_down_b = w_down.astype(BF16)
    w_kv_b = w_kv.astype(BF16)
    w_q_nat = b_w_qg[:, :, :nq].astype(BF16)
    w_q_b = w_q_nat[:, :, hperm]
    w_g_b = jnp.pad(b_w_qg[:, :, nq:], ((0, 0), (0, 0), (0, LANE - N_HEADS * N_BRANCH))).astype(BF16)
    w_g_grp = b_w_qg[:, :, nq:].reshape(b_w_qg.shape[0], d, N_KV, GROUP * N_BRANCH)
    w_g_grp = jnp.pad(w_g_grp, ((0, 0), (0, 0), (0, 0), (0, LANE - GROUP * N_BRANCH))).reshape(-1, d, N_KV * LANE).astype(BF16)
    w_o_nat = b_w_o.astype(BF16)
    w_o_b = w_o_nat[:, hperm, :]
    src = jnp.arange(KV_LANES)[None, :, None]
    dst = jnp.arange(KV_LANES)[None, None, :]
    pk = ((src == jnp.arange(N_KV)[:, None, None] * HEAD_DIM + dst) & (dst < HEAD_DIM)).astype(BF16)
    pw = pk[:, :, 0:LANE]
    g1 = norm1.astype(F32)[:, None, :]
    g2 = norm2.astype(F32)[:, None, :]
    kvn = kv_norm.astype(F32)[None, :]

    xp = x_prompt.reshape(bp * tp, d)
    xs = x_sample.reshape(bs * ts, d)
    p_conv, s_conv = [], []
    for l in range(n_a):
        xp, st = _conv_layer(xp, g1[l], w_in_b[l], a_conv_w[l], w_out_b[l], seq_len=tp)
        p_conv.append(st)
        xs, st = _conv_layer(xs, g1[l], w_in_b[l], a_conv_w[l], w_out_b[l], seq_len=ts, state=state_conv[l])
        s_conv.append(st)
        xp = _mlp_layer(xp, g2[l], w_up_b[l], w_down_b[l])
        xs = _mlp_layer(xs, g2[l], w_up_b[l], w_down_b[l])

    ks_g, kw_g, kc_g = tile4(ks_norm), tile4(kw_norm), tile4(kc_norm)
    (p_cmp_t, p_slc_t, p_win_t, kc_p, vc_p, p_sk, p_svt, p_wk, p_wvt) = _kv_rows_long(
        xp, kvn, w_kv_b, ones_bd, ks_g, kw_g, kc_g, pk, pw, seq_len=tp)
    s_cmp, s_slc, s_win = _kv_rows(xs, kvn, w_kv_b, ones_bd, ks_g, kw_g)

    nbc_p = tp // L_CMP
    nbs_p = -(-tp // L_SEL)
    half_p = _round_up(nbs_p, LANE)
    per_group = lambda a: _even_odd(a.reshape(bp, nbc_p, KV_LANES), half_p).reshape(bp, 2 * half_p, N_KV, HEAD_DIM)
    kc_p = jnp.pad(per_group(kc_p).transpose(0, 2, 1, 3), ((0, 0), (0, 0), (0, 0), (0, LANE - HEAD_DIM)))
    vct_p = jnp.pad(per_group(vc_p).transpose(0, 2, 3, 1), ((0, 0), (0, 0), (0, LANE - HEAD_DIM), (0, 0)))

    t_all = past + ts
    nbc_s = t_all // L_CMP
    assert nbc_s * L_CMP == past
    nbs_s = -(-t_all // L_SEL)
    half_s = _round_up(nbs_s, LANE)
    pool_cmp = _keys_minor(cache_cmp_kv)
    pool_slc = _keys_minor(cache_slc_kv)
    win_old = _keys_minor(cache_win_kv)
    blk_of_key = jnp.arange(past) // L_CMP
    col_of_key = blk_of_key // 2 + half_s * (blk_of_key % 2)
    pool_mat = (col_of_key[:, None] == jnp.arange(2 * half_s)[None, :]).astype(F32) * (1.0 / L_CMP)
    pool_mat = pool_mat.reshape(n_pages, page, 2 * half_s).astype(BF16)
    kc_s, vc_s = _compress_pages(pool_cmp, page_table, pool_mat, jnp.tile(kc_norm.astype(F32), N_KV)[:, None])
    e_s = _block_onehot(half_s, past + LANE, t_all)

    for j in range(depth - n_a):
        l = n_a + j
        qg = tile4(b_q_norm[j])
        q, gates = _qg_proj(xp, g1[l], w_q_nat[j], w_g_grp[j], ones_bd, qg)
        o = _attn_seq(q.reshape(bp, tp, nq), gates.reshape(bp, tp, N_KV * LANE), kc_p, vct_p, p_sk, p_svt, p_wk, p_wvt,
                      nbc=nbc_p, half=half_p, k_sel=min(N_SEL, nbs_p))
        xp = _mlp_layer(xp, g2[l], w_up_b[l], w_down_b[l], attn=o.reshape(bp * tp, nq), w_o=w_o_nat[j])

        q, gates = _qg_proj(xs, g1[l], w_q_b[j], w_g_b[j], ones_bd, qg)
        o = _attn_paged(q.reshape(bs, ts, nq), gates.reshape(bs, ts, LANE), kc_s, vc_s, pool_slc, page_table,
                        s_slc.reshape(bs, ts, ROW_LANES), win_old, s_win.reshape(bs, ts, ROW_LANES), e_s, ge,
                        nbc=nbc_s, half=half_s, k_sel=min(N_SEL, nbs_s))
        xs = _mlp_layer(xs, g2[l], w_up_b[l], w_down_b[l], attn=o.reshape(bs * ts, nq), w_o=w_o_b[j])

    rows5 = lambda a, b_, t_: a.reshape((b_, t_) + row)
    s_win_all = jnp.concatenate([cache_win_kv, rows5(s_win, bs, ts)], axis=1)
    return (xp.reshape(bp, tp, d), xs.reshape(bs, ts, d),
            _rows_major(p_cmp_t, row), _rows_major(p_slc_t, row), _rows_major(p_win_t[..., tp - min(WINDOW, tp):], row),
            jnp.stack(p_conv),
            rows5(s_cmp, bs, ts), rows5(s_slc, bs, ts), s_win_all[:, -w_buf:], jnp.stack(s_conv))
```

```python
import functools
import math

import jax
import jax.numpy as jnp
from jax import lax
from jax.experimental import pallas as pl
from jax.experimental.pallas import tpu as pltpu

F32 = jnp.float32
BF16 = jnp.bfloat16

N_HEADS = 16
N_KV = 4
HEAD_DIM = 64
GROUP = N_HEADS // N_KV
N_BRANCH = 3
L_CMP = 32
L_SEL = 64
N_SEL = 8
WINDOW = 512
Q_BLOCK = 128
CONV_W = 3
EPS = 1e-6
NEG = -1e30
FORCE = 1e4
LOG2E = math.log2(math.e)

KV_LANES = N_KV * HEAD_DIM
ROW_LANES = 2 * KV_LANES
LANE = 128
V_ROWS = HEAD_DIM + 16
ROW_TILE = 512
FF_CHUNK = 1024
SEL_TILE = 512
ATTN_CHUNK = 256
DECODE_SEQS = 2
VMEM_LIMIT = 56 * 1024 * 1024


def _params(*sem):
    return pltpu.CompilerParams(dimension_semantics=sem, vmem_limit_bytes=VMEM_LIMIT)


def _dot(a, b):
    return jnp.dot(a, b, preferred_element_type=F32)


def _dot_nt(a, b):
    return lax.dot_general(a, b, (((1,), (1,)), ((), ())), preferred_element_type=F32)


def _rms(x, g):
    return x * lax.rsqrt(jnp.mean(x * x, axis=-1, keepdims=True) + EPS) * g


def _split_dot(x, m):
    hi = x.astype(BF16)
    lo = (x - hi.astype(F32)).astype(BF16)
    return _dot(hi, m) + _dot(lo, m)


def _head_rms(k, ones_bd, gain):
    ss = _split_dot(k * k, ones_bd)
    return k * lax.rsqrt(ss * (1.0 / HEAD_DIM) + EPS) * gain


def _softmax2(s, mask, axis):
    s = jnp.where(mask, s, NEG)
    e = jnp.exp2(s - jnp.max(s, axis=axis, keepdims=True))
    return jnp.where(mask, e, 0.0) * (1.0 / jnp.sum(e, axis=axis, keepdims=True))


def _conv_core(x, g, win_ref, cw, wout_ref, s1_fix, s2_fix):
    d = x.shape[1]
    h = _rms(x, g).astype(BF16)
    bg = _dot(h, win_ref[:, 0:d])
    cg = _dot(h, win_ref[:, d:2 * d])
    xin = _dot(h, win_ref[:, 2 * d:3 * d])
    u = cg * xin
    s1 = s1_fix(pltpu.roll(u, 1, 0))
    s2 = s2_fix(pltpu.roll(u, 2, 0))
    z = cw[0:1] * s2 + cw[1:2] * s1 + cw[2:3] * u
    y = _dot((bg * z).astype(BF16), wout_ref[...])
    return x + y, u


def _conv_long_kernel(x_ref, g_ref, win_ref, cw_ref, wout_ref, o_ref, tail_ref, carry_ref, *, tiles_per_seq):
    tm = x_ref.shape[0]

    @pl.when(pl.program_id(0) % tiles_per_seq == 0)
    def _():
        carry_ref[...] = jnp.zeros_like(carry_ref)

    prev = carry_ref[...]
    row = lax.broadcasted_iota(jnp.int32, (tm, 1), 0)
    s1_fix = lambda r: jnp.where(row == 0, prev[7:8], r)
    s2_fix = lambda r: jnp.where(row == 0, prev[6:7], jnp.where(row == 1, prev[7:8], r))
    out, u = _conv_core(x_ref[...], g_ref[...], win_ref, cw_ref[...], wout_ref, s1_fix, s2_fix)
    o_ref[...] = out
    carry_ref[...] = u[tm - 8:tm]
    tail_ref[0] = u[tm - 8:tm]


def _conv_short_kernel(x_ref, g_ref, win_ref, cw_ref, wout_ref, pa_ref, pb_ref, o_ref, u_ref, *, seg):
    tm = x_ref.shape[0]
    pos = lax.broadcasted_iota(jnp.int32, (tm, 1), 0) % seg
    s1_fix = lambda r: jnp.where(pos < 1, pa_ref[...], r)
    s2_fix = lambda r: jnp.where(pos < 2, pb_ref[...], r)
    out, u = _conv_core(x_ref[...], g_ref[...], win_ref, cw_ref[...], wout_ref, s1_fix, s2_fix)
    o_ref[...] = out
    u_ref[...] = u


def _conv_layer(x, g, w_in, cw, w_out, *, seq_len, state=None):
    n, d = x.shape
    tm = min(ROW_TILE, n)
    assert n % tm == 0
    n_seq = n // seq_len
    row_spec = pl.BlockSpec((tm, d), lambda i: (i, 0))
    w_specs = [pl.BlockSpec((1, d), lambda i: (0, 0)),
               pl.BlockSpec((d, 3 * d), lambda i: (0, 0)),
               pl.BlockSpec((CONV_W, d), lambda i: (0, 0)),
               pl.BlockSpec((d, d), lambda i: (0, 0))]
    if state is None:
        assert seq_len % tm == 0 and seq_len >= CONV_W - 1
        tiles_per_seq = seq_len // tm
        out, tail = pl.pallas_call(
            functools.partial(_conv_long_kernel, tiles_per_seq=tiles_per_seq),
            grid=(n // tm,),
            in_specs=[row_spec] + w_specs,
            out_specs=[row_spec, pl.BlockSpec((1, 8, d), lambda i: (i, 0, 0))],
            out_shape=[jax.ShapeDtypeStruct((n, d), F32), jax.ShapeDtypeStruct((n // tm, 8, d), F32)],
            scratch_shapes=[pltpu.VMEM((8, d), F32)],
            compiler_params=_params("arbitrary"),
            name="conv_long",
        )(x, g, w_in, cw, w_out)
        new_state = tail.reshape(n_seq, tiles_per_seq, 8, d)[:, -1, 8 - (CONV_W - 1):]
        return out, new_state
    assert tm % seq_len == 0 and seq_len >= CONV_W - 1
    zeros = jnp.zeros((n_seq, seq_len, d), F32)
    pa = zeros.at[:, 0].set(state[:, 1]).reshape(n, d)
    pb = zeros.at[:, 0].set(state[:, 0]).at[:, 1].set(state[:, 1]).reshape(n, d)
    out, u = pl.pallas_call(
        functools.partial(_conv_short_kernel, seg=seq_len),
        grid=(n // tm,),
        in_specs=[row_spec] + w_specs + [row_spec, row_spec],
        out_specs=[row_spec, row_spec],
        out_shape=[jax.ShapeDtypeStruct((n, d), F32), jax.ShapeDtypeStruct((n, d), F32)],
        compiler_params=_params("arbitrary"),
        name="conv_short",
    )(x, g, w_in, cw, w_out, pa, pb)
    new_state = u.reshape(n_seq, seq_len, d)[:, seq_len - (CONV_W - 1):]
    return out, new_state


def _mlp_body(x, g_ref, wup_ref, wdn_ref, o_ref):
    h = _rms(x, g_ref[...]).astype(BF16)
    acc = x
    for c in range(wup_ref.shape[1] // FF_CHUNK):
        a = _dot(h, wup_ref[:, c * FF_CHUNK:(c + 1) * FF_CHUNK])
        a = jnp.square(jnp.maximum(a, 0.0)).astype(BF16)
        acc = acc + _dot(a, wdn_ref[c * FF_CHUNK:(c + 1) * FF_CHUNK, :])
    o_ref[...] = acc


def _mlp_kernel(x_ref, g_ref, wup_ref, wdn_ref, o_ref):
    _mlp_body(x_ref[...], g_ref, wup_ref, wdn_ref, o_ref)


def _proj_mlp_kernel(x_ref, a_ref, wo_ref, g_ref, wup_ref, wdn_ref, o_ref):
    _mlp_body(x_ref[...] + _dot(a_ref[...], wo_ref[...]), g_ref, wup_ref, wdn_ref, o_ref)


def _mlp_layer(x, g, w_up, w_down, attn=None, w_o=None):
    n, d = x.shape
    ff = w_up.shape[1]
    assert ff % FF_CHUNK == 0
    tm = min(ROW_TILE, n)
    assert n % tm == 0
    row_spec = pl.BlockSpec((tm, d), lambda i: (i, 0))
    w_specs = [pl.BlockSpec((1, d), lambda i: (0, 0)),
               pl.BlockSpec((d, ff), lambda i: (0, 0)),
               pl.BlockSpec((ff, d), lambda i: (0, 0))]
    if attn is None:
        kern, in_specs, args = _mlp_kernel, [row_spec] + w_specs, (x, g, w_up, w_down)
    else:
        kern = _proj_mlp_kernel
        in_specs = [row_spec, row_spec, pl.BlockSpec((d, d), lambda i: (0, 0))] + w_specs
        args = (x, attn, w_o, g, w_up, w_down)
    return pl.pallas_call(
        kern, grid=(n // tm,), in_specs=in_specs, out_specs=row_spec,
        out_shape=jax.ShapeDtypeStruct((n, d), F32),
        compiler_params=_params("arbitrary"),
        name="mlp" if attn is None else "proj_mlp",
    )(*args)


def _kv_kernel(x_ref, g_ref, w_ref, ones_ref, ks_ref, kw_ref, cmp_ref, slc_ref, win_ref):
    h = _rms(x_ref[...], g_ref[...]).astype(BF16)
    kv = _dot(h, w_ref[...])
    ones_bd = ones_ref[...]
    cmp_ref[...] = kv[:, 0:ROW_LANES]
    for br, (gain_ref, f_ref) in enumerate(((ks_ref, slc_ref), (kw_ref, win_ref)), start=1):
        base = br * ROW_LANES
        f_ref[:, 0:KV_LANES] = _head_rms(kv[:, base:base + KV_LANES], ones_bd, gain_ref[...])
        f_ref[:, KV_LANES:ROW_LANES] = kv[:, base + KV_LANES:base + ROW_LANES]


def _kv_rows(x, kv_norm, w_kv, ones_bd, ks_gain, kw_gain):
    n, d = x.shape
    tm = min(ROW_TILE, n)
    assert n % tm == 0
    row_spec = pl.BlockSpec((tm, d), lambda i: (i, 0))
    kv_spec = pl.BlockSpec((tm, ROW_LANES), lambda i: (i, 0))
    const = lambda shape: pl.BlockSpec(shape, lambda i: (0, 0))
    return pl.pallas_call(
        _kv_kernel, grid=(n // tm,),
        in_specs=[row_spec, const((1, d)), const(w_kv.shape), const((KV_LANES, KV_LANES)),
                  const((1, KV_LANES)), const((1, KV_LANES))],
        out_specs=[kv_spec] * 3,
        out_shape=[jax.ShapeDtypeStruct((n, ROW_LANES), F32)] * 3,
        compiler_params=_params("arbitrary"),
        name="kv_rows",
    )(x, kv_norm, w_kv, ones_bd, ks_gain, kw_gain)


def _kv_long_kernel(x_ref, g_ref, w_ref, ones_ref, ks_ref, kw_ref, kc_gain_ref, pk_ref, pw_ref,
                    cmp_ref, slc_ref, win_ref, kc_ref, vc_ref, sk_ref, svt_ref, wk_ref, wvt_ref, *, tiles_per_seq):
    tm = x_ref.shape[0]
    pos0 = (pl.program_id(0) % tiles_per_seq) * tm
    h = _rms(x_ref[...], g_ref[...]).astype(BF16)
    kv = _dot(h, w_ref[...])
    ones_bd = ones_ref[...]

    cmp_k, cmp_v = kv[:, 0:KV_LANES], kv[:, KV_LANES:ROW_LANES]
    cmp_ref[0, 0] = cmp_k.T
    cmp_ref[0, 1] = cmp_v.T
    kc_ref[...] = _head_rms(jnp.mean(cmp_k.reshape(tm // L_CMP, L_CMP, KV_LANES), axis=1), ones_bd, kc_gain_ref[...])
    vc_ref[...] = jnp.mean(cmp_v.reshape(tm // L_CMP, L_CMP, KV_LANES), axis=1)

    blk = (pos0 + lax.broadcasted_iota(jnp.int32, (tm, 1), 0)) // L_SEL
    onehot = lax.broadcasted_iota(jnp.int32, (1, KV_LANES), 1) - HEAD_DIM == blk
    ones_rows = jnp.ones((V_ROWS - HEAD_DIM, tm), F32)
    for br, (gain_ref, f_ref) in enumerate(((ks_ref, slc_ref), (kw_ref, win_ref)), start=1):
        base = br * ROW_LANES
        k = _head_rms(kv[:, base:base + KV_LANES], ones_bd, gain_ref[...])
        vt = kv[:, base + KV_LANES:base + ROW_LANES].T
        f_ref[0, 0] = k.T
        f_ref[0, 1] = vt
        kb = k.astype(BF16)
        for g in range(N_KV):
            vt_g = jnp.concatenate([vt[g * HEAD_DIM:(g + 1) * HEAD_DIM], ones_rows], axis=0).astype(BF16)
            if br == 1:
                sk_ref[0, g] = jnp.where(onehot, 1.0, _dot(kb, pk_ref[g])).astype(BF16)
                svt_ref[0, g, 0] = vt_g
            else:
                wk_ref[0, g] = _dot(kb, pw_ref[g]).astype(BF16)
                for j in range(tm // LANE):
                    wvt_ref[0, g, j] = vt_g[:, j * LANE:(j + 1) * LANE]


def _kv_rows_long(x, kv_norm, w_kv, ones_bd, ks_gain, kw_gain, kc_gain, pk, pw, *, seq_len):
    n, d = x.shape
    tm = SEL_TILE
    assert seq_len % tm == 0 and -(-seq_len // L_SEL) <= LANE and tm % (8 * L_CMP) == 0
    n_seq, tps = n // seq_len, seq_len // tm
    nb = tm // L_CMP
    row_spec = pl.BlockSpec((tm, d), lambda i: (i, 0))
    t_spec = pl.BlockSpec((1, 2, KV_LANES, tm), lambda i: (i // tps, 0, 0, i % tps))
    blk_spec = pl.BlockSpec((nb, KV_LANES), lambda i: (i, 0))
    const = lambda shape: pl.BlockSpec(shape, lambda i: (0,) * len(shape))
    grp = lambda *tail: pl.BlockSpec((1, N_KV) + tail, lambda i: (i // tps, 0, i % tps) + (0,) * (len(tail) - 1))
    sds = jax.ShapeDtypeStruct
    return pl.pallas_call(
        functools.partial(_kv_long_kernel, tiles_per_seq=tps), grid=(n // tm,),
        in_specs=[row_spec, const((1, d)), const(w_kv.shape), const((KV_LANES, KV_LANES)),
                  const((1, KV_LANES)), const((1, KV_LANES)), const((1, KV_LANES)), const(pk.shape), const(pw.shape)],
        out_specs=[t_spec] * 3 + [blk_spec] * 2 + [
            grp(tm, KV_LANES), grp(1, V_ROWS, tm), grp(tm, LANE), grp(tm // LANE, V_ROWS, LANE)],
        out_shape=[sds((n_seq, 2, KV_LANES, seq_len), F32)] * 3 + [sds((n // L_CMP, KV_LANES), F32)] * 2 + [
            sds((n_seq, N_KV, seq_len, KV_LANES), BF16), sds((n_seq, N_KV, tps, V_ROWS, tm), BF16),
            sds((n_seq, N_KV, seq_len, LANE), BF16), sds((n_seq, N_KV, seq_len // LANE, V_ROWS, LANE), BF16)],
        compiler_params=_params("arbitrary"),
        name="kv_rows_long",
    )(x, kv_norm, w_kv, ones_bd, ks_gain, kw_gain, kc_gain, pk, pw)


def _page_specs(n_pages, page, seqs=1):
    return [pl.BlockSpec((1, 2, KV_LANES, page), lambda b, pt, s=s, p=p: (pt[seqs * b + s, p], 0, 0, 0))
            for s in range(seqs) for p in range(n_pages)]


def _compress_pages_kernel(pt_ref, *refs, n_pages):
    page_refs = refs[:n_pages]
    pool_ref, gain_ref, kc_ref, vc_ref = refs[n_pages:]
    page = page_refs[0].shape[3]
    acc = None
    for p in range(n_pages):
        part = _dot(page_refs[p][0].reshape(ROW_LANES, page).astype(BF16), pool_ref[p])
        acc = part if acc is None else acc + part
    n_col = acc.shape[1]
    k = acc[0:KV_LANES].reshape(N_KV, HEAD_DIM, n_col)
    inv = lax.rsqrt(jnp.mean(k * k, axis=1, keepdims=True) + EPS)
    kc_ref[0] = ((k * inv).reshape(KV_LANES, n_col) * gain_ref[...]).astype(kc_ref.dtype)
    vc_ref[0] = acc[KV_LANES:ROW_LANES].astype(vc_ref.dtype)


def _compress_pages(pool, page_table, pool_mat, kc_gain_col):
    n_seq, n_pages = page_table.shape
    page = pool.shape[3]
    n_col = pool_mat.shape[2]
    const = lambda shape: pl.BlockSpec(shape, lambda b, pt: (0,) * len(shape))
    out_spec = pl.BlockSpec((1, KV_LANES, n_col), lambda b, pt: (b, 0, 0))
    return pl.pallas_call(
        functools.partial(_compress_pages_kernel, n_pages=n_pages),
        grid_spec=pltpu.PrefetchScalarGridSpec(
            num_scalar_prefetch=1, grid=(n_seq,),
            in_specs=_page_specs(n_pages, page) + [const(pool_mat.shape), const((KV_LANES, 1))],
            out_specs=[out_spec, out_spec]),
        out_shape=[jax.ShapeDtypeStruct((n_seq, KV_LANES, n_col), BF16)] * 2,
        compiler_params=_params("arbitrary"),
        name="compress_pages",
    )(page_table, *([pool] * n_pages), pool_mat, kc_gain_col)


def _even_odd(blocks, half):
    b, nbc, w = blocks.shape
    out = jnp.zeros((b, 2 * half, w), BF16)
    out = out.at[:, 0:(nbc + 1) // 2].set(blocks[:, 0::2].astype(BF16))
    return out.at[:, half:half + nbc // 2].set(blocks[:, 1::2].astype(BF16))


def _qg_kernel(x_ref, g_ref, wq_ref, wg_ref, ones_ref, qgain_ref, q_ref, gate_ref):
    h = _rms(x_ref[...], g_ref[...]).astype(BF16)
    q = _dot(h, wq_ref[...])
    ones_bd = ones_ref[...]
    gain = qgain_ref[...] * (HEAD_DIM ** -0.5 * LOG2E)
    for s in range(q.shape[1] // KV_LANES):
        sl = slice(s * KV_LANES, (s + 1) * KV_LANES)
        q_ref[:, sl] = _head_rms(q[:, sl], ones_bd, gain).astype(BF16)
    gate_ref[...] = jax.nn.sigmoid(_dot(h, wg_ref[...]))


def _qg_proj(x, g, w_q, w_g, ones_bd, q_gain):
    n, d = x.shape
    tm = min(ROW_TILE, n)
    assert n % tm == 0
    nq = w_q.shape[1]
    const = lambda shape: pl.BlockSpec(shape, lambda i: (0, 0))
    return pl.pallas_call(
        _qg_kernel, grid=(n // tm,),
        in_specs=[pl.BlockSpec((tm, d), lambda i: (i, 0)), const((1, d)), const(w_q.shape), const(w_g.shape),
                  const((KV_LANES, KV_LANES)), const((1, KV_LANES))],
        out_specs=[pl.BlockSpec((tm, nq), lambda i: (i, 0)), pl.BlockSpec((tm, w_g.shape[1]), lambda i: (i, 0))],
        out_shape=[jax.ShapeDtypeStruct((n, nq), BF16), jax.ShapeDtypeStruct((n, w_g.shape[1]), F32)],
        compiler_params=_params("arbitrary"),
        name="qg_proj",
    )(x, g, w_q, w_g, ones_bd, q_gain)


def _select_blocks(imp, t, k_sel, axis):
    blk = lax.broadcasted_iota(jnp.int32, imp.shape, axis)
    cur = t // L_SEL
    forced = (blk == 0) | (blk == cur) | (blk == cur - 1)
    n_forced = 3
    if k_sel < n_forced:
        work = jnp.where(blk > cur, NEG, jnp.where(forced, FORCE, imp))
        sel = jnp.zeros(imp.shape, jnp.bool_)
        rounds = k_sel
    else:
        work = jnp.where(forced | (blk > cur), NEG, imp)
        sel = forced & (blk <= cur)
        rounds = k_sel - n_forced
    blk_f = blk.astype(F32)
    for _ in range(rounds):
        m = jnp.max(work, axis=axis, keepdims=True)
        first = jnp.min(jnp.where(work == m, blk_f, float(imp.shape[axis])), axis=axis, keepdims=True)
        pick = blk_f == first
        sel = sel | (pick & (m > 0.5 * NEG))
        work = jnp.where(pick, -jnp.inf, work)
    return sel


def _attn_seq_kernel(q_ref, gate_ref, kc_ref, vct_ref, sk_ref, svt_ref, wk_ref, wvt_ref, wbias_ref, o_ref,
                     ma_ref, mb_ref, acca_ref, accb_ref, xa_ref, xb_ref, ya_ref, yb_ref, *, nbc, half, k_sel):
    tq = q_ref.shape[1]
    c0 = pl.program_id(2) * tq
    cols = GROUP * tq
    t = c0 + lax.broadcasted_iota(jnp.int32, (1, cols), 1) % tq
    t1 = t[:, 0:tq]

    qt = q_ref[0].astype(F32).T
    q64 = jnp.concatenate([qt[r * HEAD_DIM:(r + 1) * HEAD_DIM] for r in range(GROUP)], axis=1)
    zeros64 = jnp.zeros((HEAD_DIM, cols), F32)
    q_pad = jnp.concatenate([q64, zeros64], axis=0).astype(BF16)

    row = lax.broadcasted_iota(jnp.int32, (2 * half, 1), 0)
    blk = 2 * (row % half) + row // half
    p = _softmax2(_dot(kc_ref[0, 0], q_pad), ((blk + 1) * L_CMP - 1 <= t) & (blk < nbc), 0)
    o_cmp = _dot(vct_ref[0, 0], p.astype(BF16))[0:HEAD_DIM]
    ph = p[:, 0:tq]
    for r in range(1, GROUP):
        ph = ph + p[:, r * tq:(r + 1) * tq]
    sel = _select_blocks(ph[0:half] + ph[half:2 * half], t1, k_sel, 0)
    bias = jnp.where(sel, 0.0, NEG)
    q_sel = jnp.concatenate([q64, jnp.concatenate([bias] * GROUP, axis=1), zeros64], axis=0).astype(BF16)

    band = tq + WINDOW
    wb = jnp.maximum(c0 - WINDOW, 0) // LANE
    w0 = pl.multiple_of(wb * LANE, LANE)
    s = _dot(wk_ref[0, 0, pl.ds(w0, band), :], q_pad) + wbias_ref[0]
    p = jnp.exp2(s - jnp.max(s, axis=0, keepdims=True)).astype(BF16)
    wvt = jnp.concatenate([wvt_ref[0, 0, wb + j] for j in range(band // LANE)], axis=1)
    o_win = _dot(wvt, p)
    o_win = o_win[0:HEAD_DIM] * (1.0 / o_win[HEAD_DIM:HEAD_DIM + 1])

    hk = SEL_TILE // 2
    for m_ref, acc_ref in ((ma_ref, acca_ref), (mb_ref, accb_ref)):
        m_ref[...] = jnp.full(m_ref.shape, NEG, F32)
        acc_ref[...] = jnp.zeros(acc_ref.shape, F32)

    def scores(kt, h):
        k0 = pl.multiple_of(kt * SEL_TILE + h * hk, hk)
        return _dot(sk_ref[0, 0, pl.ds(k0, hk), :], q_sel)

    def update(s, kt, h, causal):
        m_ref, acc_ref = ((ma_ref, acca_ref), (mb_ref, accb_ref))[h]
        if causal:
            s = jnp.where(kt * SEL_TILE + h * hk + lax.broadcasted_iota(jnp.int32, (hk, 1), 0) <= t, s, NEG)
        m_old = m_ref[...]
        m_new = jnp.maximum(m_old, jnp.max(s, axis=0, keepdims=True))
        p = jnp.exp2(s - m_new).astype(BF16)
        vt = svt_ref[0, 0, kt][:, h * hk:(h + 1) * hk]
        acc_ref[...] = jnp.exp2(m_old - m_new) * acc_ref[...] + _dot(vt, p)
        m_ref[...] = m_new

    def fill(kt, buf):
        buf[0][...] = scores(kt, 0)
        buf[1][...] = scores(kt, 1)

    def drain(kt, buf, causal):
        update(buf[0][...], kt, 0, causal)
        update(buf[1][...], kt, 1, causal)

    last = (c0 + tq - 1) // SEL_TILE
    buf_x, buf_y = (xa_ref, xb_ref), (ya_ref, yb_ref)
    fill(0, buf_x)

    def pair(kt):
        fill(kt + 1, buf_y)
        drain(kt, buf_x, False)
        fill(kt + 2, buf_x)
        drain(kt + 1, buf_y, False)

    def quad(i, carry):
        pair(4 * i)
        pair(4 * i + 2)
        return carry

    def one_pair(i, carry):
        pair(2 * i)
        return carry

    lax.fori_loop(0, last // 4, quad, 0)
    lax.fori_loop(2 * (last // 4), last // 2, one_pair, 0)

    @pl.when(last % 2 == 0)
    def _():
        drain(last, buf_x, True)

    @pl.when(last % 2 == 1)
    def _():
        fill(last, buf_y)
        drain(last - 1, buf_x, False)
        drain(last, buf_y, True)

    m = jnp.maximum(ma_ref[...], mb_ref[...])
    acc = jnp.exp2(ma_ref[...] - m) * acca_ref[...] + jnp.exp2(mb_ref[...] - m) * accb_ref[...]
    o_sel = acc[0:HEAD_DIM] * (1.0 / acc[HEAD_DIM:HEAD_DIM + 1])

    gt = gate_ref[0].T
    gate = lambda br: jnp.concatenate([gt[r * N_BRANCH + br:r * N_BRANCH + br + 1] for r in range(GROUP)], axis=1)
    o = gate(0) * o_cmp + gate(1) * o_sel + gate(2) * o_win
    o = jnp.concatenate([o[:, r * tq:(r + 1) * tq] for r in range(GROUP)], axis=0)
    o_ref[0] = o.T.astype(o_ref.dtype)


def _window_bias(tq):
    v = jnp.arange(WINDOW // tq + 1)[:, None, None]
    j = jnp.arange(tq + WINDOW)[None, :, None]
    i = (jnp.arange(GROUP * tq) % tq)[None, None, :]
    dist = (v * tq + i) - (jnp.maximum(v * tq - WINDOW, 0) + j)
    return jnp.where((dist >= 0) & (dist <= WINDOW), 0.0, NEG).astype(F32)


def _attn_seq(q, gates, kc, vct, sk, svt, wk, wvt, *, nbc, half, k_sel):
    b, t_len, nq = q.shape
    tq = ATTN_CHUNK
    assert tq % LANE == 0 and half == LANE and t_len % SEL_TILE == 0 and t_len >= tq + WINDOW and WINDOW % tq == 0
    cols = GROUP * tq
    band = tq + WINDOW
    n_var = WINDOW // tq
    grp = lambda *shape: pl.BlockSpec((1, 1) + shape, lambda i, g, c: (i, g) + (0,) * len(shape))
    chunk = lambda w: pl.BlockSpec((1, tq, w), lambda i, g, c: (i, c, g))
    wbias_spec = pl.BlockSpec((1, band, cols), lambda i, g, c: (jnp.minimum(c, n_var), 0, 0))
    stat = pltpu.VMEM((1, cols), F32)
    acc = pltpu.VMEM((V_ROWS, cols), F32)
    return pl.pallas_call(
        functools.partial(_attn_seq_kernel, nbc=nbc, half=half, k_sel=k_sel),
        grid=(b, N_KV, t_len // tq),
        in_specs=[chunk(KV_LANES), chunk(LANE), grp(2 * half, LANE), grp(HEAD_DIM, 2 * half),
                  grp(t_len, KV_LANES), grp(t_len // SEL_TILE, V_ROWS, SEL_TILE),
                  grp(t_len, LANE), grp(t_len // LANE, V_ROWS, LANE), wbias_spec],
        out_specs=chunk(KV_LANES),
        out_shape=jax.ShapeDtypeStruct((b, t_len, nq), BF16),
        scratch_shapes=[stat, stat, acc, acc] + [pltpu.VMEM((SEL_TILE // 2, cols), F32)] * 4,
        compiler_params=_params("arbitrary", "arbitrary", "arbitrary"),
        name="attn_seq",
    )(q, gates, kc, vct, sk, svt, wk, wvt, _window_bias(tq))


def _attn_paged_kernel(pt_ref, *refs, n_pages, seqs, past, nbc, half, k_sel):
    page_refs = refs[:seqs * n_pages]
    (q_ref, gate_ref, kc_ref, vc_ref, snew_ref, wold_ref, wnew_ref, e_ref, ge_ref, o_ref) = refs[seqs * n_pages:]
    tq = q_ref.shape[1]
    page = page_refs[0].shape[3]
    w_buf = wold_ref.shape[3]
    rows_g = GROUP * tq
    t1 = past + lax.broadcasted_iota(jnp.int32, (tq, 1), 0)
    t = jnp.concatenate([t1] * (N_KV * GROUP), axis=0)
    t_col = past + lax.broadcasted_iota(jnp.int32, (1, LANE), 1) % tq
    lane_group = lax.broadcasted_iota(jnp.int32, (1, KV_LANES), 1) // HEAD_DIM
    col = lax.broadcasted_iota(jnp.int32, (1, 2 * half), 1)
    blk = 2 * (col % half) + col // half
    cmp_mask = ((blk + 1) * L_CMP - 1 <= t) & (blk < nbc)

    def pad_rows(x):
        return jnp.concatenate([x, jnp.zeros((LANE - x.shape[0], x.shape[1]), x.dtype)], axis=0).astype(BF16)

    for sq in range(seqs):
        pages = page_refs[sq * n_pages:(sq + 1) * n_pages]
        q_slabs = [q_ref[sq, :, r * KV_LANES:(r + 1) * KV_LANES].astype(F32) for r in range(GROUP)]
        qe = jnp.concatenate([jnp.where(lane_group == g, s, 0.0) for g in range(N_KV) for s in q_slabs], axis=0).astype(BF16)

        p = _softmax2(_dot(qe, kc_ref[sq]), cmp_mask, 1)
        o_cmp = _dot_nt(p.astype(BF16), vc_ref[sq])
        imps = []
        for g in range(N_KV):
            ph = p[g * rows_g:g * rows_g + tq]
            for r in range(1, GROUP):
                ph = ph + p[g * rows_g + r * tq:g * rows_g + (r + 1) * tq]
            imps.append(ph[:, 0:half] + ph[:, half:2 * half])
        imp = jnp.concatenate(imps + [jnp.zeros((LANE - N_KV * tq, half), F32)], axis=0)
        bias = jnp.where(_select_blocks(imp.T, t_col, k_sel, 0), 0.0, NEG).T
        bias = jnp.concatenate([bias[g * tq:(g + 1) * tq] for g in range(N_KV) for _ in range(GROUP)], axis=0)

        snew = pad_rows(snew_ref[sq])
        s = jnp.concatenate([_dot(qe, pages[i][0, 0].astype(BF16)) for i in range(n_pages)]
                            + [_dot_nt(qe, snew[:, 0:KV_LANES])], axis=1) + _dot(bias.astype(BF16), e_ref[...])
        kpos = lax.broadcasted_iota(jnp.int32, (1, s.shape[1]), 1)
        p = _softmax2(s, kpos <= t, 1).astype(BF16)
        o_sel = _dot(p[:, n_pages * page:], snew[:, KV_LANES:ROW_LANES])
        for i in range(n_pages):
            o_sel = o_sel + _dot_nt(p[:, i * page:(i + 1) * page], pages[i][0, 1].astype(BF16))

        wnew = pad_rows(wnew_ref[sq])
        s = jnp.concatenate([_dot(qe, wold_ref[sq, 0].astype(BF16)), _dot_nt(qe, wnew[:, 0:KV_LANES])], axis=1)
        j = lax.broadcasted_iota(jnp.int32, (1, s.shape[1]), 1)
        dist = t - (past - w_buf + j)
        p = _softmax2(s, (dist >= 0) & (dist <= WINDOW) & (j < w_buf + tq), 1).astype(BF16)
        o_win = _dot_nt(p[:, 0:w_buf], wold_ref[sq, 1].astype(BF16)) + _dot(p[:, w_buf:], wnew[:, KV_LANES:ROW_LANES])

        gates = [_split_dot(gate_ref[sq], ge_ref[br]) for br in range(N_BRANCH)]
        for r in range(GROUP):
            acc = None
            for br, o_br in enumerate((o_cmp, o_sel, o_win)):
                o = jnp.zeros((tq, KV_LANES), F32)
                for g in range(N_KV):
                    o = jnp.where(lane_group == g, o_br[g * rows_g + r * tq:g * rows_g + (r + 1) * tq], o)
                term = gates[br][:, r * KV_LANES:(r + 1) * KV_LANES] * o
                acc = term if acc is None else acc + term
            o_ref[sq, :, r * KV_LANES:(r + 1) * KV_LANES] = acc.astype(o_ref.dtype)


def _attn_paged(q, gates, kc, vc, pool, page_table, slc_new, win_old, win_new, e_sel, ge, *, nbc, half, k_sel):
    n_seq, tq, nq = q.shape
    n_pages = page_table.shape[1]
    page = pool.shape[3]
    past = n_pages * page
    w_buf = win_old.shape[3]
    seqs = DECODE_SEQS
    assert page == LANE and half == LANE and w_buf % LANE == 0 and tq % 8 == 0 and N_KV * tq <= LANE and n_seq % seqs == 0
    per_seq = lambda *shape: pl.BlockSpec((seqs,) + shape, lambda b, pt: (b,) + (0,) * len(shape))
    const = lambda shape: pl.BlockSpec(shape, lambda b, pt: (0,) * len(shape))
    return pl.pallas_call(
        functools.partial(_attn_paged_kernel, n_pages=n_pages, seqs=seqs, past=past, nbc=nbc, half=half, k_sel=k_sel),
        grid_spec=pltpu.PrefetchScalarGridSpec(
            num_scalar_prefetch=1, grid=(n_seq // seqs,),
            in_specs=_page_specs(n_pages, page, seqs) + [
                per_seq(tq, nq), per_seq(tq, LANE), per_seq(KV_LANES, 2 * half), per_seq(KV_LANES, 2 * half),
                per_seq(tq, ROW_LANES), per_seq(2, KV_LANES, w_buf), per_seq(tq, ROW_LANES),
                const(e_sel.shape), const(ge.shape)],
            out_specs=per_seq(tq, nq)),
        out_shape=jax.ShapeDtypeStruct((n_seq, tq, nq), BF16),
        compiler_params=_params("arbitrary"),
        name="attn_paged",
    )(page_table, *([pool] * (seqs * n_pages)), q, gates, kc, vc, slc_new, win_old, win_new, e_sel, ge)


def _round_up(x, m):
    return -(-x // m) * m


def _block_onehot(n_rows, n_keys, t_valid):
    key = jnp.arange(n_keys)[None, :]
    return ((key // L_SEL == jnp.arange(n_rows)[:, None]) & (key < t_valid)).astype(BF16)


def _keys_minor(rows5):
    b, t = rows5.shape[:2]
    return jnp.transpose(rows5, (0, 2, 3, 4, 1)).reshape(b, 2, KV_LANES, t)


def _rows_major(kt, row):
    b, t = kt.shape[0], kt.shape[3]
    return jnp.transpose(kt.reshape((b,) + row + (t,)), (0, 4, 1, 2, 3))


def kernel(x_prompt, x_sample, cache_cmp_kv, cache_slc_kv, cache_win_kv, state_conv, page_table, norm1, a_w_in, a_conv_w, a_w_out, kv_norm, w_kv, kc_norm, ks_norm, kw_norm, b_w_qg, b_q_norm, b_w_o, norm2, w_up, w_down):
    bp, tp, d = x_prompt.shape
    bs, ts, _ = x_sample.shape
    depth = norm1.shape[0]
    n_a = a_w_in.shape[0]
    n_pool, page = cache_cmp_kv.shape[:2]
    n_pages = page_table.shape[1]
    past = n_pages * page
    w_buf = cache_win_kv.shape[1]
    row = cache_cmp_kv.shape[2:]
    nq = N_HEADS * HEAD_DIM
    assert d == nq and row == (2, N_KV, HEAD_DIM) and past % L_CMP == 0 and page % L_CMP == 0 and ts < L_CMP

    hperm = jnp.array([(g * GROUP + r) * HEAD_DIM + dd for r in range(GROUP) for g in range(N_KV) for dd in range(HEAD_DIM)])
    lane = jnp.arange(KV_LANES)
    ones_bd = (lane[:, None] // HEAD_DIM == lane[None, :] // HEAD_DIM).astype(BF16)
    tile4 = lambda v: jnp.tile(v.astype(F32), N_KV)[None, :]
    head_of_lane = (jnp.arange(nq) // HEAD_DIM % N_KV) * GROUP + jnp.arange(nq) // KV_LANES
    gcol = jnp.arange(LANE)
    ge = jnp.stack([(gcol[:, None] == head_of_lane[None, :] * N_BRANCH + br) for br in range(N_BRANCH)]).astype(BF16)
    w_in_b = a_w_in.astype(BF16)
    w_out_b = a_w_out.astype(BF16)
    w_up_b = w_up.astype(BF16)
    w_down_b = w_down.astype(BF16)
    w_kv_b = w_kv.astype(BF16)
    w_q_nat = b_w_qg[:, :, :nq].astype(BF16)
    w_q_b = w_q_nat[:, :, hperm]
    w_g_b = jnp.pad(b_w_qg[:, :, nq:], ((0, 0), (0, 0), (0, LANE - N_HEADS * N_BRANCH))).astype(BF16)
    w_g_grp = b_w_qg[:, :, nq:].reshape(b_w_qg.shape[0], d, N_KV, GROUP * N_BRANCH)
    w_g_grp = jnp.pad(w_g_grp, ((0, 0), (0, 0), (0, 0), (0, LANE - GROUP * N_BRANCH))).reshape(-1, d, N_KV * LANE).astype(BF16)
    w_o_nat = b_w_o.astype(BF16)
    w_o_b = w_o_nat[:, hperm, :]
    src = jnp.arange(KV_LANES)[None, :, None]
    dst = jnp.arange(KV_LANES)[None, None, :]
    pk = ((src == jnp.arange(N_KV)[:, None, None] * HEAD_DIM + dst) & (dst < HEAD_DIM)).astype(BF16)
    pw = pk[:, :, 0:LANE]
    g1 = norm1.astype(F32)[:, None, :]
    g2 = norm2.astype(F32)[:, None, :]
    kvn = kv_norm.astype(F32)[None, :]

    xp = x_prompt.reshape(bp * tp, d)
    xs = x_sample.reshape(bs * ts, d)
    p_conv, s_conv = [], []
    for l in range(n_a):
        xp, st = _conv_layer(xp, g1[l], w_in_b[l], a_conv_w[l], w_out_b[l], seq_len=tp)
        p_conv.append(st)
        xs, st = _conv_layer(xs, g1[l], w_in_b[l], a_conv_w[l], w_out_b[l], seq_len=ts, state=state_conv[l])
        s_conv.append(st)
        xp = _mlp_layer(xp, g2[l], w_up_b[l], w_down_b[l])
        xs = _mlp_layer(xs, g2[l], w_up_b[l], w_down_b[l])

    ks_g, kw_g, kc_g = tile4(ks_norm), tile4(kw_norm), tile4(kc_norm)
    (p_cmp_t, p_slc_t, p_win_t, kc_p, vc_p, p_sk, p_svt, p_wk, p_wvt) = _kv_rows_long(
        xp, kvn, w_kv_b, ones_bd, ks_g, kw_g, kc_g, pk, pw, seq_len=tp)
    s_cmp, s_slc, s_win = _kv_rows(xs, kvn, w_kv_b, ones_bd, ks_g, kw_g)

    nbc_p = tp // L_CMP
    nbs_p = -(-tp // L_SEL)
    half_p = _round_up(nbs_p, LANE)
    per_group = lambda a: _even_odd(a.reshape(bp, nbc_p, KV_LANES), half_p).reshape(bp, 2 * half_p, N_KV, HEAD_DIM)
    kc_p = jnp.pad(per_group(kc_p).transpose(0, 2, 1, 3), ((0, 0), (0, 0), (0, 0), (0, LANE - HEAD_DIM)))
    vct_p = per_group(vc_p).transpose(0, 2, 3, 1)

    t_all = past + ts
    nbc_s = t_all // L_CMP
    assert nbc_s * L_CMP == past
    nbs_s = -(-t_all // L_SEL)
    half_s = _round_up(nbs_s, LANE)
    pool_cmp = _keys_minor(cache_cmp_kv)
    pool_slc = _keys_minor(cache_slc_kv)
    win_old = _keys_minor(cache_win_kv)
    blk_of_key = jnp.arange(past) // L_CMP
    col_of_key = blk_of_key // 2 + half_s * (blk_of_key % 2)
    pool_mat = (col_of_key[:, None] == jnp.arange(2 * half_s)[None, :]).astype(F32) * (1.0 / L_CMP)
    pool_mat = pool_mat.reshape(n_pages, page, 2 * half_s).astype(BF16)
    kc_s, vc_s = _compress_pages(pool_cmp, page_table, pool_mat, jnp.tile(kc_norm.astype(F32), N_KV)[:, None])
    e_s = _block_onehot(half_s, past + LANE, t_all)

    for j in range(depth - n_a):
        l = n_a + j
        qg = tile4(b_q_norm[j])
        q, gates = _qg_proj(xp, g1[l], w_q_nat[j], w_g_grp[j], ones_bd, qg)
        o = _attn_seq(q.reshape(bp, tp, nq), gates.reshape(bp, tp, N_KV * LANE), kc_p, vct_p, p_sk, p_svt, p_wk, p_wvt,
                      nbc=nbc_p, half=half_p, k_sel=min(N_SEL, nbs_p))
        xp = _mlp_layer(xp, g2[l], w_up_b[l], w_down_b[l], attn=o.reshape(bp * tp, nq), w_o=w_o_nat[j])

        q, gates = _qg_proj(xs, g1[l], w_q_b[j], w_g_b[j], ones_bd, qg)
        o = _attn_paged(q.reshape(bs, ts, nq), gates.reshape(bs, ts, LANE), kc_s, vc_s, pool_slc, page_table,
                        s_slc.reshape(bs, ts, ROW_LANES), win_old, s_win.reshape(bs, ts, ROW_LANES), e_s, ge,
                        nbc=nbc_s, half=half_s, k_sel=min(N_SEL, nbs_s))
        xs = _mlp_layer(xs, g2[l], w_up_b[l], w_down_b[l], attn=o.reshape(bs * ts, nq), w_o=w_o_b[j])

    rows5 = lambda a, b_, t_: a.reshape((b_, t_) + row)
    s_win_all = jnp.concatenate([cache_win_kv, rows5(s_win, bs, ts)], axis=1)
    return (xp.reshape(bp, tp, d), xs.reshape(bs, ts, d),
            _rows_major(p_cmp_t, row), _rows_major(p_slc_t, row), _rows_major(p_win_t[..., tp - min(WINDOW, tp):], row),
            jnp.stack(p_conv),
            rows5(s_cmp, bs, ts), rows5(s_slc, bs, ts), s_win_all[:, -w_buf:], jnp.stack(s_conv))
```

```python
import functools
import math

import jax
import jax.numpy as jnp
from jax import lax
from jax.experimental import pallas as pl
from jax.experimental.pallas import tpu as pltpu

F32 = jnp.float32
BF16 = jnp.bfloat16

N_HEADS = 16
N_KV = 4
HEAD_DIM = 64
GROUP = N_HEADS // N_KV
N_BRANCH = 3
L_CMP = 32
L_SEL = 64
N_SEL = 8
WINDOW = 512
CONV_W = 3
EPS = 1e-6
NEG = -1e30
FORCE = 1e4
LOG2E = math.log2(math.e)

KV_LANES = N_KV * HEAD_DIM
ROW_LANES = 2 * KV_LANES
LANE = 128
SUBLANE = 8
V_ROWS = HEAD_DIM + 2 * SUBLANE
ROW_TILE = 512
FF_CHUNK = 1024
SEL_TILE = 512
SWEEP_PARTS = 1
ATTN_CHUNK = 256
DECODE_SEQS = 2
V7X_VMEM_BYTES = 64 * 1024 * 1024
VMEM_LIMIT = V7X_VMEM_BYTES * 7 // 8


def _params(*sem):
    return pltpu.CompilerParams(dimension_semantics=sem, vmem_limit_bytes=VMEM_LIMIT)


def _dot(a, b):
    return jnp.dot(a, b, preferred_element_type=F32)


def _dot_nt(a, b):
    return lax.dot_general(a, b, (((1,), (1,)), ((), ())), preferred_element_type=F32)


def _rms(x, g):
    return x * lax.rsqrt(jnp.mean(x * x, axis=-1, keepdims=True) + EPS) * g


def _split_dot(x, m):
    hi = x.astype(BF16)
    lo = (x - hi.astype(F32)).astype(BF16)
    return _dot(hi, m) + _dot(lo, m)


def _head_rms(k, ones_bd, gain):
    ss = _split_dot(k * k, ones_bd)
    return k * lax.rsqrt(ss * (1.0 / HEAD_DIM) + EPS) * gain


def _softmax2(s, mask, axis):
    s = jnp.where(mask, s, NEG)
    e = jnp.exp2(s - jnp.max(s, axis=axis, keepdims=True))
    return jnp.where(mask, e, 0.0) * (1.0 / jnp.sum(e, axis=axis, keepdims=True))


def _conv_core(x, g, win_ref, cw, wout_ref, s1_fix, s2_fix):
    d = x.shape[1]
    h = _rms(x, g).astype(BF16)
    bg = _dot(h, win_ref[:, 0:d])
    cg = _dot(h, win_ref[:, d:2 * d])
    xin = _dot(h, win_ref[:, 2 * d:3 * d])
    u = cg * xin
    s1 = s1_fix(pltpu.roll(u, 1, 0))
    s2 = s2_fix(pltpu.roll(u, 2, 0))
    z = cw[0:1] * s2 + cw[1:2] * s1 + cw[2:3] * u
    y = _dot((bg * z).astype(BF16), wout_ref[...])
    return x + y, u


def _conv_long_kernel(x_ref, g_ref, win_ref, cw_ref, wout_ref, o_ref, tail_ref, carry_ref, *, tiles_per_seq):
    tm = x_ref.shape[0]

    @pl.when(pl.program_id(0) % tiles_per_seq == 0)
    def _():
        carry_ref[...] = jnp.zeros_like(carry_ref)

    prev = carry_ref[...]
    row = lax.broadcasted_iota(jnp.int32, (tm, 1), 0)
    last, before = prev[SUBLANE - 1:SUBLANE], prev[SUBLANE - 2:SUBLANE - 1]
    s1_fix = lambda r: jnp.where(row == 0, last, r)
    s2_fix = lambda r: jnp.where(row == 0, before, jnp.where(row == 1, last, r))
    out, u = _conv_core(x_ref[...], g_ref[...], win_ref, cw_ref[...], wout_ref, s1_fix, s2_fix)
    o_ref[...] = out
    carry_ref[...] = u[tm - SUBLANE:tm]
    tail_ref[0] = u[tm - SUBLANE:tm]


def _conv_short_kernel(x_ref, g_ref, win_ref, cw_ref, wout_ref, pa_ref, pb_ref, o_ref, u_ref, *, seg):
    tm = x_ref.shape[0]
    pos = lax.broadcasted_iota(jnp.int32, (tm, 1), 0) % seg
    s1_fix = lambda r: jnp.where(pos < 1, pa_ref[...], r)
    s2_fix = lambda r: jnp.where(pos < 2, pb_ref[...], r)
    out, u = _conv_core(x_ref[...], g_ref[...], win_ref, cw_ref[...], wout_ref, s1_fix, s2_fix)
    o_ref[...] = out
    u_ref[...] = u


def _conv_layer(x, g, w_in, cw, w_out, *, seq_len, state=None):
    n, d = x.shape
    tm = min(ROW_TILE, n)
    assert n % tm == 0
    n_seq = n // seq_len
    row_spec = pl.BlockSpec((tm, d), lambda i: (i, 0))
    w_specs = [pl.BlockSpec((1, d), lambda i: (0, 0)),
               pl.BlockSpec((d, 3 * d), lambda i: (0, 0)),
               pl.BlockSpec((CONV_W, d), lambda i: (0, 0)),
               pl.BlockSpec((d, d), lambda i: (0, 0))]
    if state is None:
        assert seq_len % tm == 0 and seq_len >= CONV_W - 1
        tiles_per_seq = seq_len // tm
        out, tail = pl.pallas_call(
            functools.partial(_conv_long_kernel, tiles_per_seq=tiles_per_seq),
            grid=(n // tm,),
            in_specs=[row_spec] + w_specs,
            out_specs=[row_spec, pl.BlockSpec((1, SUBLANE, d), lambda i: (i, 0, 0))],
            out_shape=[jax.ShapeDtypeStruct((n, d), F32), jax.ShapeDtypeStruct((n // tm, SUBLANE, d), F32)],
            scratch_shapes=[pltpu.VMEM((SUBLANE, d), F32)],
            compiler_params=_params("arbitrary"),
            name="conv_long",
        )(x, g, w_in, cw, w_out)
        new_state = tail.reshape(n_seq, tiles_per_seq, SUBLANE, d)[:, -1, SUBLANE - (CONV_W - 1):]
        return out, new_state
    assert tm % seq_len == 0 and seq_len >= CONV_W - 1
    zeros = jnp.zeros((n_seq, seq_len, d), F32)
    pa = zeros.at[:, 0].set(state[:, 1]).reshape(n, d)
    pb = zeros.at[:, 0].set(state[:, 0]).at[:, 1].set(state[:, 1]).reshape(n, d)
    out, u = pl.pallas_call(
        functools.partial(_conv_short_kernel, seg=seq_len),
        grid=(n // tm,),
        in_specs=[row_spec] + w_specs + [row_spec, row_spec],
        out_specs=[row_spec, row_spec],
        out_shape=[jax.ShapeDtypeStruct((n, d), F32), jax.ShapeDtypeStruct((n, d), F32)],
        compiler_params=_params("arbitrary"),
        name="conv_short",
    )(x, g, w_in, cw, w_out, pa, pb)
    new_state = u.reshape(n_seq, seq_len, d)[:, seq_len - (CONV_W - 1):]
    return out, new_state


def _mlp_body(x, g_ref, wup_ref, wdn_ref, o_ref):
    h = _rms(x, g_ref[...]).astype(BF16)
    acc = x
    for c in range(wup_ref.shape[1] // FF_CHUNK):
        a = _dot(h, wup_ref[:, c * FF_CHUNK:(c + 1) * FF_CHUNK])
        a = jnp.square(jnp.maximum(a, 0.0)).astype(BF16)
        acc = acc + _dot(a, wdn_ref[c * FF_CHUNK:(c + 1) * FF_CHUNK, :])
    o_ref[...] = acc


def _mlp_kernel(x_ref, g_ref, wup_ref, wdn_ref, o_ref):
    _mlp_body(x_ref[...], g_ref, wup_ref, wdn_ref, o_ref)


def _proj_mlp_kernel(x_ref, a_ref, wo_ref, g_ref, wup_ref, wdn_ref, o_ref):
    _mlp_body(x_ref[...] + _dot(a_ref[...], wo_ref[...]), g_ref, wup_ref, wdn_ref, o_ref)


def _mlp_layer(x, g, w_up, w_down, attn=None, w_o=None):
    n, d = x.shape
    ff = w_up.shape[1]
    assert ff % FF_CHUNK == 0
    tm = min(ROW_TILE, n)
    assert n % tm == 0
    row_spec = pl.BlockSpec((tm, d), lambda i: (i, 0))
    w_specs = [pl.BlockSpec((1, d), lambda i: (0, 0)),
               pl.BlockSpec((d, ff), lambda i: (0, 0)),
               pl.BlockSpec((ff, d), lambda i: (0, 0))]
    if attn is None:
        kern, in_specs, args = _mlp_kernel, [row_spec] + w_specs, (x, g, w_up, w_down)
    else:
        kern = _proj_mlp_kernel
        in_specs = [row_spec, row_spec, pl.BlockSpec((d, d), lambda i: (0, 0))] + w_specs
        args = (x, attn, w_o, g, w_up, w_down)
    return pl.pallas_call(
        kern, grid=(n // tm,), in_specs=in_specs, out_specs=row_spec,
        out_shape=jax.ShapeDtypeStruct((n, d), F32),
        compiler_params=_params("arbitrary"),
        name="mlp" if attn is None else "proj_mlp",
    )(*args)


def _kv_kernel(x_ref, g_ref, w_ref, ones_ref, ks_ref, kw_ref, cmp_ref, slc_ref, win_ref):
    h = _rms(x_ref[...], g_ref[...]).astype(BF16)
    kv = _dot(h, w_ref[...])
    ones_bd = ones_ref[...]
    cmp_ref[...] = kv[:, 0:ROW_LANES]
    for br, (gain_ref, f_ref) in enumerate(((ks_ref, slc_ref), (kw_ref, win_ref)), start=1):
        base = br * ROW_LANES
        f_ref[:, 0:KV_LANES] = _head_rms(kv[:, base:base + KV_LANES], ones_bd, gain_ref[...])
        f_ref[:, KV_LANES:ROW_LANES] = kv[:, base + KV_LANES:base + ROW_LANES]


def _kv_rows(x, kv_norm, w_kv, ones_bd, ks_gain, kw_gain):
    n, d = x.shape
    tm = min(ROW_TILE, n)
    assert n % tm == 0
    row_spec = pl.BlockSpec((tm, d), lambda i: (i, 0))
    kv_spec = pl.BlockSpec((tm, ROW_LANES), lambda i: (i, 0))
    const = lambda shape: pl.BlockSpec(shape, lambda i: (0, 0))
    return pl.pallas_call(
        _kv_kernel, grid=(n // tm,),
        in_specs=[row_spec, const((1, d)), const(w_kv.shape), const((KV_LANES, KV_LANES)),
                  const((1, KV_LANES)), const((1, KV_LANES))],
        out_specs=[kv_spec] * 3,
        out_shape=[jax.ShapeDtypeStruct((n, ROW_LANES), F32)] * 3,
        compiler_params=_params("arbitrary"),
        name="kv_rows",
    )(x, kv_norm, w_kv, ones_bd, ks_gain, kw_gain)


def _kv_long_kernel(x_ref, g_ref, w_ref, ones_ref, ks_ref, kw_ref, kc_gain_ref, pk_ref, pw_ref,
                    cmp_ref, slc_ref, win_ref, kc_ref, vc_ref, sk_ref, svt_ref, wk_ref, wvt_ref, *, tiles_per_seq):
    tm = x_ref.shape[0]
    pos0 = (pl.program_id(0) % tiles_per_seq) * tm
    h = _rms(x_ref[...], g_ref[...]).astype(BF16)
    kv = _dot(h, w_ref[...])
    ones_bd = ones_ref[...]

    cmp_k, cmp_v = kv[:, 0:KV_LANES], kv[:, KV_LANES:ROW_LANES]
    cmp_ref[0, 0] = cmp_k.T
    cmp_ref[0, 1] = cmp_v.T
    kc_ref[...] = _head_rms(jnp.mean(cmp_k.reshape(tm // L_CMP, L_CMP, KV_LANES), axis=1), ones_bd, kc_gain_ref[...])
    vc_ref[...] = jnp.mean(cmp_v.reshape(tm // L_CMP, L_CMP, KV_LANES), axis=1)

    blk = (pos0 + lax.broadcasted_iota(jnp.int32, (tm, 1), 0)) // L_SEL
    onehot = lax.broadcasted_iota(jnp.int32, (1, KV_LANES), 1) - HEAD_DIM == blk
    ones_rows = jnp.ones((V_ROWS - HEAD_DIM, tm), F32)
    for br, (gain_ref, f_ref) in enumerate(((ks_ref, slc_ref), (kw_ref, win_ref)), start=1):
        base = br * ROW_LANES
        k = _head_rms(kv[:, base:base + KV_LANES], ones_bd, gain_ref[...])
        vt = kv[:, base + KV_LANES:base + ROW_LANES].T
        f_ref[0, 0] = k.T
        f_ref[0, 1] = vt
        kb = k.astype(BF16)
        for g in range(N_KV):
            vt_g = jnp.concatenate([vt[g * HEAD_DIM:(g + 1) * HEAD_DIM], ones_rows], axis=0).astype(BF16)
            if br == 1:
                sk_ref[0, g] = jnp.where(onehot, 1.0, _dot(kb, pk_ref[g])).astype(BF16)
                svt_ref[0, g, 0] = vt_g
            else:
                wk_ref[0, g] = _dot(kb, pw_ref[g]).astype(BF16)
                for j in range(tm // LANE):
                    wvt_ref[0, g, j] = vt_g[:, j * LANE:(j + 1) * LANE]


def _kv_rows_long(x, kv_norm, w_kv, ones_bd, ks_gain, kw_gain, kc_gain, pk, pw, *, seq_len):
    n, d = x.shape
    tm = SEL_TILE
    assert seq_len % tm == 0 and -(-seq_len // L_SEL) <= LANE and tm % (SUBLANE * L_CMP) == 0
    n_seq, tps = n // seq_len, seq_len // tm
    nb = tm // L_CMP
    row_spec = pl.BlockSpec((tm, d), lambda i: (i, 0))
    t_spec = pl.BlockSpec((1, 2, KV_LANES, tm), lambda i: (i // tps, 0, 0, i % tps))
    blk_spec = pl.BlockSpec((nb, KV_LANES), lambda i: (i, 0))
    const = lambda shape: pl.BlockSpec(shape, lambda i: (0,) * len(shape))
    grp = lambda *tail: pl.BlockSpec((1, N_KV) + tail, lambda i: (i // tps, 0, i % tps) + (0,) * (len(tail) - 1))
    sds = jax.ShapeDtypeStruct
    return pl.pallas_call(
        functools.partial(_kv_long_kernel, tiles_per_seq=tps), grid=(n // tm,),
        in_specs=[row_spec, const((1, d)), const(w_kv.shape), const((KV_LANES, KV_LANES)),
                  const((1, KV_LANES)), const((1, KV_LANES)), const((1, KV_LANES)), const(pk.shape), const(pw.shape)],
        out_specs=[t_spec] * 3 + [blk_spec] * 2 + [
            grp(tm, KV_LANES), grp(1, V_ROWS, tm), grp(tm, LANE), grp(tm // LANE, V_ROWS, LANE)],
        out_shape=[sds((n_seq, 2, KV_LANES, seq_len), F32)] * 3 + [sds((n // L_CMP, KV_LANES), F32)] * 2 + [
            sds((n_seq, N_KV, seq_len, KV_LANES), BF16), sds((n_seq, N_KV, tps, V_ROWS, tm), BF16),
            sds((n_seq, N_KV, seq_len, LANE), BF16), sds((n_seq, N_KV, seq_len // LANE, V_ROWS, LANE), BF16)],
        compiler_params=_params("arbitrary"),
        name="kv_rows_long",
    )(x, kv_norm, w_kv, ones_bd, ks_gain, kw_gain, kc_gain, pk, pw)


def _page_specs(n_pages, page, seqs=1):
    return [pl.BlockSpec((1, 2, KV_LANES, page), lambda b, pt, s=s, p=p: (pt[seqs * b + s, p], 0, 0, 0))
            for s in range(seqs) for p in range(n_pages)]


def _compress_pages_kernel(pt_ref, *refs, n_pages):
    page_refs = refs[:n_pages]
    pool_ref, gain_ref, kc_ref, vc_ref = refs[n_pages:]
    page = page_refs[0].shape[3]
    acc = None
    for p in range(n_pages):
        part = _dot(page_refs[p][0].reshape(ROW_LANES, page), pool_ref[p].astype(F32))
        acc = part if acc is None else acc + part
    n_col = acc.shape[1]
    k = acc[0:KV_LANES].reshape(N_KV, HEAD_DIM, n_col)
    inv = lax.rsqrt(jnp.mean(k * k, axis=1, keepdims=True) + EPS)
    kc_ref[0] = ((k * inv).reshape(KV_LANES, n_col) * gain_ref[...]).astype(kc_ref.dtype)
    vc_ref[0] = acc[KV_LANES:ROW_LANES].astype(vc_ref.dtype)


def _compress_pages(pool, page_table, pool_mat, kc_gain_col):
    n_seq, n_pages = page_table.shape
    page = pool.shape[3]
    n_col = pool_mat.shape[2]
    const = lambda shape: pl.BlockSpec(shape, lambda b, pt: (0,) * len(shape))
    out_spec = pl.BlockSpec((1, KV_LANES, n_col), lambda b, pt: (b, 0, 0))
    return pl.pallas_call(
        functools.partial(_compress_pages_kernel, n_pages=n_pages),
        grid_spec=pltpu.PrefetchScalarGridSpec(
            num_scalar_prefetch=1, grid=(n_seq,),
            in_specs=_page_specs(n_pages, page) + [const(pool_mat.shape), const((KV_LANES, 1))],
            out_specs=[out_spec, out_spec]),
        out_shape=[jax.ShapeDtypeStruct((n_seq, KV_LANES, n_col), BF16)] * 2,
        compiler_params=_params("arbitrary"),
        name="compress_pages",
    )(page_table, *([pool] * n_pages), pool_mat, kc_gain_col)


def _even_odd(blocks, half):
    b, nbc, w = blocks.shape
    out = jnp.zeros((b, 2 * half, w), BF16)
    out = out.at[:, 0:(nbc + 1) // 2].set(blocks[:, 0::2].astype(BF16))
    return out.at[:, half:half + nbc // 2].set(blocks[:, 1::2].astype(BF16))


def _qg_kernel(x_ref, g_ref, wq_ref, wg_ref, ones_ref, qgain_ref, q_ref, gate_ref):
    h = _rms(x_ref[...], g_ref[...]).astype(BF16)
    q = _dot(h, wq_ref[...])
    ones_bd = ones_ref[...]
    gain = qgain_ref[...] * (HEAD_DIM ** -0.5 * LOG2E)
    for s in range(q.shape[1] // KV_LANES):
        sl = slice(s * KV_LANES, (s + 1) * KV_LANES)
        q_ref[:, sl] = _head_rms(q[:, sl], ones_bd, gain).astype(BF16)
    gate_ref[...] = jax.nn.sigmoid(_dot(h, wg_ref[...]))


def _qg_proj(x, g, w_q, w_g, ones_bd, q_gain):
    n, d = x.shape
    tm = min(ROW_TILE, n)
    assert n % tm == 0
    nq = w_q.shape[1]
    const = lambda shape: pl.BlockSpec(shape, lambda i: (0, 0))
    return pl.pallas_call(
        _qg_kernel, grid=(n // tm,),
        in_specs=[pl.BlockSpec((tm, d), lambda i: (i, 0)), const((1, d)), const(w_q.shape), const(w_g.shape),
                  const((KV_LANES, KV_LANES)), const((1, KV_LANES))],
        out_specs=[pl.BlockSpec((tm, nq), lambda i: (i, 0)), pl.BlockSpec((tm, w_g.shape[1]), lambda i: (i, 0))],
        out_shape=[jax.ShapeDtypeStruct((n, nq), BF16), jax.ShapeDtypeStruct((n, w_g.shape[1]), F32)],
        compiler_params=_params("arbitrary"),
        name="qg_proj",
    )(x, g, w_q, w_g, ones_bd, q_gain)


def _select_blocks(imp, t, k_sel, axis):
    blk = lax.broadcasted_iota(jnp.int32, imp.shape, axis)
    cur = t // L_SEL
    forced = (blk == 0) | (blk == cur) | (blk == cur - 1)
    n_forced = 3
    if k_sel < n_forced:
        work = jnp.where(blk > cur, NEG, jnp.where(forced, FORCE, imp))
        sel = jnp.zeros(imp.shape, jnp.bool_)
        rounds = k_sel
    else:
        work = jnp.where(forced | (blk > cur), NEG, imp)
        sel = forced & (blk <= cur)
        rounds = k_sel - n_forced
    blk_f = blk.astype(F32)
    for _ in range(rounds):
        m = jnp.max(work, axis=axis, keepdims=True)
        first = jnp.min(jnp.where(work == m, blk_f, float(imp.shape[axis])), axis=axis, keepdims=True)
        pick = blk_f == first
        sel = sel | (pick & (m > 0.5 * NEG))
        work = jnp.where(pick, -jnp.inf, work)
    return sel


def _attn_seq_kernel(q_ref, gate_ref, kc_ref, vct_ref, sk_ref, svt_ref, wk_ref, wvt_ref, wbias_ref, o_ref,
                     *scratch, nbc, half, k_sel):
    tq = q_ref.shape[1]
    c0 = pl.program_id(2) * tq
    cols = GROUP * tq
    t = c0 + lax.broadcasted_iota(jnp.int32, (1, cols), 1) % tq
    t1 = t[:, 0:tq]

    qt = q_ref[0].astype(F32).T
    q64 = jnp.concatenate([qt[r * HEAD_DIM:(r + 1) * HEAD_DIM] for r in range(GROUP)], axis=1)
    zeros64 = jnp.zeros((HEAD_DIM, cols), F32)
    q_pad = jnp.concatenate([q64, zeros64], axis=0).astype(BF16)

    row = lax.broadcasted_iota(jnp.int32, (2 * half, 1), 0)
    blk = 2 * (row % half) + row // half
    p = _softmax2(_dot(kc_ref[0, 0], q_pad), ((blk + 1) * L_CMP - 1 <= t) & (blk < nbc), 0)
    o_cmp = _dot(vct_ref[0, 0], p.astype(BF16))[0:HEAD_DIM]
    ph = p[:, 0:tq]
    for r in range(1, GROUP):
        ph = ph + p[:, r * tq:(r + 1) * tq]
    sel = _select_blocks(ph[0:half] + ph[half:2 * half], t1, k_sel, 0)
    bias = jnp.where(sel, 0.0, NEG)
    q_sel = jnp.concatenate([q64, jnp.concatenate([bias] * GROUP, axis=1), zeros64], axis=0).astype(BF16)

    band = tq + WINDOW
    wb = jnp.maximum(c0 - WINDOW, 0) // LANE
    w0 = pl.multiple_of(wb * LANE, LANE)
    s = _dot(wk_ref[0, 0, pl.ds(w0, band), :], q_pad) + wbias_ref[0]
    p = jnp.exp2(s - jnp.max(s, axis=0, keepdims=True)).astype(BF16)
    wvt = jnp.concatenate([wvt_ref[0, 0, wb + j] for j in range(band // LANE)], axis=1)
    o_win = _dot(wvt, p)
    o_win = o_win[0:HEAD_DIM] * (1.0 / o_win[HEAD_DIM:HEAD_DIM + 1])

    ns = SWEEP_PARTS
    m_refs, acc_refs = scratch[0:ns], scratch[ns:2 * ns]
    buf_x, buf_y = scratch[2 * ns:3 * ns], scratch[3 * ns:4 * ns]
    hk = SEL_TILE // ns
    for m_ref, acc_ref in zip(m_refs, acc_refs):
        m_ref[...] = jnp.full(m_ref.shape, NEG, F32)
        acc_ref[...] = jnp.zeros(acc_ref.shape, F32)

    def scores(kt, h):
        k0 = pl.multiple_of(kt * SEL_TILE + h * hk, hk)
        return _dot(sk_ref[0, 0, pl.ds(k0, hk), :], q_sel)

    def update(s, kt, h, causal):
        m_ref, acc_ref = m_refs[h], acc_refs[h]
        if causal:
            s = jnp.where(kt * SEL_TILE + h * hk + lax.broadcasted_iota(jnp.int32, (hk, 1), 0) <= t, s, NEG)
        m_old = m_ref[...]
        m_new = jnp.maximum(m_old, jnp.max(s, axis=0, keepdims=True))
        p = jnp.exp2(s - m_new).astype(BF16)
        vt = svt_ref[0, 0, kt][:, h * hk:(h + 1) * hk]
        acc_ref[...] = jnp.exp2(m_old - m_new) * acc_ref[...] + _dot(vt, p)
        m_ref[...] = m_new

    def fill(kt, buf):
        for h in range(ns):
            buf[h][...] = scores(kt, h)

    def drain(kt, buf, causal):
        for h in range(ns):
            update(buf[h][...], kt, h, causal)

    last = (c0 + tq - 1) // SEL_TILE
    fill(0, buf_x)

    def pair(kt):
        fill(kt + 1, buf_y)
        drain(kt, buf_x, False)
        fill(kt + 2, buf_x)
        drain(kt + 1, buf_y, False)

    def quad(i, carry):
        pair(4 * i)
        pair(4 * i + 2)
        return carry

    def one_pair(i, carry):
        pair(2 * i)
        return carry

    lax.fori_loop(0, last // 4, quad, 0)
    lax.fori_loop(2 * (last // 4), last // 2, one_pair, 0)

    @pl.when(last % 2 == 0)
    def _():
        drain(last, buf_x, True)

    @pl.when(last % 2 == 1)
    def _():
        fill(last, buf_y)
        drain(last - 1, buf_x, False)
        drain(last, buf_y, True)

    m = functools.reduce(jnp.maximum, [m_ref[...] for m_ref in m_refs])
    acc = sum(jnp.exp2(m_ref[...] - m) * acc_ref[...] for m_ref, acc_ref in zip(m_refs, acc_refs))
    o_sel = acc[0:HEAD_DIM] * (1.0 / acc[HEAD_DIM:HEAD_DIM + 1])

    gt = gate_ref[0].T
    gate = lambda br: jnp.concatenate([gt[r * N_BRANCH + br:r * N_BRANCH + br + 1] for r in range(GROUP)], axis=1)
    o = gate(0) * o_cmp + gate(1) * o_sel + gate(2) * o_win
    o = jnp.concatenate([o[:, r * tq:(r + 1) * tq] for r in range(GROUP)], axis=0)
    o_ref[0] = o.T.astype(o_ref.dtype)


def _window_bias(tq):
    v = jnp.arange(WINDOW // tq + 1)[:, None, None]
    j = jnp.arange(tq + WINDOW)[None, :, None]
    i = (jnp.arange(GROUP * tq) % tq)[None, None, :]
    dist = (v * tq + i) - (jnp.maximum(v * tq - WINDOW, 0) + j)
    return jnp.where((dist >= 0) & (dist <= WINDOW), 0.0, NEG).astype(F32)


def _attn_seq(q, gates, kc, vct, sk, svt, wk, wvt, *, nbc, half, k_sel):
    b, t_len, nq = q.shape
    tq = ATTN_CHUNK
    assert tq % LANE == 0 and half == LANE and t_len % SEL_TILE == 0 and t_len >= tq + WINDOW and WINDOW % tq == 0
    cols = GROUP * tq
    band = tq + WINDOW
    n_var = WINDOW // tq
    grp = lambda *shape: pl.BlockSpec((1, 1) + shape, lambda i, g, c: (i, g) + (0,) * len(shape))
    chunk = lambda w: pl.BlockSpec((1, tq, w), lambda i, g, c: (i, c, g))
    wbias_spec = pl.BlockSpec((1, band, cols), lambda i, g, c: (jnp.minimum(c, n_var), 0, 0))
    stat = pltpu.VMEM((1, cols), F32)
    acc = pltpu.VMEM((V_ROWS, cols), F32)
    return pl.pallas_call(
        functools.partial(_attn_seq_kernel, nbc=nbc, half=half, k_sel=k_sel),
        grid=(b, N_KV, t_len // tq),
        in_specs=[chunk(KV_LANES), chunk(LANE), grp(2 * half, LANE), grp(HEAD_DIM, 2 * half),
                  grp(t_len, KV_LANES), grp(t_len // SEL_TILE, V_ROWS, SEL_TILE),
                  grp(t_len, LANE), grp(t_len // LANE, V_ROWS, LANE), wbias_spec],
        out_specs=chunk(KV_LANES),
        out_shape=jax.ShapeDtypeStruct((b, t_len, nq), BF16),
        scratch_shapes=[stat] * SWEEP_PARTS + [acc] * SWEEP_PARTS
        + [pltpu.VMEM((SEL_TILE // SWEEP_PARTS, cols), F32)] * (2 * SWEEP_PARTS),
        compiler_params=_params("arbitrary", "arbitrary", "arbitrary"),
        name="attn_seq",
    )(q, gates, kc, vct, sk, svt, wk, wvt, _window_bias(tq))


def _attn_paged_kernel(pt_ref, *refs, n_pages, seqs, past, nbc, half, k_sel):
    page_refs = refs[:seqs * n_pages]
    (q_ref, gate_ref, kc_ref, vc_ref, snew_ref, wold_ref, wnew_ref, e_ref, ge_ref, o_ref) = refs[seqs * n_pages:]
    tq = q_ref.shape[1]
    page = page_refs[0].shape[3]
    w_buf = wold_ref.shape[3]
    rows_g = GROUP * tq
    t1 = past + lax.broadcasted_iota(jnp.int32, (tq, 1), 0)
    t = jnp.concatenate([t1] * (N_KV * GROUP), axis=0)
    t_col = past + lax.broadcasted_iota(jnp.int32, (1, LANE), 1) % tq
    lane_group = lax.broadcasted_iota(jnp.int32, (1, KV_LANES), 1) // HEAD_DIM
    col = lax.broadcasted_iota(jnp.int32, (1, 2 * half), 1)
    blk = 2 * (col % half) + col // half
    cmp_mask = ((blk + 1) * L_CMP - 1 <= t) & (blk < nbc)

    def pad_rows(x):
        return jnp.concatenate([x, jnp.zeros((LANE - x.shape[0], x.shape[1]), x.dtype)], axis=0).astype(BF16)

    for sq in range(seqs):
        pages = page_refs[sq * n_pages:(sq + 1) * n_pages]
        q_slabs = [q_ref[sq, :, r * KV_LANES:(r + 1) * KV_LANES].astype(F32) for r in range(GROUP)]
        qe = jnp.concatenate([jnp.where(lane_group == g, s, 0.0) for g in range(N_KV) for s in q_slabs], axis=0).astype(BF16)

        p = _softmax2(_dot(qe, kc_ref[sq]), cmp_mask, 1)
        o_cmp = _dot_nt(p.astype(BF16), vc_ref[sq])
        imps = []
        for g in range(N_KV):
            ph = p[g * rows_g:g * rows_g + tq]
            for r in range(1, GROUP):
                ph = ph + p[g * rows_g + r * tq:g * rows_g + (r + 1) * tq]
            imps.append(ph[:, 0:half] + ph[:, half:2 * half])
        imp = jnp.concatenate(imps + [jnp.zeros((LANE - N_KV * tq, half), F32)], axis=0)
        bias = jnp.where(_select_blocks(imp.T, t_col, k_sel, 0), 0.0, NEG).T
        bias = jnp.concatenate([bias[g * tq:(g + 1) * tq] for g in range(N_KV) for _ in range(GROUP)], axis=0)

        snew = pad_rows(snew_ref[sq])
        s = jnp.concatenate([_dot(qe, pages[i][0, 0].astype(BF16)) for i in range(n_pages)]
                            + [_dot_nt(qe, snew[:, 0:KV_LANES])], axis=1) + _dot(bias.astype(BF16), e_ref[...])
        kpos = lax.broadcasted_iota(jnp.int32, (1, s.shape[1]), 1)
        p = _softmax2(s, kpos <= t, 1).astype(BF16)
        o_sel = _dot(p[:, n_pages * page:], snew[:, KV_LANES:ROW_LANES])
        for i in range(n_pages):
            o_sel = o_sel + _dot_nt(p[:, i * page:(i + 1) * page], pages[i][0, 1].astype(BF16))

        wnew = pad_rows(wnew_ref[sq])
        s = jnp.concatenate([_dot(qe, wold_ref[sq, 0].astype(BF16)), _dot_nt(qe, wnew[:, 0:KV_LANES])], axis=1)
        j = lax.broadcasted_iota(jnp.int32, (1, s.shape[1]), 1)
        dist = t - (past - w_buf + j)
        p = _softmax2(s, (dist >= 0) & (dist <= WINDOW) & (j < w_buf + tq), 1).astype(BF16)
        o_win = _dot_nt(p[:, 0:w_buf], wold_ref[sq, 1].astype(BF16)) + _dot(p[:, w_buf:], wnew[:, KV_LANES:ROW_LANES])

        gates = [_split_dot(gate_ref[sq], ge_ref[br]) for br in range(N_BRANCH)]
        for r in range(GROUP):
            acc = None
            for br, o_br in enumerate((o_cmp, o_sel, o_win)):
                o = jnp.zeros((tq, KV_LANES), F32)
                for g in range(N_KV):
                    o = jnp.where(lane_group == g, o_br[g * rows_g + r * tq:g * rows_g + (r + 1) * tq], o)
                term = gates[br][:, r * KV_LANES:(r + 1) * KV_LANES] * o
                acc = term if acc is None else acc + term
            o_ref[sq, :, r * KV_LANES:(r + 1) * KV_LANES] = acc.astype(o_ref.dtype)


def _attn_paged(q, gates, kc, vc, pool, page_table, slc_new, win_old, win_new, e_sel, ge, *, nbc, half, k_sel):
    n_seq, tq, nq = q.shape
    n_pages = page_table.shape[1]
    page = pool.shape[3]
    past = n_pages * page
    w_buf = win_old.shape[3]
    seqs = DECODE_SEQS
    assert page == LANE and half == LANE and w_buf % LANE == 0 and tq % SUBLANE == 0 and N_KV * tq <= LANE and n_seq % seqs == 0
    per_seq = lambda *shape: pl.BlockSpec((seqs,) + shape, lambda b, pt: (b,) + (0,) * len(shape))
    const = lambda shape: pl.BlockSpec(shape, lambda b, pt: (0,) * len(shape))
    return pl.pallas_call(
        functools.partial(_attn_paged_kernel, n_pages=n_pages, seqs=seqs, past=past, nbc=nbc, half=half, k_sel=k_sel),
        grid_spec=pltpu.PrefetchScalarGridSpec(
            num_scalar_prefetch=1, grid=(n_seq // seqs,),
            in_specs=_page_specs(n_pages, page, seqs) + [
                per_seq(tq, nq), per_seq(tq, LANE), per_seq(KV_LANES, 2 * half), per_seq(KV_LANES, 2 * half),
                per_seq(tq, ROW_LANES), per_seq(2, KV_LANES, w_buf), per_seq(tq, ROW_LANES),
                const(e_sel.shape), const(ge.shape)],
            out_specs=per_seq(tq, nq)),
        out_shape=jax.ShapeDtypeStruct((n_seq, tq, nq), BF16),
        compiler_params=_params("arbitrary"),
        name="attn_paged",
    )(page_table, *([pool] * (seqs * n_pages)), q, gates, kc, vc, slc_new, win_old, win_new, e_sel, ge)


def _round_up(x, m):
    return -(-x // m) * m


def _block_onehot(n_rows, n_keys, t_valid):
    key = jnp.arange(n_keys)[None, :]
    return ((key // L_SEL == jnp.arange(n_rows)[:, None]) & (key < t_valid)).astype(BF16)


def _keys_minor(rows5):
    b, t = rows5.shape[:2]
    return jnp.transpose(rows5, (0, 2, 3, 4, 1)).reshape(b, 2, KV_LANES, t)


def _rows_major(kt, row):
    b, t = kt.shape[0], kt.shape[3]
    return jnp.transpose(kt.reshape((b,) + row + (t,)), (0, 4, 1, 2, 3))


def kernel(x_prompt, x_sample, cache_cmp_kv, cache_slc_kv, cache_win_kv, state_conv, page_table, norm1, a_w_in, a_conv_w, a_w_out, kv_norm, w_kv, kc_norm, ks_norm, kw_norm, b_w_qg, b_q_norm, b_w_o, norm2, w_up, w_down):
    bp, tp, d = x_prompt.shape
    bs, ts, _ = x_sample.shape
    depth = norm1.shape[0]
    n_a = a_w_in.shape[0]
    n_pool, page = cache_cmp_kv.shape[:2]
    n_pages = page_table.shape[1]
    past = n_pages * page
    w_buf = cache_win_kv.shape[1]
    row = cache_cmp_kv.shape[2:]
    nq = N_HEADS * HEAD_DIM
    assert d == nq and row == (2, N_KV, HEAD_DIM) and past % L_CMP == 0 and page % L_CMP == 0 and ts < L_CMP

    hperm = jnp.array([(g * GROUP + r) * HEAD_DIM + dd for r in range(GROUP) for g in range(N_KV) for dd in range(HEAD_DIM)])
    lane = jnp.arange(KV_LANES)
    ones_bd = (lane[:, None] // HEAD_DIM == lane[None, :] // HEAD_DIM).astype(BF16)
    tile4 = lambda v: jnp.tile(v.astype(F32), N_KV)[None, :]
    head_of_lane = (jnp.arange(nq) // HEAD_DIM % N_KV) * GROUP + jnp.arange(nq) // KV_LANES
    gcol = jnp.arange(LANE)
    ge = jnp.stack([(gcol[:, None] == head_of_lane[None, :] * N_BRANCH + br) for br in range(N_BRANCH)]).astype(BF16)
    w_in_b = a_w_in.astype(BF16)
    w_out_b = a_w_out.astype(BF16)
    w_up_b = w_up.astype(BF16)
    w_down_b = w_down.astype(BF16)
    w_kv_b = w_kv.astype(BF16)
    w_q_nat = b_w_qg[:, :, :nq].astype(BF16)
    w_q_b = w_q_nat[:, :, hperm]
    w_g_b = jnp.pad(b_w_qg[:, :, nq:], ((0, 0), (0, 0), (0, LANE - N_HEADS * N_BRANCH))).astype(BF16)
    w_g_grp = b_w_qg[:, :, nq:].reshape(b_w_qg.shape[0], d, N_KV, GROUP * N_BRANCH)
    w_g_grp = jnp.pad(w_g_grp, ((0, 0), (0, 0), (0, 0), (0, LANE - GROUP * N_BRANCH))).reshape(-1, d, N_KV * LANE).astype(BF16)
    w_o_nat = b_w_o.astype(BF16)
    w_o_b = w_o_nat[:, hperm, :]
    src = jnp.arange(KV_LANES)[None, :, None]
    dst = jnp.arange(KV_LANES)[None, None, :]
    pk = ((src == jnp.arange(N_KV)[:, None, None] * HEAD_DIM + dst) & (dst < HEAD_DIM)).astype(BF16)
    pw = pk[:, :, 0:LANE]
    g1 = norm1.astype(F32)[:, None, :]
    g2 = norm2.astype(F32)[:, None, :]
    kvn = kv_norm.astype(F32)[None, :]

    xp = x_prompt.reshape(bp * tp, d)
    xs = x_sample.reshape(bs * ts, d)
    p_conv, s_conv = [], []
    for l in range(n_a):
        xp, st = _conv_layer(xp, g1[l], w_in_b[l], a_conv_w[l], w_out_b[l], seq_len=tp)
        p_conv.append(st)
        xs, st = _conv_layer(xs, g1[l], w_in_b[l], a_conv_w[l], w_out_b[l], seq_len=ts, state=state_conv[l])
        s_conv.append(st)
        xp = _mlp_layer(xp, g2[l], w_up_b[l], w_down_b[l])
        xs = _mlp_layer(xs, g2[l], w_up_b[l], w_down_b[l])

    ks_g, kw_g, kc_g = tile4(ks_norm), tile4(kw_norm), tile4(kc_norm)
    (p_cmp_t, p_slc_t, p_win_t, kc_p, vc_p, p_sk, p_svt, p_wk, p_wvt) = _kv_rows_long(
        xp, kvn, w_kv_b, ones_bd, ks_g, kw_g, kc_g, pk, pw, seq_len=tp)
    s_cmp, s_slc, s_win = _kv_rows(xs, kvn, w_kv_b, ones_bd, ks_g, kw_g)

    nbc_p = tp // L_CMP
    nbs_p = -(-tp // L_SEL)
    half_p = _round_up(nbs_p, LANE)
    per_group = lambda a: _even_odd(a.reshape(bp, nbc_p, KV_LANES), half_p).reshape(bp, 2 * half_p, N_KV, HEAD_DIM)
    kc_p = jnp.pad(per_group(kc_p).transpose(0, 2, 1, 3), ((0, 0), (0, 0), (0, 0), (0, LANE - HEAD_DIM)))
    vct_p = per_group(vc_p).transpose(0, 2, 3, 1)

    t_all = past + ts
    nbc_s = t_all // L_CMP
    assert nbc_s * L_CMP == past
    nbs_s = -(-t_all // L_SEL)
    half_s = _round_up(nbs_s, LANE)
    pool_cmp = _keys_minor(cache_cmp_kv)
    pool_slc = _keys_minor(cache_slc_kv)
    win_old = _keys_minor(cache_win_kv)
    blk_of_key = jnp.arange(past) // L_CMP
    col_of_key = blk_of_key // 2 + half_s * (blk_of_key % 2)
    pool_mat = (col_of_key[:, None] == jnp.arange(2 * half_s)[None, :]).astype(F32) * (1.0 / L_CMP)
    pool_mat = pool_mat.reshape(n_pages, page, 2 * half_s).astype(BF16)
    kc_s, vc_s = _compress_pages(pool_cmp, page_table, pool_mat, jnp.tile(kc_norm.astype(F32), N_KV)[:, None])
    e_s = _block_onehot(half_s, past + LANE, t_all)

    for j in range(depth - n_a):
        l = n_a + j
        qg = tile4(b_q_norm[j])
        q, gates = _qg_proj(xp, g1[l], w_q_nat[j], w_g_grp[j], ones_bd, qg)
        o = _attn_seq(q.reshape(bp, tp, nq), gates.reshape(bp, tp, N_KV * LANE), kc_p, vct_p, p_sk, p_svt, p_wk, p_wvt,
                      nbc=nbc_p, half=half_p, k_sel=min(N_SEL, nbs_p))
        xp = _mlp_layer(xp, g2[l], w_up_b[l], w_down_b[l], attn=o.reshape(bp * tp, nq), w_o=w_o_nat[j])

        q, gates = _qg_proj(xs, g1[l], w_q_b[j], w_g_b[j], ones_bd, qg)
        o = _attn_paged(q.reshape(bs, ts, nq), gates.reshape(bs, ts, LANE), kc_s, vc_s, pool_slc, page_table,
                        s_slc.reshape(bs, ts, ROW_LANES), win_old, s_win.reshape(bs, ts, ROW_LANES), e_s, ge,
                        nbc=nbc_s, half=half_s, k_sel=min(N_SEL, nbs_s))
        xs = _mlp_layer(xs, g2[l], w_up_b[l], w_down_b[l], attn=o.reshape(bs * ts, nq), w_o=w_o_b[j])

    rows5 = lambda a, b_, t_: a.reshape((b_, t_) + row)
    s_win_all = jnp.concatenate([cache_win_kv, rows5(s_win, bs, ts)], axis=1)
    return (xp.reshape(bp, tp, d), xs.reshape(bs, ts, d),
            _rows_major(p_cmp_t, row), _rows_major(p_slc_t, row), _rows_major(p_win_t[..., tp - min(WINDOW, tp):], row),
            jnp.stack(p_conv),
            rows5(s_cmp, bs, ts), rows5(s_slc, bs, ts), s_win_all[:, -w_buf:], jnp.stack(s_conv))
```

```python
import functools
import math

import jax
import jax.numpy as jnp
from jax import lax
from jax.experimental import pallas as pl
from jax.experimental.pallas import tpu as pltpu

F32 = jnp.float32
BF16 = jnp.bfloat16

N_HEADS = 16
N_KV = 4
HEAD_DIM = 64
GROUP = N_HEADS // N_KV
N_BRANCH = 3
L_CMP = 32
L_SEL = 64
N_SEL = 8
WINDOW = 512
CONV_W = 3
EPS = 1e-6
NEG = -1e30
FORCE = 1e4
LOG2E = math.log2(math.e)

KV_LANES = N_KV * HEAD_DIM
ROW_LANES = 2 * KV_LANES
LANE = 128
SUBLANE = 8
V_ROWS = HEAD_DIM + 2 * SUBLANE
ROW_TILE = 512
FF_CHUNK = 1024
SEL_TILE = 512
SWEEP_PARTS = 1
ATTN_CHUNK = 256
DECODE_SEQS = 2
V7X_VMEM_BYTES = 64 * 1024 * 1024
VMEM_LIMIT = V7X_VMEM_BYTES * 7 // 8


def _params(*sem):
    return pltpu.CompilerParams(dimension_semantics=sem, vmem_limit_bytes=VMEM_LIMIT)


def _dot(a, b):
    return jnp.dot(a, b, preferred_element_type=F32)


def _dot_nt(a, b):
    return lax.dot_general(a, b, (((1,), (1,)), ((), ())), preferred_element_type=F32)


def _rms(x, g):
    return x * lax.rsqrt(jnp.mean(x * x, axis=-1, keepdims=True) + EPS) * g


def _split_dot(x, m):
    hi = x.astype(BF16)
    lo = (x - hi.astype(F32)).astype(BF16)
    return _dot(hi, m) + _dot(lo, m)


def _head_rms(k, ones_bd, gain):
    ss = _split_dot(k * k, ones_bd)
    return k * lax.rsqrt(ss * (1.0 / HEAD_DIM) + EPS) * gain


def _softmax2(s, mask, axis):
    s = jnp.where(mask, s, NEG)
    e = jnp.exp2(s - jnp.max(s, axis=axis, keepdims=True))
    return jnp.where(mask, e, 0.0) * (1.0 / jnp.sum(e, axis=axis, keepdims=True))


def _conv_core(x, g, win_ref, cw, wout_ref, s1_fix, s2_fix):
    d = x.shape[1]
    h = _rms(x, g).astype(BF16)
    bg = _dot(h, win_ref[:, 0:d])
    cg = _dot(h, win_ref[:, d:2 * d])
    xin = _dot(h, win_ref[:, 2 * d:3 * d])
    u = cg * xin
    s1 = s1_fix(pltpu.roll(u, 1, 0))
    s2 = s2_fix(pltpu.roll(u, 2, 0))
    z = cw[0:1] * s2 + cw[1:2] * s1 + cw[2:3] * u
    y = _dot((bg * z).astype(BF16), wout_ref[...])
    return x + y, u


def _conv_long_kernel(x_ref, g_ref, win_ref, cw_ref, wout_ref, o_ref, tail_ref, carry_ref, *, tiles_per_seq):
    tm = x_ref.shape[0]

    @pl.when(pl.program_id(0) % tiles_per_seq == 0)
    def _():
        carry_ref[...] = jnp.zeros_like(carry_ref)

    prev = carry_ref[...]
    row = lax.broadcasted_iota(jnp.int32, (tm, 1), 0)
    last, before = prev[SUBLANE - 1:SUBLANE], prev[SUBLANE - 2:SUBLANE - 1]
    s1_fix = lambda r: jnp.where(row == 0, last, r)
    s2_fix = lambda r: jnp.where(row == 0, before, jnp.where(row == 1, last, r))
    out, u = _conv_core(x_ref[...], g_ref[...], win_ref, cw_ref[...], wout_ref, s1_fix, s2_fix)
    o_ref[...] = out
    carry_ref[...] = u[tm - SUBLANE:tm]
    tail_ref[0] = u[tm - SUBLANE:tm]


def _conv_short_kernel(x_ref, g_ref, win_ref, cw_ref, wout_ref, pa_ref, pb_ref, o_ref, u_ref, *, seg):
    tm = x_ref.shape[0]
    pos = lax.broadcasted_iota(jnp.int32, (tm, 1), 0) % seg
    s1_fix = lambda r: jnp.where(pos < 1, pa_ref[...], r)
    s2_fix = lambda r: jnp.where(pos < 2, pb_ref[...], r)
    out, u = _conv_core(x_ref[...], g_ref[...], win_ref, cw_ref[...], wout_ref, s1_fix, s2_fix)
    o_ref[...] = out
    u_ref[...] = u


def _conv_layer(x, g, w_in, cw, w_out, *, seq_len, state=None):
    n, d = x.shape
    tm = min(ROW_TILE, n)
    assert n % tm == 0
    n_seq = n // seq_len
    row_spec = pl.BlockSpec((tm, d), lambda i: (i, 0))
    w_specs = [pl.BlockSpec((1, d), lambda i: (0, 0)),
               pl.BlockSpec((d, 3 * d), lambda i: (0, 0)),
               pl.BlockSpec((CONV_W, d), lambda i: (0, 0)),
               pl.BlockSpec((d, d), lambda i: (0, 0))]
    if state is None:
        assert seq_len % tm == 0 and seq_len >= CONV_W - 1
        tiles_per_seq = seq_len // tm
        out, tail = pl.pallas_call(
            functools.partial(_conv_long_kernel, tiles_per_seq=tiles_per_seq),
            grid=(n // tm,),
            in_specs=[row_spec] + w_specs,
            out_specs=[row_spec, pl.BlockSpec((1, SUBLANE, d), lambda i: (i, 0, 0))],
            out_shape=[jax.ShapeDtypeStruct((n, d), F32), jax.ShapeDtypeStruct((n // tm, SUBLANE, d), F32)],
            scratch_shapes=[pltpu.VMEM((SUBLANE, d), F32)],
            compiler_params=_params("arbitrary"),
            name="conv_long",
        )(x, g, w_in, cw, w_out)
        new_state = tail.reshape(n_seq, tiles_per_seq, SUBLANE, d)[:, -1, SUBLANE - (CONV_W - 1):]
        return out, new_state
    assert tm % seq_len == 0 and seq_len >= CONV_W - 1
    zeros = jnp.zeros((n_seq, seq_len, d), F32)
    pa = zeros.at[:, 0].set(state[:, 1]).reshape(n, d)
    pb = zeros.at[:, 0].set(state[:, 0]).at[:, 1].set(state[:, 1]).reshape(n, d)
    out, u = pl.pallas_call(
        functools.partial(_conv_short_kernel, seg=seq_len),
        grid=(n // tm,),
        in_specs=[row_spec] + w_specs + [row_spec, row_spec],
        out_specs=[row_spec, row_spec],
        out_shape=[jax.ShapeDtypeStruct((n, d), F32), jax.ShapeDtypeStruct((n, d), F32)],
        compiler_params=_params("arbitrary"),
        name="conv_short",
    )(x, g, w_in, cw, w_out, pa, pb)
    new_state = u.reshape(n_seq, seq_len, d)[:, seq_len - (CONV_W - 1):]
    return out, new_state


def _mlp_body(x, g_ref, wup_ref, wdn_ref, o_ref):
    h = _rms(x, g_ref[...]).astype(BF16)
    acc = x
    for c in range(wup_ref.shape[1] // FF_CHUNK):
        a = _dot(h, wup_ref[:, c * FF_CHUNK:(c + 1) * FF_CHUNK])
        a = jnp.square(jnp.maximum(a, 0.0)).astype(BF16)
        acc = acc + _dot(a, wdn_ref[c * FF_CHUNK:(c + 1) * FF_CHUNK, :])
    o_ref[...] = acc


def _mlp_kernel(x_ref, g_ref, wup_ref, wdn_ref, o_ref):
    _mlp_body(x_ref[...], g_ref, wup_ref, wdn_ref, o_ref)


def _proj_mlp_kernel(x_ref, a_ref, wo_ref, g_ref, wup_ref, wdn_ref, o_ref):
    _mlp_body(x_ref[...] + _dot(a_ref[...], wo_ref[...]), g_ref, wup_ref, wdn_ref, o_ref)


def _mlp_layer(x, g, w_up, w_down, attn=None, w_o=None):
    n, d = x.shape
    ff = w_up.shape[1]
    assert ff % FF_CHUNK == 0
    tm = min(ROW_TILE, n)
    assert n % tm == 0
    row_spec = pl.BlockSpec((tm, d), lambda i: (i, 0))
    w_specs = [pl.BlockSpec((1, d), lambda i: (0, 0)),
               pl.BlockSpec((d, ff), lambda i: (0, 0)),
               pl.BlockSpec((ff, d), lambda i: (0, 0))]
    if attn is None:
        kern, in_specs, args = _mlp_kernel, [row_spec] + w_specs, (x, g, w_up, w_down)
    else:
        kern = _proj_mlp_kernel
        in_specs = [row_spec, row_spec, pl.BlockSpec((d, d), lambda i: (0, 0))] + w_specs
        args = (x, attn, w_o, g, w_up, w_down)
    return pl.pallas_call(
        kern, grid=(n // tm,), in_specs=in_specs, out_specs=row_spec,
        out_shape=jax.ShapeDtypeStruct((n, d), F32),
        compiler_params=_params("arbitrary"),
        name="mlp" if attn is None else "proj_mlp",
    )(*args)


def _kv_kernel(x_ref, g_ref, w_ref, ones_ref, ks_ref, kw_ref, cmp_ref, slc_ref, win_ref):
    h = _rms(x_ref[...], g_ref[...]).astype(BF16)
    kv = _dot(h, w_ref[...])
    ones_bd = ones_ref[...]
    cmp_ref[...] = kv[:, 0:ROW_LANES]
    for br, (gain_ref, f_ref) in enumerate(((ks_ref, slc_ref), (kw_ref, win_ref)), start=1):
        base = br * ROW_LANES
        f_ref[:, 0:KV_LANES] = _head_rms(kv[:, base:base + KV_LANES], ones_bd, gain_ref[...])
        f_ref[:, KV_LANES:ROW_LANES] = kv[:, base + KV_LANES:base + ROW_LANES]


def _kv_rows(x, kv_norm, w_kv, ones_bd, ks_gain, kw_gain):
    n, d = x.shape
    tm = min(ROW_TILE, n)
    assert n % tm == 0
    row_spec = pl.BlockSpec((tm, d), lambda i: (i, 0))
    kv_spec = pl.BlockSpec((tm, ROW_LANES), lambda i: (i, 0))
    const = lambda shape: pl.BlockSpec(shape, lambda i: (0, 0))
    return pl.pallas_call(
        _kv_kernel, grid=(n // tm,),
        in_specs=[row_spec, const((1, d)), const(w_kv.shape), const((KV_LANES, KV_LANES)),
                  const((1, KV_LANES)), const((1, KV_LANES))],
        out_specs=[kv_spec] * 3,
        out_shape=[jax.ShapeDtypeStruct((n, ROW_LANES), F32)] * 3,
        compiler_params=_params("arbitrary"),
        name="kv_rows",
    )(x, kv_norm, w_kv, ones_bd, ks_gain, kw_gain)


def _kv_long_kernel(x_ref, g_ref, w_ref, ones_ref, ks_ref, kw_ref, kc_gain_ref, pk_ref, pw_ref,
                    cmp_ref, slc_ref, win_ref, kc_ref, vc_ref, sk_ref, svt_ref, wk_ref, wvt_ref, *, tiles_per_seq):
    tm = x_ref.shape[0]
    pos0 = (pl.program_id(0) % tiles_per_seq) * tm
    h = _rms(x_ref[...], g_ref[...]).astype(BF16)
    kv = _dot(h, w_ref[...])
    ones_bd = ones_ref[...]

    cmp_k, cmp_v = kv[:, 0:KV_LANES], kv[:, KV_LANES:ROW_LANES]
    cmp_ref[0, 0] = cmp_k.T
    cmp_ref[0, 1] = cmp_v.T
    kc_ref[...] = _head_rms(jnp.mean(cmp_k.reshape(tm // L_CMP, L_CMP, KV_LANES), axis=1), ones_bd, kc_gain_ref[...])
    vc_ref[...] = jnp.mean(cmp_v.reshape(tm // L_CMP, L_CMP, KV_LANES), axis=1)

    blk = (pos0 + lax.broadcasted_iota(jnp.int32, (tm, 1), 0)) // L_SEL
    onehot = lax.broadcasted_iota(jnp.int32, (1, KV_LANES), 1) - HEAD_DIM == blk
    ones_rows = jnp.ones((V_ROWS - HEAD_DIM, tm), F32)
    for br, (gain_ref, f_ref) in enumerate(((ks_ref, slc_ref), (kw_ref, win_ref)), start=1):
        base = br * ROW_LANES
        k = _head_rms(kv[:, base:base + KV_LANES], ones_bd, gain_ref[...])
        vt = kv[:, base + KV_LANES:base + ROW_LANES].T
        f_ref[0, 0] = k.T
        f_ref[0, 1] = vt
        kb = k.astype(BF16)
        for g in range(N_KV):
            vt_g = jnp.concatenate([vt[g * HEAD_DIM:(g + 1) * HEAD_DIM], ones_rows], axis=0).astype(BF16)
            if br == 1:
                sk_ref[0, g] = jnp.where(onehot, 1.0, _dot(kb, pk_ref[g])).astype(BF16)
                svt_ref[0, g, 0] = vt_g
            else:
                wk_ref[0, g] = _dot(kb, pw_ref[g]).astype(BF16)
                for j in range(tm // LANE):
                    wvt_ref[0, g, j] = vt_g[:, j * LANE:(j + 1) * LANE]


def _kv_rows_long(x, kv_norm, w_kv, ones_bd, ks_gain, kw_gain, kc_gain, pk, pw, *, seq_len):
    n, d = x.shape
    tm = SEL_TILE
    assert seq_len % tm == 0 and -(-seq_len // L_SEL) <= LANE and tm % (SUBLANE * L_CMP) == 0
    n_seq, tps = n // seq_len, seq_len // tm
    nb = tm // L_CMP
    row_spec = pl.BlockSpec((tm, d), lambda i: (i, 0))
    t_spec = pl.BlockSpec((1, 2, KV_LANES, tm), lambda i: (i // tps, 0, 0, i % tps))
    blk_spec = pl.BlockSpec((nb, KV_LANES), lambda i: (i, 0))
    const = lambda shape: pl.BlockSpec(shape, lambda i: (0,) * len(shape))
    grp = lambda *tail: pl.BlockSpec((1, N_KV) + tail, lambda i: (i // tps, 0, i % tps) + (0,) * (len(tail) - 1))
    sds = jax.ShapeDtypeStruct
    return pl.pallas_call(
        functools.partial(_kv_long_kernel, tiles_per_seq=tps), grid=(n // tm,),
        in_specs=[row_spec, const((1, d)), const(w_kv.shape), const((KV_LANES, KV_LANES)),
                  const((1, KV_LANES)), const((1, KV_LANES)), const((1, KV_LANES)), const(pk.shape), const(pw.shape)],
        out_specs=[t_spec] * 3 + [blk_spec] * 2 + [
            grp(tm, KV_LANES), grp(1, V_ROWS, tm), grp(tm, LANE), grp(tm // LANE, V_ROWS, LANE)],
        out_shape=[sds((n_seq, 2, KV_LANES, seq_len), F32)] * 3 + [sds((n // L_CMP, KV_LANES), F32)] * 2 + [
            sds((n_seq, N_KV, seq_len, KV_LANES), BF16), sds((n_seq, N_KV, tps, V_ROWS, tm), BF16),
            sds((n_seq, N_KV, seq_len, LANE), BF16), sds((n_seq, N_KV, seq_len // LANE, V_ROWS, LANE), BF16)],
        compiler_params=_params("arbitrary"),
        name="kv_rows_long",
    )(x, kv_norm, w_kv, ones_bd, ks_gain, kw_gain, kc_gain, pk, pw)


def _page_specs(n_pages, page, seqs=1):
    return [pl.BlockSpec((1, 2, KV_LANES, page), lambda b, pt, s=s, p=p: (pt[seqs * b + s, p], 0, 0, 0))
            for s in range(seqs) for p in range(n_pages)]


def _compress_pages_kernel(pt_ref, *refs, n_pages):
    page_refs = refs[:n_pages]
    pool_ref, gain_ref, kc_ref, vc_ref = refs[n_pages:]
    page = page_refs[0].shape[3]
    acc = None
    for p in range(n_pages):
        part = _dot(page_refs[p][0].reshape(ROW_LANES, page), pool_ref[p].astype(F32))
        acc = part if acc is None else acc + part
    n_col = acc.shape[1]
    k = acc[0:KV_LANES].reshape(N_KV, HEAD_DIM, n_col)
    inv = lax.rsqrt(jnp.mean(k * k, axis=1, keepdims=True) + EPS)
    kc_ref[0] = ((k * inv).reshape(KV_LANES, n_col) * gain_ref[...]).astype(kc_ref.dtype)
    vc_ref[0] = acc[KV_LANES:ROW_LANES].astype(vc_ref.dtype)


def _compress_pages(pool, page_table, pool_mat, kc_gain_col):
    n_seq, n_pages = page_table.shape
    page = pool.shape[3]
    n_col = pool_mat.shape[2]
    const = lambda shape: pl.BlockSpec(shape, lambda b, pt: (0,) * len(shape))
    out_spec = pl.BlockSpec((1, KV_LANES, n_col), lambda b, pt: (b, 0, 0))
    return pl.pallas_call(
        functools.partial(_compress_pages_kernel, n_pages=n_pages),
        grid_spec=pltpu.PrefetchScalarGridSpec(
            num_scalar_prefetch=1, grid=(n_seq,),
            in_specs=_page_specs(n_pages, page) + [const(pool_mat.shape), const((KV_LANES, 1))],
            out_specs=[out_spec, out_spec]),
        out_shape=[jax.ShapeDtypeStruct((n_seq, KV_LANES, n_col), BF16)] * 2,
        compiler_params=_params("arbitrary"),
        name="compress_pages",
    )(page_table, *([pool] * n_pages), pool_mat, kc_gain_col)


def _even_odd(blocks, half):
    b, nbc, w = blocks.shape
    out = jnp.zeros((b, 2 * half, w), BF16)
    out = out.at[:, 0:(nbc + 1) // 2].set(blocks[:, 0::2].astype(BF16))
    return out.at[:, half:half + nbc // 2].set(blocks[:, 1::2].astype(BF16))


def _qg_kernel(x_ref, g_ref, wq_ref, wg_ref, ones_ref, qgain_ref, q_ref, gate_ref):
    h = _rms(x_ref[...], g_ref[...]).astype(BF16)
    q = _dot(h, wq_ref[...])
    ones_bd = ones_ref[...]
    gain = qgain_ref[...] * (HEAD_DIM ** -0.5 * LOG2E)
    for s in range(q.shape[1] // KV_LANES):
        sl = slice(s * KV_LANES, (s + 1) * KV_LANES)
        q_ref[:, sl] = _head_rms(q[:, sl], ones_bd, gain).astype(BF16)
    gate_ref[...] = jax.nn.sigmoid(_dot(h, wg_ref[...]))


def _qg_proj(x, g, w_q, w_g, ones_bd, q_gain):
    n, d = x.shape
    tm = min(ROW_TILE, n)
    assert n % tm == 0
    nq = w_q.shape[1]
    const = lambda shape: pl.BlockSpec(shape, lambda i: (0, 0))
    return pl.pallas_call(
        _qg_kernel, grid=(n // tm,),
        in_specs=[pl.BlockSpec((tm, d), lambda i: (i, 0)), const((1, d)), const(w_q.shape), const(w_g.shape),
                  const((KV_LANES, KV_LANES)), const((1, KV_LANES))],
        out_specs=[pl.BlockSpec((tm, nq), lambda i: (i, 0)), pl.BlockSpec((tm, w_g.shape[1]), lambda i: (i, 0))],
        out_shape=[jax.ShapeDtypeStruct((n, nq), BF16), jax.ShapeDtypeStruct((n, w_g.shape[1]), F32)],
        compiler_params=_params("arbitrary"),
        name="qg_proj",
    )(x, g, w_q, w_g, ones_bd, q_gain)


def _select_blocks(imp, t, k_sel, axis):
    blk = lax.broadcasted_iota(jnp.int32, imp.shape, axis)
    cur = t // L_SEL
    forced = (blk == 0) | (blk == cur) | (blk == cur - 1)
    n_forced = 3
    if k_sel < n_forced:
        work = jnp.where(blk > cur, NEG, jnp.where(forced, FORCE, imp))
        sel = jnp.zeros(imp.shape, jnp.bool_)
        rounds = k_sel
    else:
        work = jnp.where(forced | (blk > cur), NEG, imp)
        sel = forced & (blk <= cur)
        rounds = k_sel - n_forced
    blk_f = blk.astype(F32)
    for _ in range(rounds):
        m = jnp.max(work, axis=axis, keepdims=True)
        first = jnp.min(jnp.where(work == m, blk_f, float(imp.shape[axis])), axis=axis, keepdims=True)
        pick = blk_f == first
        sel = sel | (pick & (m > 0.5 * NEG))
        work = jnp.where(pick, -jnp.inf, work)
    return sel


def _attn_seq_kernel(q_ref, gate_ref, kc_ref, vct_ref, sk_ref, svt_ref, wk_ref, wvt_ref, wbias_ref, o_ref,
                     *scratch, nbc, half, k_sel):
    tq = q_ref.shape[1]
    c0 = pl.program_id(2) * tq
    cols = GROUP * tq
    t = c0 + lax.broadcasted_iota(jnp.int32, (1, cols), 1) % tq
    t1 = t[:, 0:tq]

    qt = q_ref[0].astype(F32).T
    q64 = jnp.concatenate([qt[r * HEAD_DIM:(r + 1) * HEAD_DIM] for r in range(GROUP)], axis=1)
    zeros64 = jnp.zeros((HEAD_DIM, cols), F32)
    q_pad = jnp.concatenate([q64, zeros64], axis=0).astype(BF16)

    row = lax.broadcasted_iota(jnp.int32, (2 * half, 1), 0)
    blk = 2 * (row % half) + row // half
    p = _softmax2(_dot(kc_ref[0, 0], q_pad), ((blk + 1) * L_CMP - 1 <= t) & (blk < nbc), 0)
    o_cmp = _dot(vct_ref[0, 0], p.astype(BF16))[0:HEAD_DIM]
    ph = p[:, 0:tq]
    for r in range(1, GROUP):
        ph = ph + p[:, r * tq:(r + 1) * tq]
    sel = _select_blocks(ph[0:half] + ph[half:2 * half], t1, k_sel, 0)
    bias = jnp.where(sel, 0.0, NEG)
    q_sel = jnp.concatenate([q64, jnp.concatenate([bias] * GROUP, axis=1), zeros64], axis=0).astype(BF16)

    band = tq + WINDOW
    wb = jnp.maximum(c0 - WINDOW, 0) // LANE
    w0 = pl.multiple_of(wb * LANE, LANE)
    s_win = _dot(wk_ref[0, 0, pl.ds(w0, band), :], q_pad) + wbias_ref[0]

    ns = SWEEP_PARTS
    m_refs, acc_refs = scratch[0:ns], scratch[ns:2 * ns]
    buf_x, buf_y = scratch[2 * ns:3 * ns], scratch[3 * ns:4 * ns]
    hk = SEL_TILE // ns
    for m_ref, acc_ref in zip(m_refs, acc_refs):
        m_ref[...] = jnp.full(m_ref.shape, NEG, F32)
        acc_ref[...] = jnp.zeros(acc_ref.shape, F32)

    def scores(kt, h):
        k0 = pl.multiple_of(kt * SEL_TILE + h * hk, hk)
        return _dot(sk_ref[0, 0, pl.ds(k0, hk), :], q_sel)

    def update(s, kt, h, causal):
        m_ref, acc_ref = m_refs[h], acc_refs[h]
        if causal:
            s = jnp.where(kt * SEL_TILE + h * hk + lax.broadcasted_iota(jnp.int32, (hk, 1), 0) <= t, s, NEG)
        m_old = m_ref[...]
        m_new = jnp.maximum(m_old, jnp.max(s, axis=0, keepdims=True))
        p = jnp.exp2(s - m_new).astype(BF16)
        vt = svt_ref[0, 0, kt][:, h * hk:(h + 1) * hk]
        acc_ref[...] = jnp.exp2(m_old - m_new) * acc_ref[...] + _dot(vt, p)
        m_ref[...] = m_new

    def fill(kt, buf):
        for h in range(ns):
            buf[h][...] = scores(kt, h)

    def drain(kt, buf, causal):
        for h in range(ns):
            update(buf[h][...], kt, h, causal)

    last = (c0 + tq - 1) // SEL_TILE
    fill(0, buf_x)

    p = jnp.exp2(s_win - jnp.max(s_win, axis=0, keepdims=True)).astype(BF16)
    wvt = jnp.concatenate([wvt_ref[0, 0, wb + j] for j in range(band // LANE)], axis=1)
    o_win = _dot(wvt, p)
    o_win = o_win[0:HEAD_DIM] * (1.0 / o_win[HEAD_DIM:HEAD_DIM + 1])

    def pair(kt):
        fill(kt + 1, buf_y)
        drain(kt, buf_x, False)
        fill(kt + 2, buf_x)
        drain(kt + 1, buf_y, False)

    def quad(i, carry):
        pair(4 * i)
        pair(4 * i + 2)
        return carry

    def one_pair(i, carry):
        pair(2 * i)
        return carry

    lax.fori_loop(0, last // 4, quad, 0)
    lax.fori_loop(2 * (last // 4), last // 2, one_pair, 0)

    @pl.when(last % 2 == 0)
    def _():
        drain(last, buf_x, True)

    @pl.when(last % 2 == 1)
    def _():
        fill(last, buf_y)
        drain(last - 1, buf_x, False)
        drain(last, buf_y, True)

    m = functools.reduce(jnp.maximum, [m_ref[...] for m_ref in m_refs])
    acc = sum(jnp.exp2(m_ref[...] - m) * acc_ref[...] for m_ref, acc_ref in zip(m_refs, acc_refs))
    o_sel = acc[0:HEAD_DIM] * (1.0 / acc[HEAD_DIM:HEAD_DIM + 1])

    gt = gate_ref[0].T
    gate = lambda br: jnp.concatenate([gt[r * N_BRANCH + br:r * N_BRANCH + br + 1] for r in range(GROUP)], axis=1)
    o = gate(0) * o_cmp + gate(1) * o_sel + gate(2) * o_win
    o = jnp.concatenate([o[:, r * tq:(r + 1) * tq] for r in range(GROUP)], axis=0)
    o_ref[0] = o.T.astype(o_ref.dtype)


def _window_bias(tq):
    v = jnp.arange(WINDOW // tq + 1)[:, None, None]
    j = jnp.arange(tq + WINDOW)[None, :, None]
    i = (jnp.arange(GROUP * tq) % tq)[None, None, :]
    dist = (v * tq + i) - (jnp.maximum(v * tq - WINDOW, 0) + j)
    return jnp.where((dist >= 0) & (dist <= WINDOW), 0.0, NEG).astype(F32)


def _attn_seq(q, gates, kc, vct, sk, svt, wk, wvt, *, nbc, half, k_sel):
    b, t_len, nq = q.shape
    tq = ATTN_CHUNK
    assert tq % LANE == 0 and half == LANE and t_len % SEL_TILE == 0 and t_len >= tq + WINDOW and WINDOW % tq == 0
    cols = GROUP * tq
    band = tq + WINDOW
    n_var = WINDOW // tq
    grp = lambda *shape: pl.BlockSpec((1, 1) + shape, lambda i, g, c: (i, g) + (0,) * len(shape))
    chunk = lambda w: pl.BlockSpec((1, tq, w), lambda i, g, c: (i, c, g))
    wbias_spec = pl.BlockSpec((1, band, cols), lambda i, g, c: (jnp.minimum(c, n_var), 0, 0))
    stat = pltpu.VMEM((1, cols), F32)
    acc = pltpu.VMEM((V_ROWS, cols), F32)
    return pl.pallas_call(
        functools.partial(_attn_seq_kernel, nbc=nbc, half=half, k_sel=k_sel),
        grid=(b, N_KV, t_len // tq),
        in_specs=[chunk(KV_LANES), chunk(LANE), grp(2 * half, LANE), grp(HEAD_DIM, 2 * half),
                  grp(t_len, KV_LANES), grp(t_len // SEL_TILE, V_ROWS, SEL_TILE),
                  grp(t_len, LANE), grp(t_len // LANE, V_ROWS, LANE), wbias_spec],
        out_specs=chunk(KV_LANES),
        out_shape=jax.ShapeDtypeStruct((b, t_len, nq), BF16),
        scratch_shapes=[stat] * SWEEP_PARTS + [acc] * SWEEP_PARTS
        + [pltpu.VMEM((SEL_TILE // SWEEP_PARTS, cols), F32)] * (2 * SWEEP_PARTS),
        compiler_params=_params("arbitrary", "arbitrary", "arbitrary"),
        name="attn_seq",
    )(q, gates, kc, vct, sk, svt, wk, wvt, _window_bias(tq))


def _attn_paged_kernel(pt_ref, *refs, n_pages, seqs, past, nbc, half, k_sel):
    page_refs = refs[:seqs * n_pages]
    (q_ref, gate_ref, kc_ref, vc_ref, snew_ref, wold_ref, wnew_ref, e_ref, ge_ref, o_ref) = refs[seqs * n_pages:]
    tq = q_ref.shape[1]
    page = page_refs[0].shape[3]
    w_buf = wold_ref.shape[3]
    rows_g = GROUP * tq
    t1 = past + lax.broadcasted_iota(jnp.int32, (tq, 1), 0)
    t = jnp.concatenate([t1] * (N_KV * GROUP), axis=0)
    t_col = past + lax.broadcasted_iota(jnp.int32, (1, LANE), 1) % tq
    lane_group = lax.broadcasted_iota(jnp.int32, (1, KV_LANES), 1) // HEAD_DIM
    col = lax.broadcasted_iota(jnp.int32, (1, 2 * half), 1)
    blk = 2 * (col % half) + col // half
    cmp_mask = ((blk + 1) * L_CMP - 1 <= t) & (blk < nbc)

    def pad_rows(x):
        return jnp.concatenate([x, jnp.zeros((LANE - x.shape[0], x.shape[1]), x.dtype)], axis=0).astype(BF16)

    for sq in range(seqs):
        pages = page_refs[sq * n_pages:(sq + 1) * n_pages]
        q_slabs = [q_ref[sq, :, r * KV_LANES:(r + 1) * KV_LANES].astype(F32) for r in range(GROUP)]
        qe = jnp.concatenate([jnp.where(lane_group == g, s, 0.0) for g in range(N_KV) for s in q_slabs], axis=0).astype(BF16)

        p = _softmax2(_dot(qe, kc_ref[sq]), cmp_mask, 1)
        o_cmp = _dot_nt(p.astype(BF16), vc_ref[sq])
        imps = []
        for g in range(N_KV):
            ph = p[g * rows_g:g * rows_g + tq]
            for r in range(1, GROUP):
                ph = ph + p[g * rows_g + r * tq:g * rows_g + (r + 1) * tq]
            imps.append(ph[:, 0:half] + ph[:, half:2 * half])
        imp = jnp.concatenate(imps + [jnp.zeros((LANE - N_KV * tq, half), F32)], axis=0)
        bias = jnp.where(_select_blocks(imp.T, t_col, k_sel, 0), 0.0, NEG).T
        bias = jnp.concatenate([bias[g * tq:(g + 1) * tq] for g in range(N_KV) for _ in range(GROUP)], axis=0)

        snew = pad_rows(snew_ref[sq])
        s = jnp.concatenate([_dot(qe, pages[i][0, 0].astype(BF16)) for i in range(n_pages)]
                            + [_dot_nt(qe, snew[:, 0:KV_LANES])], axis=1) + _dot(bias.astype(BF16), e_ref[...])
        kpos = lax.broadcasted_iota(jnp.int32, (1, s.shape[1]), 1)
        p = _softmax2(s, kpos <= t, 1).astype(BF16)
        o_sel = _dot(p[:, n_pages * page:], snew[:, KV_LANES:ROW_LANES])
        for i in range(n_pages):
            o_sel = o_sel + _dot_nt(p[:, i * page:(i + 1) * page], pages[i][0, 1].astype(BF16))

        wnew = pad_rows(wnew_ref[sq])
        s = jnp.concatenate([_dot(qe, wold_ref[sq, 0].astype(BF16)), _dot_nt(qe, wnew[:, 0:KV_LANES])], axis=1)
        j = lax.broadcasted_iota(jnp.int32, (1, s.shape[1]), 1)
        dist = t - (past - w_buf + j)
        p = _softmax2(s, (dist >= 0) & (dist <= WINDOW) & (j < w_buf + tq), 1).astype(BF16)
        o_win = _dot_nt(p[:, 0:w_buf], wold_ref[sq, 1].astype(BF16)) + _dot(p[:, w_buf:], wnew[:, KV_LANES:ROW_LANES])

        gates = [_split_dot(gate_ref[sq], ge_ref[br]) for br in range(N_BRANCH)]
        for r in range(GROUP):
            acc = None
            for br, o_br in enumerate((o_cmp, o_sel, o_win)):
                o = jnp.zeros((tq, KV_LANES), F32)
                for g in range(N_KV):
                    o = jnp.where(lane_group == g, o_br[g * rows_g + r * tq:g * rows_g + (r + 1) * tq], o)
                term = gates[br][:, r * KV_LANES:(r + 1) * KV_LANES] * o
                acc = term if acc is None else acc + term
            o_ref[sq, :, r * KV_LANES:(r + 1) * KV_LANES] = acc.astype(o_ref.dtype)


def _attn_paged(q, gates, kc, vc, pool, page_table, slc_new, win_old, win_new, e_sel, ge, *, nbc, half, k_sel):
    n_seq, tq, nq = q.shape
    n_pages = page_table.shape[1]
    page = pool.shape[3]
    past = n_pages * page
    w_buf = win_old.shape[3]
    seqs = DECODE_SEQS
    assert page == LANE and half == LANE and w_buf % LANE == 0 and tq % SUBLANE == 0 and N_KV * tq <= LANE and n_seq % seqs == 0
    per_seq = lambda *shape: pl.BlockSpec((seqs,) + shape, lambda b, pt: (b,) + (0,) * len(shape))
    const = lambda shape: pl.BlockSpec(shape, lambda b, pt: (0,) * len(shape))
    return pl.pallas_call(
        functools.partial(_attn_paged_kernel, n_pages=n_pages, seqs=seqs, past=past, nbc=nbc, half=half, k_sel=k_sel),
        grid_spec=pltpu.PrefetchScalarGridSpec(
            num_scalar_prefetch=1, grid=(n_seq // seqs,),
            in_specs=_page_specs(n_pages, page, seqs) + [
                per_seq(tq, nq), per_seq(tq, LANE), per_seq(KV_LANES, 2 * half), per_seq(KV_LANES, 2 * half),
                per_seq(tq, ROW_LANES), per_seq(2, KV_LANES, w_buf), per_seq(tq, ROW_LANES),
                const(e_sel.shape), const(ge.shape)],
            out_specs=per_seq(tq, nq)),
        out_shape=jax.ShapeDtypeStruct((n_seq, tq, nq), BF16),
        compiler_params=_params("arbitrary"),
        name="attn_paged",
    )(page_table, *([pool] * (seqs * n_pages)), q, gates, kc, vc, slc_new, win_old, win_new, e_sel, ge)


def _round_up(x, m):
    return -(-x // m) * m


def _block_onehot(n_rows, n_keys, t_valid):
    key = jnp.arange(n_keys)[None, :]
    return ((key // L_SEL == jnp.arange(n_rows)[:, None]) & (key < t_valid)).astype(BF16)


def _keys_minor(rows5):
    b, t = rows5.shape[:2]
    return jnp.transpose(rows5, (0, 2, 3, 4, 1)).reshape(b, 2, KV_LANES, t)


def _rows_major(kt, row):
    b, t = kt.shape[0], kt.shape[3]
    return jnp.transpose(kt.reshape((b,) + row + (t,)), (0, 4, 1, 2, 3))


def kernel(x_prompt, x_sample, cache_cmp_kv, cache_slc_kv, cache_win_kv, state_conv, page_table, norm1, a_w_in, a_conv_w, a_w_out, kv_norm, w_kv, kc_norm, ks_norm, kw_norm, b_w_qg, b_q_norm, b_w_o, norm2, w_up, w_down):
    bp, tp, d = x_prompt.shape
    bs, ts, _ = x_sample.shape
    depth = norm1.shape[0]
    n_a = a_w_in.shape[0]
    n_pool, page = cache_cmp_kv.shape[:2]
    n_pages = page_table.shape[1]
    past = n_pages * page
    w_buf = cache_win_kv.shape[1]
    row = cache_cmp_kv.shape[2:]
    nq = N_HEADS * HEAD_DIM
    assert d == nq and row == (2, N_KV, HEAD_DIM) and past % L_CMP == 0 and page % L_CMP == 0 and ts < L_CMP

    hperm = jnp.array([(g * GROUP + r) * HEAD_DIM + dd for r in range(GROUP) for g in range(N_KV) for dd in range(HEAD_DIM)])
    lane = jnp.arange(KV_LANES)
    ones_bd = (lane[:, None] // HEAD_DIM == lane[None, :] // HEAD_DIM).astype(BF16)
    tile4 = lambda v: jnp.tile(v.astype(F32), N_KV)[None, :]
    head_of_lane = (jnp.arange(nq) // HEAD_DIM % N_KV) * GROUP + jnp.arange(nq) // KV_LANES
    gcol = jnp.arange(LANE)
    ge = jnp.stack([(gcol[:, None] == head_of_lane[None, :] * N_BRANCH + br) for br in range(N_BRANCH)]).astype(BF16)
    w_in_b = a_w_in.astype(BF16)
    w_out_b = a_w_out.astype(BF16)
    w_up_b = w_up.astype(BF16)
    w_down_b = w_down.astype(BF16)
    w_kv_b = w_kv.astype(BF16)
    w_q_nat = b_w_qg[:, :, :nq].astype(BF16)
    w_q_b = w_q_nat[:, :, hperm]
    w_g_b = jnp.pad(b_w_qg[:, :, nq:], ((0, 0), (0, 0), (0, LANE - N_HEADS * N_BRANCH))).astype(BF16)
    w_g_grp = b_w_qg[:, :, nq:].reshape(b_w_qg.shape[0], d, N_KV, GROUP * N_BRANCH)
    w_g_grp = jnp.pad(w_g_grp, ((0, 0), (0, 0), (0, 0), (0, LANE - GROUP * N_BRANCH))).reshape(-1, d, N_KV * LANE).astype(BF16)
    w_o_nat = b_w_o.astype(BF16)
    w_o_b = w_o_nat[:, hperm, :]
    src = jnp.arange(KV_LANES)[None, :, None]
    dst = jnp.arange(KV_LANES)[None, None, :]
    pk = ((src == jnp.arange(N_KV)[:, None, None] * HEAD_DIM + dst) & (dst < HEAD_DIM)).astype(BF16)
    pw = pk[:, :, 0:LANE]
    g1 = norm1.astype(F32)[:, None, :]
    g2 = norm2.astype(F32)[:, None, :]
    kvn = kv_norm.astype(F32)[None, :]

    xp = x_prompt.reshape(bp * tp, d)
    xs = x_sample.reshape(bs * ts, d)
    p_conv, s_conv = [], []
    for l in range(n_a):
        xp, st = _conv_layer(xp, g1[l], w_in_b[l], a_conv_w[l], w_out_b[l], seq_len=tp)
        p_conv.append(st)
        xs, st = _conv_layer(xs, g1[l], w_in_b[l], a_conv_w[l], w_out_b[l], seq_len=ts, state=state_conv[l])
        s_conv.append(st)
        xp = _mlp_layer(xp, g2[l], w_up_b[l], w_down_b[l])
        xs = _mlp_layer(xs, g2[l], w_up_b[l], w_down_b[l])

    ks_g, kw_g, kc_g = tile4(ks_norm), tile4(kw_norm), tile4(kc_norm)
    (p_cmp_t, p_slc_t, p_win_t, kc_p, vc_p, p_sk, p_svt, p_wk, p_wvt) = _kv_rows_long(
        xp, kvn, w_kv_b, ones_bd, ks_g, kw_g, kc_g, pk, pw, seq_len=tp)
    s_cmp, s_slc, s_win = _kv_rows(xs, kvn, w_kv_b, ones_bd, ks_g, kw_g)

    nbc_p = tp // L_CMP
    nbs_p = -(-tp // L_SEL)
    half_p = _round_up(nbs_p, LANE)
    per_group = lambda a: _even_odd(a.reshape(bp, nbc_p, KV_LANES), half_p).reshape(bp, 2 * half_p, N_KV, HEAD_DIM)
    kc_p = jnp.pad(per_group(kc_p).transpose(0, 2, 1, 3), ((0, 0), (0, 0), (0, 0), (0, LANE - HEAD_DIM)))
    vct_p = per_group(vc_p).transpose(0, 2, 3, 1)

    t_all = past + ts
    nbc_s = t_all // L_CMP
    assert nbc_s * L_CMP == past
    nbs_s = -(-t_all // L_SEL)
    half_s = _round_up(nbs_s, LANE)
    pool_cmp = _keys_minor(cache_cmp_kv)
    pool_slc = _keys_minor(cache_slc_kv)
    win_old = _keys_minor(cache_win_kv)
    blk_of_key = jnp.arange(past) // L_CMP
    col_of_key = blk_of_key // 2 + half_s * (blk_of_key % 2)
    pool_mat = (col_of_key[:, None] == jnp.arange(2 * half_s)[None, :]).astype(F32) * (1.0 / L_CMP)
    pool_mat = pool_mat.reshape(n_pages, page, 2 * half_s).astype(BF16)
    kc_s, vc_s = _compress_pages(pool_cmp, page_table, pool_mat, jnp.tile(kc_norm.astype(F32), N_KV)[:, None])
    e_s = _block_onehot(half_s, past + LANE, t_all)

    for j in range(depth - n_a):
        l = n_a + j
        qg = tile4(b_q_norm[j])
        q, gates = _qg_proj(xp, g1[l], w_q_nat[j], w_g_grp[j], ones_bd, qg)
        o = _attn_seq(q.reshape(bp, tp, nq), gates.reshape(bp, tp, N_KV * LANE), kc_p, vct_p, p_sk, p_svt, p_wk, p_wvt,
                      nbc=nbc_p, half=half_p, k_sel=min(N_SEL, nbs_p))
        xp = _mlp_layer(xp, g2[l], w_up_b[l], w_down_b[l], attn=o.reshape(bp * tp, nq), w_o=w_o_nat[j])

        q, gates = _qg_proj(xs, g1[l], w_q_b[j], w_g_b[j], ones_bd, qg)
        o = _attn_paged(q.reshape(bs, ts, nq), gates.reshape(bs, ts, LANE), kc_s, vc_s, pool_slc, page_table,
                        s_slc.reshape(bs, ts, ROW_LANES), win_old, s_win.reshape(bs, ts, ROW_LANES), e_s, ge,
                        nbc=nbc_s, half=half_s, k_sel=min(N_SEL, nbs_s))
        xs = _mlp_layer(xs, g2[l], w_up_b[l], w_down_b[l], attn=o.reshape(bs * ts, nq), w_o=w_o_b[j])

    rows5 = lambda a, b_, t_: a.reshape((b_, t_) + row)
    s_win_all = jnp.concatenate([cache_win_kv, rows5(s_win, bs, ts)], axis=1)
    return (xp.reshape(bp, tp, d), xs.reshape(bs, ts, d),
            _rows_major(p_cmp_t, row), _rows_major(p_slc_t, row), _rows_major(p_win_t[..., tp - min(WINDOW, tp):], row),
            jnp.stack(p_conv),
            rows5(s_cmp, bs, ts), rows5(s_slc, bs, ts), s_win_all[:, -w_buf:], jnp.stack(s_conv))
```

```python
import functools
import math

import jax
import jax.numpy as jnp
from jax import lax
from jax.experimental import pallas as pl
from jax.experimental.pallas import tpu as pltpu

F32 = jnp.float32
BF16 = jnp.bfloat16

N_HEADS = 16
N_KV = 4
HEAD_DIM = 64
GROUP = N_HEADS // N_KV
N_BRANCH = 3
L_CMP = 32
L_SEL = 64
N_SEL = 8
WINDOW = 512
CONV_W = 3
EPS = 1e-6
NEG = -1e30
FORCE = 1e4
LOG2E = math.log2(math.e)

KV_LANES = N_KV * HEAD_DIM
ROW_LANES = 2 * KV_LANES
LANE = 128
SUBLANE = 8
V_ROWS = HEAD_DIM + 2 * SUBLANE
ROW_TILE = 512
FF_CHUNK = 1024
SEL_TILE = 512
SWEEP_PARTS = 1
ATTN_CHUNK = 256
DECODE_SEQS = 2
V7X_VMEM_BYTES = 64 * 1024 * 1024
VMEM_LIMIT = V7X_VMEM_BYTES * 7 // 8


def _params(*sem):
    return pltpu.CompilerParams(dimension_semantics=sem, vmem_limit_bytes=VMEM_LIMIT)


def _dot(a, b):
    return jnp.dot(a, b, preferred_element_type=F32)


def _dot_nt(a, b):
    return lax.dot_general(a, b, (((1,), (1,)), ((), ())), preferred_element_type=F32)


def _rms(x, g):
    return x * lax.rsqrt(jnp.mean(x * x, axis=-1, keepdims=True) + EPS) * g


def _split_dot(x, m):
    hi = x.astype(BF16)
    lo = (x - hi.astype(F32)).astype(BF16)
    return _dot(hi, m) + _dot(lo, m)


def _head_rms(k, ones_bd, gain):
    ss = _split_dot(k * k, ones_bd)
    return k * lax.rsqrt(ss * (1.0 / HEAD_DIM) + EPS) * gain


def _softmax2(s, mask, axis):
    s = jnp.where(mask, s, NEG)
    e = jnp.exp2(s - jnp.max(s, axis=axis, keepdims=True))
    return jnp.where(mask, e, 0.0) * (1.0 / jnp.sum(e, axis=axis, keepdims=True))


def _conv_core(x, g, win_ref, cw, wout_ref, s1_fix, s2_fix):
    d = x.shape[1]
    h = _rms(x, g).astype(BF16)
    bg = _dot(h, win_ref[:, 0:d])
    cg = _dot(h, win_ref[:, d:2 * d])
    xin = _dot(h, win_ref[:, 2 * d:3 * d])
    u = cg * xin
    s1 = s1_fix(pltpu.roll(u, 1, 0))
    s2 = s2_fix(pltpu.roll(u, 2, 0))
    z = cw[0:1] * s2 + cw[1:2] * s1 + cw[2:3] * u
    y = _dot((bg * z).astype(BF16), wout_ref[...])
    return x + y, u


def _conv_long_kernel(x_ref, g_ref, win_ref, cw_ref, wout_ref, o_ref, tail_ref, carry_ref, *, tiles_per_seq):
    tm = x_ref.shape[0]

    @pl.when(pl.program_id(0) % tiles_per_seq == 0)
    def _():
        carry_ref[...] = jnp.zeros_like(carry_ref)

    prev = carry_ref[...]
    row = lax.broadcasted_iota(jnp.int32, (tm, 1), 0)
    last, before = prev[SUBLANE - 1:SUBLANE], prev[SUBLANE - 2:SUBLANE - 1]
    s1_fix = lambda r: jnp.where(row == 0, last, r)
    s2_fix = lambda r: jnp.where(row == 0, before, jnp.where(row == 1, last, r))
    out, u = _conv_core(x_ref[...], g_ref[...], win_ref, cw_ref[...], wout_ref, s1_fix, s2_fix)
    o_ref[...] = out
    carry_ref[...] = u[tm - SUBLANE:tm]
    tail_ref[0] = u[tm - SUBLANE:tm]


def _conv_short_kernel(x_ref, g_ref, win_ref, cw_ref, wout_ref, pa_ref, pb_ref, o_ref, u_ref, *, seg):
    tm = x_ref.shape[0]
    pos = lax.broadcasted_iota(jnp.int32, (tm, 1), 0) % seg
    s1_fix = lambda r: jnp.where(pos < 1, pa_ref[...], r)
    s2_fix = lambda r: jnp.where(pos < 2, pb_ref[...], r)
    out, u = _conv_core(x_ref[...], g_ref[...], win_ref, cw_ref[...], wout_ref, s1_fix, s2_fix)
    o_ref[...] = out
    u_ref[...] = u


def _conv_layer(x, g, w_in, cw, w_out, *, seq_len, state=None):
    n, d = x.shape
    tm = min(ROW_TILE, n)
    assert n % tm == 0
    n_seq = n // seq_len
    row_spec = pl.BlockSpec((tm, d), lambda i: (i, 0))
    w_specs = [pl.BlockSpec((1, d), lambda i: (0, 0)),
               pl.BlockSpec((d, 3 * d), lambda i: (0, 0)),
               pl.BlockSpec((CONV_W, d), lambda i: (0, 0)),
               pl.BlockSpec((d, d), lambda i: (0, 0))]
    if state is None:
        assert seq_len % tm == 0 and seq_len >= CONV_W - 1
        tiles_per_seq = seq_len // tm
        out, tail = pl.pallas_call(
            functools.partial(_conv_long_kernel, tiles_per_seq=tiles_per_seq),
            grid=(n // tm,),
            in_specs=[row_spec] + w_specs,
            out_specs=[row_spec, pl.BlockSpec((1, SUBLANE, d), lambda i: (i, 0, 0))],
            out_shape=[jax.ShapeDtypeStruct((n, d), F32), jax.ShapeDtypeStruct((n // tm, SUBLANE, d), F32)],
            scratch_shapes=[pltpu.VMEM((SUBLANE, d), F32)],
            compiler_params=_params("arbitrary"),
            name="conv_long",
        )(x, g, w_in, cw, w_out)
        new_state = tail.reshape(n_seq, tiles_per_seq, SUBLANE, d)[:, -1, SUBLANE - (CONV_W - 1):]
        return out, new_state
    assert tm % seq_len == 0 and seq_len >= CONV_W - 1
    zeros = jnp.zeros((n_seq, seq_len, d), F32)
    pa = zeros.at[:, 0].set(state[:, 1]).reshape(n, d)
    pb = zeros.at[:, 0].set(state[:, 0]).at[:, 1].set(state[:, 1]).reshape(n, d)
    out, u = pl.pallas_call(
        functools.partial(_conv_short_kernel, seg=seq_len),
        grid=(n // tm,),
        in_specs=[row_spec] + w_specs + [row_spec, row_spec],
        out_specs=[row_spec, row_spec],
        out_shape=[jax.ShapeDtypeStruct((n, d), F32), jax.ShapeDtypeStruct((n, d), F32)],
        compiler_params=_params("arbitrary"),
        name="conv_short",
    )(x, g, w_in, cw, w_out, pa, pb)
    new_state = u.reshape(n_seq, seq_len, d)[:, seq_len - (CONV_W - 1):]
    return out, new_state


def _mlp_body(x, g_ref, wup_ref, wdn_ref, o_ref):
    h = _rms(x, g_ref[...]).astype(BF16)
    acc = x
    for c in range(wup_ref.shape[1] // FF_CHUNK):
        a = _dot(h, wup_ref[:, c * FF_CHUNK:(c + 1) * FF_CHUNK])
        a = jnp.square(jnp.maximum(a, 0.0)).astype(BF16)
        acc = acc + _dot(a, wdn_ref[c * FF_CHUNK:(c + 1) * FF_CHUNK, :])
    o_ref[...] = acc


def _mlp_kernel(x_ref, g_ref, wup_ref, wdn_ref, o_ref):
    _mlp_body(x_ref[...], g_ref, wup_ref, wdn_ref, o_ref)


def _proj_mlp_kernel(x_ref, a_ref, wo_ref, g_ref, wup_ref, wdn_ref, o_ref):
    _mlp_body(x_ref[...] + _dot(a_ref[...], wo_ref[...]), g_ref, wup_ref, wdn_ref, o_ref)


def _mlp_layer(x, g, w_up, w_down, attn=None, w_o=None):
    n, d = x.shape
    ff = w_up.shape[1]
    assert ff % FF_CHUNK == 0
    tm = min(ROW_TILE, n)
    assert n % tm == 0
    row_spec = pl.BlockSpec((tm, d), lambda i: (i, 0))
    w_specs = [pl.BlockSpec((1, d), lambda i: (0, 0)),
               pl.BlockSpec((d, ff), lambda i: (0, 0)),
               pl.BlockSpec((ff, d), lambda i: (0, 0))]
    if attn is None:
        kern, in_specs, args = _mlp_kernel, [row_spec] + w_specs, (x, g, w_up, w_down)
    else:
        kern = _proj_mlp_kernel
        in_specs = [row_spec, row_spec, pl.BlockSpec((d, d), lambda i: (0, 0))] + w_specs
        args = (x, attn, w_o, g, w_up, w_down)
    return pl.pallas_call(
        kern, grid=(n // tm,), in_specs=in_specs, out_specs=row_spec,
        out_shape=jax.ShapeDtypeStruct((n, d), F32),
        compiler_params=_params("arbitrary"),
        name="mlp" if attn is None else "proj_mlp",
    )(*args)


def _kv_kernel(x_ref, g_ref, w_ref, ones_ref, ks_ref, kw_ref, cmp_ref, slc_ref, win_ref):
    h = _rms(x_ref[...], g_ref[...]).astype(BF16)
    kv = _dot(h, w_ref[...])
    ones_bd = ones_ref[...]
    cmp_ref[...] = kv[:, 0:ROW_LANES]
    for br, (gain_ref, f_ref) in enumerate(((ks_ref, slc_ref), (kw_ref, win_ref)), start=1):
        base = br * ROW_LANES
        f_ref[:, 0:KV_LANES] = _head_rms(kv[:, base:base + KV_LANES], ones_bd, gain_ref[...])
        f_ref[:, KV_LANES:ROW_LANES] = kv[:, base + KV_LANES:base + ROW_LANES]


def _kv_rows(x, kv_norm, w_kv, ones_bd, ks_gain, kw_gain):
    n, d = x.shape
    tm = min(ROW_TILE, n)
    assert n % tm == 0
    row_spec = pl.BlockSpec((tm, d), lambda i: (i, 0))
    kv_spec = pl.BlockSpec((tm, ROW_LANES), lambda i: (i, 0))
    const = lambda shape: pl.BlockSpec(shape, lambda i: (0, 0))
    return pl.pallas_call(
        _kv_kernel, grid=(n // tm,),
        in_specs=[row_spec, const((1, d)), const(w_kv.shape), const((KV_LANES, KV_LANES)),
                  const((1, KV_LANES)), const((1, KV_LANES))],
        out_specs=[kv_spec] * 3,
        out_shape=[jax.ShapeDtypeStruct((n, ROW_LANES), F32)] * 3,
        compiler_params=_params("arbitrary"),
        name="kv_rows",
    )(x, kv_norm, w_kv, ones_bd, ks_gain, kw_gain)


def _kv_long_kernel(x_ref, g_ref, w_ref, ones_ref, ks_ref, kw_ref, kc_gain_ref, pk_ref, pw_ref,
                    cmp_ref, slc_ref, win_ref, kc_ref, vc_ref, sk_ref, svt_ref, wk_ref, wvt_ref, *, tiles_per_seq):
    tm = x_ref.shape[0]
    pos0 = (pl.program_id(0) % tiles_per_seq) * tm
    h = _rms(x_ref[...], g_ref[...]).astype(BF16)
    kv = _dot(h, w_ref[...])
    ones_bd = ones_ref[...]

    cmp_k, cmp_v = kv[:, 0:KV_LANES], kv[:, KV_LANES:ROW_LANES]
    cmp_ref[0, 0] = cmp_k.T
    cmp_ref[0, 1] = cmp_v.T
    kc_ref[...] = _head_rms(jnp.mean(cmp_k.reshape(tm // L_CMP, L_CMP, KV_LANES), axis=1), ones_bd, kc_gain_ref[...])
    vc_ref[...] = jnp.mean(cmp_v.reshape(tm // L_CMP, L_CMP, KV_LANES), axis=1)

    blk = (pos0 + lax.broadcasted_iota(jnp.int32, (tm, 1), 0)) // L_SEL
    onehot = lax.broadcasted_iota(jnp.int32, (1, KV_LANES), 1) - HEAD_DIM == blk
    ones_rows = jnp.ones((V_ROWS - HEAD_DIM, tm), F32)
    for br, (gain_ref, f_ref) in enumerate(((ks_ref, slc_ref), (kw_ref, win_ref)), start=1):
        base = br * ROW_LANES
        k = _head_rms(kv[:, base:base + KV_LANES], ones_bd, gain_ref[...])
        vt = kv[:, base + KV_LANES:base + ROW_LANES].T
        f_ref[0, 0] = k.T
        f_ref[0, 1] = vt
        kb = k.astype(BF16)
        for g in range(N_KV):
            vt_g = jnp.concatenate([vt[g * HEAD_DIM:(g + 1) * HEAD_DIM], ones_rows], axis=0).astype(BF16)
            if br == 1:
                sk_ref[0, g] = jnp.where(onehot, 1.0, _dot(kb, pk_ref[g])).astype(BF16)
                svt_ref[0, g, 0] = vt_g
            else:
                wk_ref[0, g] = _dot(kb, pw_ref[g]).astype(BF16)
                for j in range(tm // LANE):
                    wvt_ref[0, g, j] = vt_g[:, j * LANE:(j + 1) * LANE]


def _kv_rows_long(x, kv_norm, w_kv, ones_bd, ks_gain, kw_gain, kc_gain, pk, pw, *, seq_len):
    n, d = x.shape
    tm = SEL_TILE
    assert seq_len % tm == 0 and -(-seq_len // L_SEL) <= LANE and tm % (SUBLANE * L_CMP) == 0
    n_seq, tps = n // seq_len, seq_len // tm
    nb = tm // L_CMP
    row_spec = pl.BlockSpec((tm, d), lambda i: (i, 0))
    t_spec = pl.BlockSpec((1, 2, KV_LANES, tm), lambda i: (i // tps, 0, 0, i % tps))
    blk_spec = pl.BlockSpec((nb, KV_LANES), lambda i: (i, 0))
    const = lambda shape: pl.BlockSpec(shape, lambda i: (0,) * len(shape))
    grp = lambda *tail: pl.BlockSpec((1, N_KV) + tail, lambda i: (i // tps, 0, i % tps) + (0,) * (len(tail) - 1))
    sds = jax.ShapeDtypeStruct
    return pl.pallas_call(
        functools.partial(_kv_long_kernel, tiles_per_seq=tps), grid=(n // tm,),
        in_specs=[row_spec, const((1, d)), const(w_kv.shape), const((KV_LANES, KV_LANES)),
                  const((1, KV_LANES)), const((1, KV_LANES)), const((1, KV_LANES)), const(pk.shape), const(pw.shape)],
        out_specs=[t_spec] * 3 + [blk_spec] * 2 + [
            grp(tm, KV_LANES), grp(1, V_ROWS, tm), grp(tm, LANE), grp(tm // LANE, V_ROWS, LANE)],
        out_shape=[sds((n_seq, 2, KV_LANES, seq_len), F32)] * 3 + [sds((n // L_CMP, KV_LANES), F32)] * 2 + [
            sds((n_seq, N_KV, seq_len, KV_LANES), BF16), sds((n_seq, N_KV, tps, V_ROWS, tm), BF16),
            sds((n_seq, N_KV, seq_len, LANE), BF16), sds((n_seq, N_KV, seq_len // LANE, V_ROWS, LANE), BF16)],
        compiler_params=_params("arbitrary"),
        name="kv_rows_long",
    )(x, kv_norm, w_kv, ones_bd, ks_gain, kw_gain, kc_gain, pk, pw)


def _page_specs(n_pages, page, seqs=1):
    return [pl.BlockSpec((1, 2, KV_LANES, page), lambda b, pt, s=s, p=p: (pt[seqs * b + s, p], 0, 0, 0))
            for s in range(seqs) for p in range(n_pages)]


def _compress_pages_kernel(pt_ref, *refs, n_pages):
    page_refs = refs[:n_pages]
    pool_ref, gain_ref, kc_ref, vc_ref = refs[n_pages:]
    page = page_refs[0].shape[3]
    rows = jnp.concatenate([page_refs[p][0].reshape(ROW_LANES, page) for p in range(n_pages)], axis=1)
    acc = _dot(rows, pool_ref[...].astype(F32))
    n_col = acc.shape[1]
    k = acc[0:KV_LANES].reshape(N_KV, HEAD_DIM, n_col)
    inv = lax.rsqrt(jnp.mean(k * k, axis=1, keepdims=True) + EPS)
    kc_ref[0] = ((k * inv).reshape(KV_LANES, n_col) * gain_ref[...]).astype(kc_ref.dtype)
    vc_ref[0] = acc[KV_LANES:ROW_LANES].astype(vc_ref.dtype)


def _compress_pages(pool, page_table, pool_mat, kc_gain_col):
    n_seq, n_pages = page_table.shape
    page = pool.shape[3]
    n_col = pool_mat.shape[1]
    const = lambda shape: pl.BlockSpec(shape, lambda b, pt: (0,) * len(shape))
    out_spec = pl.BlockSpec((1, KV_LANES, n_col), lambda b, pt: (b, 0, 0))
    return pl.pallas_call(
        functools.partial(_compress_pages_kernel, n_pages=n_pages),
        grid_spec=pltpu.PrefetchScalarGridSpec(
            num_scalar_prefetch=1, grid=(n_seq,),
            in_specs=_page_specs(n_pages, page) + [const(pool_mat.shape), const((KV_LANES, 1))],
            out_specs=[out_spec, out_spec]),
        out_shape=[jax.ShapeDtypeStruct((n_seq, KV_LANES, n_col), BF16)] * 2,
        compiler_params=_params("arbitrary"),
        name="compress_pages",
    )(page_table, *([pool] * n_pages), pool_mat, kc_gain_col)


def _even_odd(blocks, half):
    b, nbc, w = blocks.shape
    out = jnp.zeros((b, 2 * half, w), BF16)
    out = out.at[:, 0:(nbc + 1) // 2].set(blocks[:, 0::2].astype(BF16))
    return out.at[:, half:half + nbc // 2].set(blocks[:, 1::2].astype(BF16))


def _qg_kernel(x_ref, g_ref, wq_ref, wg_ref, ones_ref, qgain_ref, q_ref, gate_ref):
    h = _rms(x_ref[...], g_ref[...]).astype(BF16)
    q = _dot(h, wq_ref[...])
    ones_bd = ones_ref[...]
    gain = qgain_ref[...] * (HEAD_DIM ** -0.5 * LOG2E)
    for s in range(q.shape[1] // KV_LANES):
        sl = slice(s * KV_LANES, (s + 1) * KV_LANES)
        q_ref[:, sl] = _head_rms(q[:, sl], ones_bd, gain).astype(BF16)
    gate_ref[...] = jax.nn.sigmoid(_dot(h, wg_ref[...]))


def _qg_proj(x, g, w_q, w_g, ones_bd, q_gain):
    n, d = x.shape
    tm = min(ROW_TILE, n)
    assert n % tm == 0
    nq = w_q.shape[1]
    const = lambda shape: pl.BlockSpec(shape, lambda i: (0, 0))
    return pl.pallas_call(
        _qg_kernel, grid=(n // tm,),
        in_specs=[pl.BlockSpec((tm, d), lambda i: (i, 0)), const((1, d)), const(w_q.shape), const(w_g.shape),
                  const((KV_LANES, KV_LANES)), const((1, KV_LANES))],
        out_specs=[pl.BlockSpec((tm, nq), lambda i: (i, 0)), pl.BlockSpec((tm, w_g.shape[1]), lambda i: (i, 0))],
        out_shape=[jax.ShapeDtypeStruct((n, nq), BF16), jax.ShapeDtypeStruct((n, w_g.shape[1]), F32)],
        compiler_params=_params("arbitrary"),
        name="qg_proj",
    )(x, g, w_q, w_g, ones_bd, q_gain)


def _select_blocks(imp, t, k_sel, axis):
    blk = lax.broadcasted_iota(jnp.int32, imp.shape, axis)
    cur = t // L_SEL
    forced = (blk == 0) | (blk == cur) | (blk == cur - 1)
    n_forced = 3
    if k_sel < n_forced:
        work = jnp.where(blk > cur, NEG, jnp.where(forced, FORCE, imp))
        sel = jnp.zeros(imp.shape, jnp.bool_)
        rounds = k_sel
    else:
        work = jnp.where(forced | (blk > cur), NEG, imp)
        sel = forced & (blk <= cur)
        rounds = k_sel - n_forced
    blk_f = blk.astype(F32)
    for _ in range(rounds):
        m = jnp.max(work, axis=axis, keepdims=True)
        first = jnp.min(jnp.where(work == m, blk_f, float(imp.shape[axis])), axis=axis, keepdims=True)
        pick = blk_f == first
        sel = sel | (pick & (m > 0.5 * NEG))
        work = jnp.where(pick, -jnp.inf, work)
    return sel


def _attn_seq_kernel(q_ref, gate_ref, kc_ref, vct_ref, sk_ref, svt_ref, wk_ref, wvt_ref, wbias_ref, o_ref,
                     *scratch, nbc, half, k_sel):
    tq = q_ref.shape[1]
    c0 = pl.program_id(2) * tq
    cols = GROUP * tq
    t = c0 + lax.broadcasted_iota(jnp.int32, (1, cols), 1) % tq
    t1 = t[:, 0:tq]

    qt = q_ref[0].astype(F32).T
    q64 = jnp.concatenate([qt[r * HEAD_DIM:(r + 1) * HEAD_DIM] for r in range(GROUP)], axis=1)
    zeros64 = jnp.zeros((HEAD_DIM, cols), F32)
    q_pad = jnp.concatenate([q64, zeros64], axis=0).astype(BF16)

    row = lax.broadcasted_iota(jnp.int32, (2 * half, 1), 0)
    blk = 2 * (row % half) + row // half
    p = _softmax2(_dot(kc_ref[0, 0], q_pad), ((blk + 1) * L_CMP - 1 <= t) & (blk < nbc), 0)
    o_cmp = _dot(vct_ref[0, 0], p.astype(BF16))[0:HEAD_DIM]
    ph = p[:, 0:tq]
    for r in range(1, GROUP):
        ph = ph + p[:, r * tq:(r + 1) * tq]
    sel = _select_blocks(ph[0:half] + ph[half:2 * half], t1, k_sel, 0)
    bias = jnp.where(sel, 0.0, NEG)
    q_sel = jnp.concatenate([q64, jnp.concatenate([bias] * GROUP, axis=1), zeros64], axis=0).astype(BF16)

    band = tq + WINDOW
    wb = jnp.maximum(c0 - WINDOW, 0) // LANE
    w0 = pl.multiple_of(wb * LANE, LANE)
    s_win = _dot(wk_ref[0, 0, pl.ds(w0, band), :], q_pad) + wbias_ref[0]

    ns = SWEEP_PARTS
    m_refs, acc_refs = scratch[0:ns], scratch[ns:2 * ns]
    buf_x, buf_y = scratch[2 * ns:3 * ns], scratch[3 * ns:4 * ns]
    hk = SEL_TILE // ns
    for m_ref, acc_ref in zip(m_refs, acc_refs):
        m_ref[...] = jnp.full(m_ref.shape, NEG, F32)
        acc_ref[...] = jnp.zeros(acc_ref.shape, F32)

    def scores(kt, h):
        k0 = pl.multiple_of(kt * SEL_TILE + h * hk, hk)
        return _dot(sk_ref[0, 0, pl.ds(k0, hk), :], q_sel)

    def update(s, kt, h, causal):
        m_ref, acc_ref = m_refs[h], acc_refs[h]
        if causal:
            s = jnp.where(kt * SEL_TILE + h * hk + lax.broadcasted_iota(jnp.int32, (hk, 1), 0) <= t, s, NEG)
        m_old = m_ref[...]
        m_new = jnp.maximum(m_old, jnp.max(s, axis=0, keepdims=True))
        p = jnp.exp2(s - m_new).astype(BF16)
        vt = svt_ref[0, 0, kt][:, h * hk:(h + 1) * hk]
        acc_ref[...] = jnp.exp2(m_old - m_new) * acc_ref[...] + _dot(vt, p)
        m_ref[...] = m_new

    def fill(kt, buf):
        for h in range(ns):
            buf[h][...] = scores(kt, h)

    def drain(kt, buf, causal):
        for h in range(ns):
            update(buf[h][...], kt, h, causal)

    last = (c0 + tq - 1) // SEL_TILE
    fill(0, buf_x)

    p = jnp.exp2(s_win - jnp.max(s_win, axis=0, keepdims=True)).astype(BF16)
    wvt = jnp.concatenate([wvt_ref[0, 0, wb + j] for j in range(band // LANE)], axis=1)
    o_win = _dot(wvt, p)
    o_win = o_win[0:HEAD_DIM] * (1.0 / o_win[HEAD_DIM:HEAD_DIM + 1])

    def pair(kt):
        fill(kt + 1, buf_y)
        drain(kt, buf_x, False)
        fill(kt + 2, buf_x)
        drain(kt + 1, buf_y, False)

    def quad(i, carry):
        pair(4 * i)
        pair(4 * i + 2)
        return carry

    def one_pair(i, carry):
        pair(2 * i)
        return carry

    lax.fori_loop(0, last // 4, quad, 0)
    lax.fori_loop(2 * (last // 4), last // 2, one_pair, 0)

    @pl.when(last % 2 == 0)
    def _():
        drain(last, buf_x, True)

    @pl.when(last % 2 == 1)
    def _():
        fill(last, buf_y)
        drain(last - 1, buf_x, False)
        drain(last, buf_y, True)

    m = functools.reduce(jnp.maximum, [m_ref[...] for m_ref in m_refs])
    acc = sum(jnp.exp2(m_ref[...] - m) * acc_ref[...] for m_ref, acc_ref in zip(m_refs, acc_refs))
    o_sel = acc[0:HEAD_DIM] * (1.0 / acc[HEAD_DIM:HEAD_DIM + 1])

    gt = gate_ref[0].T
    gate = lambda br: jnp.concatenate([gt[r * N_BRANCH + br:r * N_BRANCH + br + 1] for r in range(GROUP)], axis=1)
    o = gate(0) * o_cmp + gate(1) * o_sel + gate(2) * o_win
    o = jnp.concatenate([o[:, r * tq:(r + 1) * tq] for r in range(GROUP)], axis=0)
    o_ref[0] = o.T.astype(o_ref.dtype)


def _window_bias(tq):
    v = jnp.arange(WINDOW // tq + 1)[:, None, None]
    j = jnp.arange(tq + WINDOW)[None, :, None]
    i = (jnp.arange(GROUP * tq) % tq)[None, None, :]
    dist = (v * tq + i) - (jnp.maximum(v * tq - WINDOW, 0) + j)
    return jnp.where((dist >= 0) & (dist <= WINDOW), 0.0, NEG).astype(F32)


def _attn_seq(q, gates, kc, vct, sk, svt, wk, wvt, *, nbc, half, k_sel):
    b, t_len, nq = q.shape
    tq = ATTN_CHUNK
    assert tq % LANE == 0 and half == LANE and t_len % SEL_TILE == 0 and t_len >= tq + WINDOW and WINDOW % tq == 0
    cols = GROUP * tq
    band = tq + WINDOW
    n_var = WINDOW // tq
    grp = lambda *shape: pl.BlockSpec((1, 1) + shape, lambda i, g, c: (i, g) + (0,) * len(shape))
    chunk = lambda w: pl.BlockSpec((1, tq, w), lambda i, g, c: (i, c, g))
    wbias_spec = pl.BlockSpec((1, band, cols), lambda i, g, c: (jnp.minimum(c, n_var), 0, 0))
    stat = pltpu.VMEM((1, cols), F32)
    acc = pltpu.VMEM((V_ROWS, cols), F32)
    return pl.pallas_call(
        functools.partial(_attn_seq_kernel, nbc=nbc, half=half, k_sel=k_sel),
        grid=(b, N_KV, t_len // tq),
        in_specs=[chunk(KV_LANES), chunk(LANE), grp(2 * half, LANE), grp(HEAD_DIM, 2 * half),
                  grp(t_len, KV_LANES), grp(t_len // SEL_TILE, V_ROWS, SEL_TILE),
                  grp(t_len, LANE), grp(t_len // LANE, V_ROWS, LANE), wbias_spec],
        out_specs=chunk(KV_LANES),
        out_shape=jax.ShapeDtypeStruct((b, t_len, nq), BF16),
        scratch_shapes=[stat] * SWEEP_PARTS + [acc] * SWEEP_PARTS
        + [pltpu.VMEM((SEL_TILE // SWEEP_PARTS, cols), F32)] * (2 * SWEEP_PARTS),
        compiler_params=_params("arbitrary", "arbitrary", "arbitrary"),
        name="attn_seq",
    )(q, gates, kc, vct, sk, svt, wk, wvt, _window_bias(tq))


def _attn_paged_kernel(pt_ref, *refs, n_pages, seqs, past, nbc, half, k_sel):
    page_refs = refs[:seqs * n_pages]
    (q_ref, gate_ref, kc_ref, vc_ref, snew_ref, wold_ref, wnew_ref, e_ref, ge_ref, o_ref) = refs[seqs * n_pages:]
    tq = q_ref.shape[1]
    page = page_refs[0].shape[3]
    w_buf = wold_ref.shape[3]
    rows_g = GROUP * tq
    t1 = past + lax.broadcasted_iota(jnp.int32, (tq, 1), 0)
    t = jnp.concatenate([t1] * (N_KV * GROUP), axis=0)
    t_col = past + lax.broadcasted_iota(jnp.int32, (1, LANE), 1) % tq
    lane_group = lax.broadcasted_iota(jnp.int32, (1, KV_LANES), 1) // HEAD_DIM
    col = lax.broadcasted_iota(jnp.int32, (1, 2 * half), 1)
    blk = 2 * (col % half) + col // half
    cmp_mask = ((blk + 1) * L_CMP - 1 <= t) & (blk < nbc)

    def pad_rows(x):
        return jnp.concatenate([x, jnp.zeros((LANE - x.shape[0], x.shape[1]), x.dtype)], axis=0).astype(BF16)

    for sq in range(seqs):
        pages = page_refs[sq * n_pages:(sq + 1) * n_pages]
        q_slabs = [q_ref[sq, :, r * KV_LANES:(r + 1) * KV_LANES].astype(F32) for r in range(GROUP)]
        qe = jnp.concatenate([jnp.where(lane_group == g, s, 0.0) for g in range(N_KV) for s in q_slabs], axis=0).astype(BF16)

        p = _softmax2(_dot(qe, kc_ref[sq]), cmp_mask, 1)
        o_cmp = _dot_nt(p.astype(BF16), vc_ref[sq])
        imps = []
        for g in range(N_KV):
            ph = p[g * rows_g:g * rows_g + tq]
            for r in range(1, GROUP):
                ph = ph + p[g * rows_g + r * tq:g * rows_g + (r + 1) * tq]
            imps.append(ph[:, 0:half] + ph[:, half:2 * half])
        imp = jnp.concatenate(imps + [jnp.zeros((LANE - N_KV * tq, half), F32)], axis=0)
        bias = jnp.where(_select_blocks(imp.T, t_col, k_sel, 0), 0.0, NEG).T
        bias = jnp.concatenate([bias[g * tq:(g + 1) * tq] for g in range(N_KV) for _ in range(GROUP)], axis=0)

        snew = pad_rows(snew_ref[sq])
        s = jnp.concatenate([_dot(qe, pages[i][0, 0].astype(BF16)) for i in range(n_pages)]
                            + [_dot_nt(qe, snew[:, 0:KV_LANES])], axis=1) + _dot(bias.astype(BF16), e_ref[...])
        kpos = lax.broadcasted_iota(jnp.int32, (1, s.shape[1]), 1)
        p = _softmax2(s, kpos <= t, 1).astype(BF16)
        vt_all = jnp.concatenate([pages[i][0, 1].astype(BF16) for i in range(n_pages)], axis=1)
        o_sel = _dot_nt(p[:, 0:n_pages * page], vt_all) + _dot(p[:, n_pages * page:], snew[:, KV_LANES:ROW_LANES])

        wnew = pad_rows(wnew_ref[sq])
        s = jnp.concatenate([_dot(qe, wold_ref[sq, 0].astype(BF16)), _dot_nt(qe, wnew[:, 0:KV_LANES])], axis=1)
        j = lax.broadcasted_iota(jnp.int32, (1, s.shape[1]), 1)
        dist = t - (past - w_buf + j)
        p = _softmax2(s, (dist >= 0) & (dist <= WINDOW) & (j < w_buf + tq), 1).astype(BF16)
        o_win = _dot_nt(p[:, 0:w_buf], wold_ref[sq, 1].astype(BF16)) + _dot(p[:, w_buf:], wnew[:, KV_LANES:ROW_LANES])

        gates = [_split_dot(gate_ref[sq], ge_ref[br]) for br in range(N_BRANCH)]
        for r in range(GROUP):
            acc = None
            for br, o_br in enumerate((o_cmp, o_sel, o_win)):
                o = jnp.zeros((tq, KV_LANES), F32)
                for g in range(N_KV):
                    o = jnp.where(lane_group == g, o_br[g * rows_g + r * tq:g * rows_g + (r + 1) * tq], o)
                term = gates[br][:, r * KV_LANES:(r + 1) * KV_LANES] * o
                acc = term if acc is None else acc + term
            o_ref[sq, :, r * KV_LANES:(r + 1) * KV_LANES] = acc.astype(o_ref.dtype)


def _attn_paged(q, gates, kc, vc, pool, page_table, slc_new, win_old, win_new, e_sel, ge, *, nbc, half, k_sel):
    n_seq, tq, nq = q.shape
    n_pages = page_table.shape[1]
    page = pool.shape[3]
    past = n_pages * page
    w_buf = win_old.shape[3]
    seqs = DECODE_SEQS
    assert page == LANE and half == LANE and w_buf % LANE == 0 and tq % SUBLANE == 0 and N_KV * tq <= LANE and n_seq % seqs == 0
    per_seq = lambda *shape: pl.BlockSpec((seqs,) + shape, lambda b, pt: (b,) + (0,) * len(shape))
    const = lambda shape: pl.BlockSpec(shape, lambda b, pt: (0,) * len(shape))
    return pl.pallas_call(
        functools.partial(_attn_paged_kernel, n_pages=n_pages, seqs=seqs, past=past, nbc=nbc, half=half, k_sel=k_sel),
        grid_spec=pltpu.PrefetchScalarGridSpec(
            num_scalar_prefetch=1, grid=(n_seq // seqs,),
            in_specs=_page_specs(n_pages, page, seqs) + [
                per_seq(tq, nq), per_seq(tq, LANE), per_seq(KV_LANES, 2 * half), per_seq(KV_LANES, 2 * half),
                per_seq(tq, ROW_LANES), per_seq(2, KV_LANES, w_buf), per_seq(tq, ROW_LANES),
                const(e_sel.shape), const(ge.shape)],
            out_specs=per_seq(tq, nq)),
        out_shape=jax.ShapeDtypeStruct((n_seq, tq, nq), BF16),
        compiler_params=_params("arbitrary"),
        name="attn_paged",
    )(page_table, *([pool] * (seqs * n_pages)), q, gates, kc, vc, slc_new, win_old, win_new, e_sel, ge)


def _round_up(x, m):
    return -(-x // m) * m


def _block_onehot(n_rows, n_keys, t_valid):
    key = jnp.arange(n_keys)[None, :]
    return ((key // L_SEL == jnp.arange(n_rows)[:, None]) & (key < t_valid)).astype(BF16)


def _keys_minor(rows5):
    b, t = rows5.shape[:2]
    return jnp.transpose(rows5, (0, 2, 3, 4, 1)).reshape(b, 2, KV_LANES, t)


def _rows_major(kt, row):
    b, t = kt.shape[0], kt.shape[3]
    return jnp.transpose(kt.reshape((b,) + row + (t,)), (0, 4, 1, 2, 3))


def kernel(x_prompt, x_sample, cache_cmp_kv, cache_slc_kv, cache_win_kv, state_conv, page_table, norm1, a_w_in, a_conv_w, a_w_out, kv_norm, w_kv, kc_norm, ks_norm, kw_norm, b_w_qg, b_q_norm, b_w_o, norm2, w_up, w_down):
    bp, tp, d = x_prompt.shape
    bs, ts, _ = x_sample.shape
    depth = norm1.shape[0]
    n_a = a_w_in.shape[0]
    n_pool, page = cache_cmp_kv.shape[:2]
    n_pages = page_table.shape[1]
    past = n_pages * page
    w_buf = cache_win_kv.shape[1]
    row = cache_cmp_kv.shape[2:]
    nq = N_HEADS * HEAD_DIM
    assert d == nq and row == (2, N_KV, HEAD_DIM) and past % L_CMP == 0 and page % L_CMP == 0 and ts < L_CMP

    hperm = jnp.array([(g * GROUP + r) * HEAD_DIM + dd for r in range(GROUP) for g in range(N_KV) for dd in range(HEAD_DIM)])
    lane = jnp.arange(KV_LANES)
    ones_bd = (lane[:, None] // HEAD_DIM == lane[None, :] // HEAD_DIM).astype(BF16)
    tile4 = lambda v: jnp.tile(v.astype(F32), N_KV)[None, :]
    head_of_lane = (jnp.arange(nq) // HEAD_DIM % N_KV) * GROUP + jnp.arange(nq) // KV_LANES
    gcol = jnp.arange(LANE)
    ge = jnp.stack([(gcol[:, None] == head_of_lane[None, :] * N_BRANCH + br) for br in range(N_BRANCH)]).astype(BF16)
    w_in_b = a_w_in.astype(BF16)
    w_out_b = a_w_out.astype(BF16)
    w_up_b = w_up.astype(BF16)
    w_down_b = w_down.astype(BF16)
    w_kv_b = w_kv.astype(BF16)
    w_q_nat = b_w_qg[:, :, :nq].astype(BF16)
    w_q_b = w_q_nat[:, :, hperm]
    w_g_b = jnp.pad(b_w_qg[:, :, nq:], ((0, 0), (0, 0), (0, LANE - N_HEADS * N_BRANCH))).astype(BF16)
    w_g_grp = b_w_qg[:, :, nq:].reshape(b_w_qg.shape[0], d, N_KV, GROUP * N_BRANCH)
    w_g_grp = jnp.pad(w_g_grp, ((0, 0), (0, 0), (0, 0), (0, LANE - GROUP * N_BRANCH))).reshape(-1, d, N_KV * LANE).astype(BF16)
    w_o_nat = b_w_o.astype(BF16)
    w_o_b = w_o_nat[:, hperm, :]
    src = jnp.arange(KV_LANES)[None, :, None]
    dst = jnp.arange(KV_LANES)[None, None, :]
    pk = ((src == jnp.arange(N_KV)[:, None, None] * HEAD_DIM + dst) & (dst < HEAD_DIM)).astype(BF16)
    pw = pk[:, :, 0:LANE]
    g1 = norm1.astype(F32)[:, None, :]
    g2 = norm2.astype(F32)[:, None, :]
    kvn = kv_norm.astype(F32)[None, :]

    xp = x_prompt.reshape(bp * tp, d)
    xs = x_sample.reshape(bs * ts, d)
    p_conv, s_conv = [], []
    for l in range(n_a):
        xp, st = _conv_layer(xp, g1[l], w_in_b[l], a_conv_w[l], w_out_b[l], seq_len=tp)
        p_conv.append(st)
        xs, st = _conv_layer(xs, g1[l], w_in_b[l], a_conv_w[l], w_out_b[l], seq_len=ts, state=state_conv[l])
        s_conv.append(st)
        xp = _mlp_layer(xp, g2[l], w_up_b[l], w_down_b[l])
        xs = _mlp_layer(xs, g2[l], w_up_b[l], w_down_b[l])

    ks_g, kw_g, kc_g = tile4(ks_norm), tile4(kw_norm), tile4(kc_norm)
    (p_cmp_t, p_slc_t, p_win_t, kc_p, vc_p, p_sk, p_svt, p_wk, p_wvt) = _kv_rows_long(
        xp, kvn, w_kv_b, ones_bd, ks_g, kw_g, kc_g, pk, pw, seq_len=tp)
    s_cmp, s_slc, s_win = _kv_rows(xs, kvn, w_kv_b, ones_bd, ks_g, kw_g)

    nbc_p = tp // L_CMP
    nbs_p = -(-tp // L_SEL)
    half_p = _round_up(nbs_p, LANE)
    per_group = lambda a: _even_odd(a.reshape(bp, nbc_p, KV_LANES), half_p).reshape(bp, 2 * half_p, N_KV, HEAD_DIM)
    kc_p = jnp.pad(per_group(kc_p).transpose(0, 2, 1, 3), ((0, 0), (0, 0), (0, 0), (0, LANE - HEAD_DIM)))
    vct_p = per_group(vc_p).transpose(0, 2, 3, 1)

    t_all = past + ts
    nbc_s = t_all // L_CMP
    assert nbc_s * L_CMP == past
    nbs_s = -(-t_all // L_SEL)
    half_s = _round_up(nbs_s, LANE)
    pool_cmp = _keys_minor(cache_cmp_kv)
    pool_slc = _keys_minor(cache_slc_kv)
    win_old = _keys_minor(cache_win_kv)
    blk_of_key = jnp.arange(past) // L_CMP
    col_of_key = blk_of_key // 2 + half_s * (blk_of_key % 2)
    pool_mat = (col_of_key[:, None] == jnp.arange(2 * half_s)[None, :]).astype(F32) * (1.0 / L_CMP)
    pool_mat = pool_mat.astype(BF16)
    kc_s, vc_s = _compress_pages(pool_cmp, page_table, pool_mat, jnp.tile(kc_norm.astype(F32), N_KV)[:, None])
    e_s = _block_onehot(half_s, past + LANE, t_all)

    for j in range(depth - n_a):
        l = n_a + j
        qg = tile4(b_q_norm[j])
        q, gates = _qg_proj(xp, g1[l], w_q_nat[j], w_g_grp[j], ones_bd, qg)
        o = _attn_seq(q.reshape(bp, tp, nq), gates.reshape(bp, tp, N_KV * LANE), kc_p, vct_p, p_sk, p_svt, p_wk, p_wvt,
                      nbc=nbc_p, half=half_p, k_sel=min(N_SEL, nbs_p))
        xp = _mlp_layer(xp, g2[l], w_up_b[l], w_down_b[l], attn=o.reshape(bp * tp, nq), w_o=w_o_nat[j])

        q, gates = _qg_proj(xs, g1[l], w_q_b[j], w_g_b[j], ones_bd, qg)
        o = _attn_paged(q.reshape(bs, ts, nq), gates.reshape(bs, ts, LANE), kc_s, vc_s, pool_slc, page_table,
                        s_slc.reshape(bs, ts, ROW_LANES), win_old, s_win.reshape(bs, ts, ROW_LANES), e_s, ge,
                        nbc=nbc_s, half=half_s, k_sel=min(N_SEL, nbs_s))
        xs = _mlp_layer(xs, g2[l], w_up_b[l], w_down_b[l], attn=o.reshape(bs * ts, nq), w_o=w_o_b[j])

    rows5 = lambda a, b_, t_: a.reshape((b_, t_) + row)
    s_win_all = jnp.concatenate([cache_win_kv, rows5(s_win, bs, ts)], axis=1)
    return (xp.reshape(bp, tp, d), xs.reshape(bs, ts, d),
            _rows_major(p_cmp_t, row), _rows_major(p_slc_t, row), _rows_major(p_win_t[..., tp - min(WINDOW, tp):], row),
            jnp.stack(p_conv),
            rows5(s_cmp, bs, ts), rows5(s_slc, bs, ts), s_win_all[:, -w_buf:], jnp.stack(s_conv))
```

```python
import functools
import math

import jax
import jax.numpy as jnp
from jax import lax
from jax.experimental import pallas as pl
from jax.experimental.pallas import tpu as pltpu

F32 = jnp.float32
BF16 = jnp.bfloat16

N_HEADS = 16
N_KV = 4
HEAD_DIM = 64
GROUP = N_HEADS // N_KV
N_BRANCH = 3
L_CMP = 32
L_SEL = 64
N_SEL = 8
WINDOW = 512
CONV_W = 3
EPS = 1e-6
NEG = -1e30
FORCE = 1e4
LOG2E = math.log2(math.e)

KV_LANES = N_KV * HEAD_DIM
ROW_LANES = 2 * KV_LANES
LANE = 128
SUBLANE = 8
V_ROWS = HEAD_DIM + 2 * SUBLANE
ROW_TILE = 512
FF_CHUNK = 1024
SEL_TILE = 512
SWEEP_PARTS = 1
ATTN_CHUNK = 256
DECODE_SEQS = 2
V7X_VMEM_BYTES = 64 * 1024 * 1024
VMEM_LIMIT = V7X_VMEM_BYTES * 7 // 8


def _params(*sem):
    return pltpu.CompilerParams(dimension_semantics=sem, vmem_limit_bytes=VMEM_LIMIT)


def _dot(a, b):
    return jnp.dot(a, b, preferred_element_type=F32)


def _dot_nt(a, b):
    return lax.dot_general(a, b, (((1,), (1,)), ((), ())), preferred_element_type=F32)


def _rms(x, g):
    return x * lax.rsqrt(jnp.mean(x * x, axis=-1, keepdims=True) + EPS) * g


def _split_dot(x, m):
    hi = x.astype(BF16)
    lo = (x - hi.astype(F32)).astype(BF16)
    return _dot(hi, m) + _dot(lo, m)


def _head_rms(k, ones_bd, gain):
    ss = _split_dot(k * k, ones_bd)
    return k * lax.rsqrt(ss * (1.0 / HEAD_DIM) + EPS) * gain


def _softmax2(s, mask, axis):
    s = jnp.where(mask, s, NEG)
    e = jnp.exp2(s - jnp.max(s, axis=axis, keepdims=True))
    return jnp.where(mask, e, 0.0) * (1.0 / jnp.sum(e, axis=axis, keepdims=True))


def _conv_core(x, g, win_ref, cw, wout_ref, s1_fix, s2_fix):
    d = x.shape[1]
    h = _rms(x, g).astype(BF16)
    bg = _dot(h, win_ref[:, 0:d])
    cg = _dot(h, win_ref[:, d:2 * d])
    xin = _dot(h, win_ref[:, 2 * d:3 * d])
    u = cg * xin
    s1 = s1_fix(pltpu.roll(u, 1, 0))
    s2 = s2_fix(pltpu.roll(u, 2, 0))
    z = cw[0:1] * s2 + cw[1:2] * s1 + cw[2:3] * u
    y = _dot((bg * z).astype(BF16), wout_ref[...])
    return x + y, u


def _conv_long_kernel(x_ref, g_ref, win_ref, cw_ref, wout_ref, o_ref, tail_ref, carry_ref, *, tiles_per_seq):
    tm = x_ref.shape[0]

    @pl.when(pl.program_id(0) % tiles_per_seq == 0)
    def _():
        carry_ref[...] = jnp.zeros_like(carry_ref)

    prev = carry_ref[...]
    row = lax.broadcasted_iota(jnp.int32, (tm, 1), 0)
    last, before = prev[SUBLANE - 1:SUBLANE], prev[SUBLANE - 2:SUBLANE - 1]
    s1_fix = lambda r: jnp.where(row == 0, last, r)
    s2_fix = lambda r: jnp.where(row == 0, before, jnp.where(row == 1, last, r))
    out, u = _conv_core(x_ref[...], g_ref[...], win_ref, cw_ref[...], wout_ref, s1_fix, s2_fix)
    o_ref[...] = out
    carry_ref[...] = u[tm - SUBLANE:tm]
    tail_ref[0] = u[tm - SUBLANE:tm]


def _conv_short_kernel(x_ref, g_ref, win_ref, cw_ref, wout_ref, pa_ref, pb_ref, o_ref, u_ref, *, seg):
    tm = x_ref.shape[0]
    pos = lax.broadcasted_iota(jnp.int32, (tm, 1), 0) % seg
    s1_fix = lambda r: jnp.where(pos < 1, pa_ref[...], r)
    s2_fix = lambda r: jnp.where(pos < 2, pb_ref[...], r)
    out, u = _conv_core(x_ref[...], g_ref[...], win_ref, cw_ref[...], wout_ref, s1_fix, s2_fix)
    o_ref[...] = out
    u_ref[...] = u


def _conv_layer(x, g, w_in, cw, w_out, *, seq_len, state=None):
    n, d = x.shape
    tm = min(ROW_TILE, n)
    assert n % tm == 0
    n_seq = n // seq_len
    row_spec = pl.BlockSpec((tm, d), lambda i: (i, 0))
    w_specs = [pl.BlockSpec((1, d), lambda i: (0, 0)),
               pl.BlockSpec((d, 3 * d), lambda i: (0, 0)),
               pl.BlockSpec((CONV_W, d), lambda i: (0, 0)),
               pl.BlockSpec((d, d), lambda i: (0, 0))]
    if state is None:
        assert seq_len % tm == 0 and seq_len >= CONV_W - 1
        tiles_per_seq = seq_len // tm
        out, tail = pl.pallas_call(
            functools.partial(_conv_long_kernel, tiles_per_seq=tiles_per_seq),
            grid=(n // tm,),
            in_specs=[row_spec] + w_specs,
            out_specs=[row_spec, pl.BlockSpec((1, SUBLANE, d), lambda i: (i, 0, 0))],
            out_shape=[jax.ShapeDtypeStruct((n, d), F32), jax.ShapeDtypeStruct((n // tm, SUBLANE, d), F32)],
            scratch_shapes=[pltpu.VMEM((SUBLANE, d), F32)],
            compiler_params=_params("arbitrary"),
            name="conv_long",
        )(x, g, w_in, cw, w_out)
        new_state = tail.reshape(n_seq, tiles_per_seq, SUBLANE, d)[:, -1, SUBLANE - (CONV_W - 1):]
        return out, new_state
    assert tm % seq_len == 0 and seq_len >= CONV_W - 1
    zeros = jnp.zeros((n_seq, seq_len, d), F32)
    pa = zeros.at[:, 0].set(state[:, 1]).reshape(n, d)
    pb = zeros.at[:, 0].set(state[:, 0]).at[:, 1].set(state[:, 1]).reshape(n, d)
    out, u = pl.pallas_call(
        functools.partial(_conv_short_kernel, seg=seq_len),
        grid=(n // tm,),
        in_specs=[row_spec] + w_specs + [row_spec, row_spec],
        out_specs=[row_spec, row_spec],
        out_shape=[jax.ShapeDtypeStruct((n, d), F32), jax.ShapeDtypeStruct((n, d), F32)],
        compiler_params=_params("arbitrary"),
        name="conv_short",
    )(x, g, w_in, cw, w_out, pa, pb)
    new_state = u.reshape(n_seq, seq_len, d)[:, seq_len - (CONV_W - 1):]
    return out, new_state


def _mlp_body(x, g_ref, wup_ref, wdn_ref, o_ref):
    h = _rms(x, g_ref[...]).astype(BF16)
    acc = x
    for c in range(wup_ref.shape[1] // FF_CHUNK):
        a = _dot(h, wup_ref[:, c * FF_CHUNK:(c + 1) * FF_CHUNK])
        a = jnp.square(jnp.maximum(a, 0.0)).astype(BF16)
        acc = acc + _dot(a, wdn_ref[c * FF_CHUNK:(c + 1) * FF_CHUNK, :])
    o_ref[...] = acc


def _mlp_kernel(x_ref, g_ref, wup_ref, wdn_ref, o_ref):
    _mlp_body(x_ref[...], g_ref, wup_ref, wdn_ref, o_ref)


def _proj_mlp_kernel(x_ref, a_ref, wo_ref, g_ref, wup_ref, wdn_ref, o_ref):
    _mlp_body(x_ref[...] + _dot(a_ref[...], wo_ref[...]), g_ref, wup_ref, wdn_ref, o_ref)


def _mlp_layer(x, g, w_up, w_down, attn=None, w_o=None):
    n, d = x.shape
    ff = w_up.shape[1]
    assert ff % FF_CHUNK == 0
    tm = min(ROW_TILE, n)
    assert n % tm == 0
    row_spec = pl.BlockSpec((tm, d), lambda i: (i, 0))
    w_specs = [pl.BlockSpec((1, d), lambda i: (0, 0)),
               pl.BlockSpec((d, ff), lambda i: (0, 0)),
               pl.BlockSpec((ff, d), lambda i: (0, 0))]
    if attn is None:
        kern, in_specs, args = _mlp_kernel, [row_spec] + w_specs, (x, g, w_up, w_down)
    else:
        kern = _proj_mlp_kernel
        in_specs = [row_spec, row_spec, pl.BlockSpec((d, d), lambda i: (0, 0))] + w_specs
        args = (x, attn, w_o, g, w_up, w_down)
    return pl.pallas_call(
        kern, grid=(n // tm,), in_specs=in_specs, out_specs=row_spec,
        out_shape=jax.ShapeDtypeStruct((n, d), F32),
        compiler_params=_params("arbitrary"),
        name="mlp" if attn is None else "proj_mlp",
    )(*args)


def _kv_kernel(x_ref, g_ref, w_ref, ones_ref, ks_ref, kw_ref, cmp_ref, slc_ref, win_ref):
    h = _rms(x_ref[...], g_ref[...]).astype(BF16)
    kv = _dot(h, w_ref[...])
    ones_bd = ones_ref[...]
    cmp_ref[...] = kv[:, 0:ROW_LANES]
    for br, (gain_ref, f_ref) in enumerate(((ks_ref, slc_ref), (kw_ref, win_ref)), start=1):
        base = br * ROW_LANES
        f_ref[:, 0:KV_LANES] = _head_rms(kv[:, base:base + KV_LANES], ones_bd, gain_ref[...])
        f_ref[:, KV_LANES:ROW_LANES] = kv[:, base + KV_LANES:base + ROW_LANES]


def _kv_rows(x, kv_norm, w_kv, ones_bd, ks_gain, kw_gain):
    n, d = x.shape
    tm = min(ROW_TILE, n)
    assert n % tm == 0
    row_spec = pl.BlockSpec((tm, d), lambda i: (i, 0))
    kv_spec = pl.BlockSpec((tm, ROW_LANES), lambda i: (i, 0))
    const = lambda shape: pl.BlockSpec(shape, lambda i: (0, 0))
    return pl.pallas_call(
        _kv_kernel, grid=(n // tm,),
        in_specs=[row_spec, const((1, d)), const(w_kv.shape), const((KV_LANES, KV_LANES)),
                  const((1, KV_LANES)), const((1, KV_LANES))],
        out_specs=[kv_spec] * 3,
        out_shape=[jax.ShapeDtypeStruct((n, ROW_LANES), F32)] * 3,
        compiler_params=_params("arbitrary"),
        name="kv_rows",
    )(x, kv_norm, w_kv, ones_bd, ks_gain, kw_gain)


def _kv_long_kernel(x_ref, g_ref, w_ref, ones_ref, ks_ref, kw_ref, kc_gain_ref, pk_ref, pw_ref,
                    cmp_ref, slc_ref, win_ref, kc_ref, vc_ref, sk_ref, svt_ref, wk_ref, wvt_ref, *, tiles_per_seq):
    tm = x_ref.shape[0]
    pos0 = (pl.program_id(0) % tiles_per_seq) * tm
    h = _rms(x_ref[...], g_ref[...]).astype(BF16)
    kv = _dot(h, w_ref[...])
    ones_bd = ones_ref[...]

    cmp_k, cmp_v = kv[:, 0:KV_LANES], kv[:, KV_LANES:ROW_LANES]
    cmp_ref[0, 0] = cmp_k.T
    cmp_ref[0, 1] = cmp_v.T
    kc_ref[...] = _head_rms(jnp.mean(cmp_k.reshape(tm // L_CMP, L_CMP, KV_LANES), axis=1), ones_bd, kc_gain_ref[...])
    vc_ref[...] = jnp.mean(cmp_v.reshape(tm // L_CMP, L_CMP, KV_LANES), axis=1)

    blk = (pos0 + lax.broadcasted_iota(jnp.int32, (tm, 1), 0)) // L_SEL
    onehot = lax.broadcasted_iota(jnp.int32, (1, KV_LANES), 1) - HEAD_DIM == blk
    ones_rows = jnp.ones((V_ROWS - HEAD_DIM, tm), F32)
    for br, (gain_ref, f_ref) in enumerate(((ks_ref, slc_ref), (kw_ref, win_ref)), start=1):
        base = br * ROW_LANES
        k = _head_rms(kv[:, base:base + KV_LANES], ones_bd, gain_ref[...])
        vt = kv[:, base + KV_LANES:base + ROW_LANES].T
        f_ref[0, 0] = k.T
        f_ref[0, 1] = vt
        kb = k.astype(BF16)
        for g in range(N_KV):
            vt_g = jnp.concatenate([vt[g * HEAD_DIM:(g + 1) * HEAD_DIM], ones_rows], axis=0).astype(BF16)
            if br == 1:
                sk_ref[0, g] = jnp.where(onehot, 1.0, _dot(kb, pk_ref[g])).astype(BF16)
                svt_ref[0, g, 0] = vt_g
            else:
                wk_ref[0, g] = _dot(kb, pw_ref[g]).astype(BF16)
                for j in range(tm // LANE):
                    wvt_ref[0, g, j] = vt_g[:, j * LANE:(j + 1) * LANE]


def _kv_rows_long(x, kv_norm, w_kv, ones_bd, ks_gain, kw_gain, kc_gain, pk, pw, *, seq_len):
    n, d = x.shape
    tm = SEL_TILE
    assert seq_len % tm == 0 and -(-seq_len // L_SEL) <= LANE and tm % (SUBLANE * L_CMP) == 0
    n_seq, tps = n // seq_len, seq_len // tm
    nb = tm // L_CMP
    row_spec = pl.BlockSpec((tm, d), lambda i: (i, 0))
    t_spec = pl.BlockSpec((1, 2, KV_LANES, tm), lambda i: (i // tps, 0, 0, i % tps))
    blk_spec = pl.BlockSpec((nb, KV_LANES), lambda i: (i, 0))
    const = lambda shape: pl.BlockSpec(shape, lambda i: (0,) * len(shape))
    grp = lambda *tail: pl.BlockSpec((1, N_KV) + tail, lambda i: (i // tps, 0, i % tps) + (0,) * (len(tail) - 1))
    sds = jax.ShapeDtypeStruct
    return pl.pallas_call(
        functools.partial(_kv_long_kernel, tiles_per_seq=tps), grid=(n // tm,),
        in_specs=[row_spec, const((1, d)), const(w_kv.shape), const((KV_LANES, KV_LANES)),
                  const((1, KV_LANES)), const((1, KV_LANES)), const((1, KV_LANES)), const(pk.shape), const(pw.shape)],
        out_specs=[t_spec] * 3 + [blk_spec] * 2 + [
            grp(tm, KV_LANES), grp(1, V_ROWS, tm), grp(tm, LANE), grp(tm // LANE, V_ROWS, LANE)],
        out_shape=[sds((n_seq, 2, KV_LANES, seq_len), F32)] * 3 + [sds((n // L_CMP, KV_LANES), F32)] * 2 + [
            sds((n_seq, N_KV, seq_len, KV_LANES), BF16), sds((n_seq, N_KV, tps, V_ROWS, tm), BF16),
            sds((n_seq, N_KV, seq_len, LANE), BF16), sds((n_seq, N_KV, seq_len // LANE, V_ROWS, LANE), BF16)],
        compiler_params=_params("arbitrary"),
        name="kv_rows_long",
    )(x, kv_norm, w_kv, ones_bd, ks_gain, kw_gain, kc_gain, pk, pw)


def _page_specs(n_pages, page, seqs=1):
    return [pl.BlockSpec((1, 2, KV_LANES, page), lambda b, pt, s=s, p=p: (pt[seqs * b + s, p], 0, 0, 0))
            for s in range(seqs) for p in range(n_pages)]


def _compress_pages_kernel(pt_ref, *refs, n_pages):
    page_refs = refs[:n_pages]
    pool_ref, gain_ref, kc_ref, vc_ref = refs[n_pages:]
    page = page_refs[0].shape[3]
    rows = jnp.concatenate([page_refs[p][0].reshape(ROW_LANES, page) for p in range(n_pages)], axis=1)
    acc = _dot(rows, pool_ref[...].astype(F32))
    n_col = acc.shape[1]
    k = acc[0:KV_LANES].reshape(N_KV, HEAD_DIM, n_col)
    inv = lax.rsqrt(jnp.mean(k * k, axis=1, keepdims=True) + EPS)
    kc_ref[0] = ((k * inv).reshape(KV_LANES, n_col) * gain_ref[...]).astype(kc_ref.dtype)
    vc_ref[0] = acc[KV_LANES:ROW_LANES].astype(vc_ref.dtype)


def _compress_pages(pool, page_table, pool_mat, kc_gain_col):
    n_seq, n_pages = page_table.shape
    page = pool.shape[3]
    n_col = pool_mat.shape[1]
    const = lambda shape: pl.BlockSpec(shape, lambda b, pt: (0,) * len(shape))
    out_spec = pl.BlockSpec((1, KV_LANES, n_col), lambda b, pt: (b, 0, 0))
    return pl.pallas_call(
        functools.partial(_compress_pages_kernel, n_pages=n_pages),
        grid_spec=pltpu.PrefetchScalarGridSpec(
            num_scalar_prefetch=1, grid=(n_seq,),
            in_specs=_page_specs(n_pages, page) + [const(pool_mat.shape), const((KV_LANES, 1))],
            out_specs=[out_spec, out_spec]),
        out_shape=[jax.ShapeDtypeStruct((n_seq, KV_LANES, n_col), BF16)] * 2,
        compiler_params=_params("arbitrary"),
        name="compress_pages",
    )(page_table, *([pool] * n_pages), pool_mat, kc_gain_col)


def _even_odd(blocks, half):
    b, nbc, w = blocks.shape
    out = jnp.zeros((b, 2 * half, w), BF16)
    out = out.at[:, 0:(nbc + 1) // 2].set(blocks[:, 0::2].astype(BF16))
    return out.at[:, half:half + nbc // 2].set(blocks[:, 1::2].astype(BF16))


def _qg_kernel(x_ref, g_ref, wq_ref, wg_ref, ones_ref, qgain_ref, q_ref, gate_ref):
    h = _rms(x_ref[...], g_ref[...]).astype(BF16)
    q = _dot(h, wq_ref[...])
    ones_bd = ones_ref[...]
    gain = qgain_ref[...] * (HEAD_DIM ** -0.5 * LOG2E)
    for s in range(q.shape[1] // KV_LANES):
        sl = slice(s * KV_LANES, (s + 1) * KV_LANES)
        q_ref[:, sl] = _head_rms(q[:, sl], ones_bd, gain).astype(BF16)
    gate_ref[...] = jax.nn.sigmoid(_dot(h, wg_ref[...]))


def _qg_proj(x, g, w_q, w_g, ones_bd, q_gain):
    n, d = x.shape
    tm = min(ROW_TILE, n)
    assert n % tm == 0
    nq = w_q.shape[1]
    const = lambda shape: pl.BlockSpec(shape, lambda i: (0, 0))
    return pl.pallas_call(
        _qg_kernel, grid=(n // tm,),
        in_specs=[pl.BlockSpec((tm, d), lambda i: (i, 0)), const((1, d)), const(w_q.shape), const(w_g.shape),
                  const((KV_LANES, KV_LANES)), const((1, KV_LANES))],
        out_specs=[pl.BlockSpec((tm, nq), lambda i: (i, 0)), pl.BlockSpec((tm, w_g.shape[1]), lambda i: (i, 0))],
        out_shape=[jax.ShapeDtypeStruct((n, nq), BF16), jax.ShapeDtypeStruct((n, w_g.shape[1]), F32)],
        compiler_params=_params("arbitrary"),
        name="qg_proj",
    )(x, g, w_q, w_g, ones_bd, q_gain)


def _select_blocks(imp, t, k_sel, axis):
    blk = lax.broadcasted_iota(jnp.int32, imp.shape, axis)
    cur = t // L_SEL
    forced = (blk == 0) | (blk == cur) | (blk == cur - 1)
    n_forced = 3
    if k_sel < n_forced:
        work = jnp.where(blk > cur, NEG, jnp.where(forced, FORCE, imp))
        sel = jnp.zeros(imp.shape, jnp.bool_)
        rounds = k_sel
    else:
        work = jnp.where(forced | (blk > cur), NEG, imp)
        sel = forced & (blk <= cur)
        rounds = k_sel - n_forced
    blk_f = blk.astype(F32)
    for _ in range(rounds):
        m = jnp.max(work, axis=axis, keepdims=True)
        first = jnp.min(jnp.where(work == m, blk_f, float(imp.shape[axis])), axis=axis, keepdims=True)
        pick = blk_f == first
        sel = sel | (pick & (m > 0.5 * NEG))
        work = jnp.where(pick, -jnp.inf, work)
    return sel


def _attn_seq_kernel(q_ref, gate_ref, kc_ref, vct_ref, sk_ref, svt_ref, wk_ref, wvt_ref, wbias_ref, o_ref,
                     *scratch, nbc, half, k_sel):
    tq = q_ref.shape[1]
    c0 = pl.program_id(2) * tq
    cols = GROUP * tq
    t = c0 + lax.broadcasted_iota(jnp.int32, (1, cols), 1) % tq
    t1 = t[:, 0:tq]

    qt = q_ref[0].astype(F32).T
    q64 = jnp.concatenate([qt[r * HEAD_DIM:(r + 1) * HEAD_DIM] for r in range(GROUP)], axis=1)
    zeros64 = jnp.zeros((HEAD_DIM, cols), F32)
    q_pad = jnp.concatenate([q64, zeros64], axis=0).astype(BF16)

    row = lax.broadcasted_iota(jnp.int32, (2 * half, 1), 0)
    blk = 2 * (row % half) + row // half
    p = _softmax2(_dot(kc_ref[0, 0], q_pad), ((blk + 1) * L_CMP - 1 <= t) & (blk < nbc), 0)
    o_cmp = _dot(vct_ref[0, 0], p.astype(BF16))[0:HEAD_DIM]
    ph = p[:, 0:tq]
    for r in range(1, GROUP):
        ph = ph + p[:, r * tq:(r + 1) * tq]
    sel = _select_blocks(ph[0:half] + ph[half:2 * half], t1, k_sel, 0)
    bias = jnp.where(sel, 0.0, NEG)
    q_sel = jnp.concatenate([q64, jnp.concatenate([bias] * GROUP, axis=1), zeros64], axis=0).astype(BF16)

    band = tq + WINDOW
    wb = jnp.maximum(c0 - WINDOW, 0) // LANE
    w0 = pl.multiple_of(wb * LANE, LANE)
    s_win = _dot(wk_ref[0, 0, pl.ds(w0, band), :], q_pad) + wbias_ref[0]

    ns = SWEEP_PARTS
    m_refs, acc_refs = scratch[0:ns], scratch[ns:2 * ns]
    buf_x, buf_y = scratch[2 * ns:3 * ns], scratch[3 * ns:4 * ns]
    hk = SEL_TILE // ns
    for m_ref, acc_ref in zip(m_refs, acc_refs):
        m_ref[...] = jnp.full(m_ref.shape, NEG, F32)
        acc_ref[...] = jnp.zeros(acc_ref.shape, F32)

    def scores(kt, h):
        k0 = pl.multiple_of(kt * SEL_TILE + h * hk, hk)
        return _dot(sk_ref[0, 0, pl.ds(k0, hk), :], q_sel)

    def update(s, kt, h, causal):
        m_ref, acc_ref = m_refs[h], acc_refs[h]
        if causal:
            s = jnp.where(kt * SEL_TILE + h * hk + lax.broadcasted_iota(jnp.int32, (hk, 1), 0) <= t, s, NEG)
        m_old = m_ref[...]
        m_new = jnp.maximum(m_old, jnp.max(s, axis=0, keepdims=True))
        p = jnp.exp2(s - m_new).astype(BF16)
        vt = svt_ref[0, 0, kt][:, h * hk:(h + 1) * hk]
        acc_ref[...] = jnp.exp2(m_old - m_new) * acc_ref[...] + _dot(vt, p)
        m_ref[...] = m_new

    def fill(kt, buf):
        for h in range(ns):
            buf[h][...] = scores(kt, h)

    def drain(kt, buf, causal):
        for h in range(ns):
            update(buf[h][...], kt, h, causal)

    last = (c0 + tq - 1) // SEL_TILE
    fill(0, buf_x)

    p = jnp.exp2(s_win - jnp.max(s_win, axis=0, keepdims=True)).astype(BF16)
    wvt = jnp.concatenate([wvt_ref[0, 0, wb + j] for j in range(band // LANE)], axis=1)
    o_win = _dot(wvt, p)
    o_win = o_win[0:HEAD_DIM] * (1.0 / o_win[HEAD_DIM:HEAD_DIM + 1])

    def pair(kt):
        fill(kt + 1, buf_y)
        drain(kt, buf_x, False)
        fill(kt + 2, buf_x)
        drain(kt + 1, buf_y, False)

    def quad(i, carry):
        pair(4 * i)
        pair(4 * i + 2)
        return carry

    def one_pair(i, carry):
        pair(2 * i)
        return carry

    def octet(i, carry):
        quad(2 * i, carry)
        return quad(2 * i + 1, carry)

    lax.fori_loop(0, last // 8, octet, 0)
    lax.fori_loop(2 * (last // 8), last // 4, quad, 0)
    lax.fori_loop(2 * (last // 4), last // 2, one_pair, 0)

    @pl.when(last % 2 == 0)
    def _():
        drain(last, buf_x, True)

    @pl.when(last % 2 == 1)
    def _():
        fill(last, buf_y)
        drain(last - 1, buf_x, False)
        drain(last, buf_y, True)

    m = functools.reduce(jnp.maximum, [m_ref[...] for m_ref in m_refs])
    acc = sum(jnp.exp2(m_ref[...] - m) * acc_ref[...] for m_ref, acc_ref in zip(m_refs, acc_refs))
    o_sel = acc[0:HEAD_DIM] * (1.0 / acc[HEAD_DIM:HEAD_DIM + 1])

    gt = gate_ref[0].T
    gate = lambda br: jnp.concatenate([gt[r * N_BRANCH + br:r * N_BRANCH + br + 1] for r in range(GROUP)], axis=1)
    o = gate(0) * o_cmp + gate(1) * o_sel + gate(2) * o_win
    o = jnp.concatenate([o[:, r * tq:(r + 1) * tq] for r in range(GROUP)], axis=0)
    o_ref[0] = o.T.astype(o_ref.dtype)


def _window_bias(tq):
    v = jnp.arange(WINDOW // tq + 1)[:, None, None]
    j = jnp.arange(tq + WINDOW)[None, :, None]
    i = (jnp.arange(GROUP * tq) % tq)[None, None, :]
    dist = (v * tq + i) - (jnp.maximum(v * tq - WINDOW, 0) + j)
    return jnp.where((dist >= 0) & (dist <= WINDOW), 0.0, NEG).astype(F32)


def _attn_seq(q, gates, kc, vct, sk, svt, wk, wvt, *, nbc, half, k_sel):
    b, t_len, nq = q.shape
    tq = ATTN_CHUNK
    assert tq % LANE == 0 and half == LANE and t_len % SEL_TILE == 0 and t_len >= tq + WINDOW and WINDOW % tq == 0
    cols = GROUP * tq
    band = tq + WINDOW
    n_var = WINDOW // tq
    grp = lambda *shape: pl.BlockSpec((1, 1) + shape, lambda i, g, c: (i, g) + (0,) * len(shape))
    chunk = lambda w: pl.BlockSpec((1, tq, w), lambda i, g, c: (i, c, g))
    wbias_spec = pl.BlockSpec((1, band, cols), lambda i, g, c: (jnp.minimum(c, n_var), 0, 0))
    stat = pltpu.VMEM((1, cols), F32)
    acc = pltpu.VMEM((V_ROWS, cols), F32)
    return pl.pallas_call(
        functools.partial(_attn_seq_kernel, nbc=nbc, half=half, k_sel=k_sel),
        grid=(b, N_KV, t_len // tq),
        in_specs=[chunk(KV_LANES), chunk(LANE), grp(2 * half, LANE), grp(HEAD_DIM, 2 * half),
                  grp(t_len, KV_LANES), grp(t_len // SEL_TILE, V_ROWS, SEL_TILE),
                  grp(t_len, LANE), grp(t_len // LANE, V_ROWS, LANE), wbias_spec],
        out_specs=chunk(KV_LANES),
        out_shape=jax.ShapeDtypeStruct((b, t_len, nq), BF16),
        scratch_shapes=[stat] * SWEEP_PARTS + [acc] * SWEEP_PARTS
        + [pltpu.VMEM((SEL_TILE // SWEEP_PARTS, cols), F32)] * (2 * SWEEP_PARTS),
        compiler_params=_params("arbitrary", "arbitrary", "arbitrary"),
        name="attn_seq",
    )(q, gates, kc, vct, sk, svt, wk, wvt, _window_bias(tq))


def _attn_paged_kernel(pt_ref, *refs, n_pages, seqs, past, nbc, half, k_sel):
    page_refs = refs[:seqs * n_pages]
    (q_ref, gate_ref, kc_ref, vc_ref, snew_ref, wold_ref, wnew_ref, e_ref, ge_ref, o_ref) = refs[seqs * n_pages:]
    tq = q_ref.shape[1]
    page = page_refs[0].shape[3]
    w_buf = wold_ref.shape[3]
    rows_g = GROUP * tq
    t1 = past + lax.broadcasted_iota(jnp.int32, (tq, 1), 0)
    t = jnp.concatenate([t1] * (N_KV * GROUP), axis=0)
    t_col = past + lax.broadcasted_iota(jnp.int32, (1, LANE), 1) % tq
    lane_group = lax.broadcasted_iota(jnp.int32, (1, KV_LANES), 1) // HEAD_DIM
    col = lax.broadcasted_iota(jnp.int32, (1, 2 * half), 1)
    blk = 2 * (col % half) + col // half
    cmp_mask = ((blk + 1) * L_CMP - 1 <= t) & (blk < nbc)

    def pad_rows(x):
        return jnp.concatenate([x, jnp.zeros((LANE - x.shape[0], x.shape[1]), x.dtype)], axis=0).astype(BF16)

    for sq in range(seqs):
        pages = page_refs[sq * n_pages:(sq + 1) * n_pages]
        q_slabs = [q_ref[sq, :, r * KV_LANES:(r + 1) * KV_LANES].astype(F32) for r in range(GROUP)]
        qe = jnp.concatenate([jnp.where(lane_group == g, s, 0.0) for g in range(N_KV) for s in q_slabs], axis=0).astype(BF16)

        p = _softmax2(_dot(qe, kc_ref[sq]), cmp_mask, 1)
        o_cmp = _dot_nt(p.astype(BF16), vc_ref[sq])
        imps = []
        for g in range(N_KV):
            ph = p[g * rows_g:g * rows_g + tq]
            for r in range(1, GROUP):
                ph = ph + p[g * rows_g + r * tq:g * rows_g + (r + 1) * tq]
            imps.append(ph[:, 0:half] + ph[:, half:2 * half])
        imp = jnp.concatenate(imps + [jnp.zeros((LANE - N_KV * tq, half), F32)], axis=0)
        bias = jnp.where(_select_blocks(imp.T, t_col, k_sel, 0), 0.0, NEG).T
        bias = jnp.concatenate([bias[g * tq:(g + 1) * tq] for g in range(N_KV) for _ in range(GROUP)], axis=0)

        snew = pad_rows(snew_ref[sq])
        s = jnp.concatenate([_dot(qe, pages[i][0, 0].astype(BF16)) for i in range(n_pages)]
                            + [_dot_nt(qe, snew[:, 0:KV_LANES])], axis=1) + _dot(bias.astype(BF16), e_ref[...])
        kpos = lax.broadcasted_iota(jnp.int32, (1, s.shape[1]), 1)
        p = _softmax2(s, kpos <= t, 1).astype(BF16)
        vt_all = jnp.concatenate([pages[i][0, 1].astype(BF16) for i in range(n_pages)], axis=1)
        o_sel = _dot_nt(p[:, 0:n_pages * page], vt_all) + _dot(p[:, n_pages * page:], snew[:, KV_LANES:ROW_LANES])

        wnew = pad_rows(wnew_ref[sq])
        s = jnp.concatenate([_dot(qe, wold_ref[sq, 0].astype(BF16)), _dot_nt(qe, wnew[:, 0:KV_LANES])], axis=1)
        j = lax.broadcasted_iota(jnp.int32, (1, s.shape[1]), 1)
        dist = t - (past - w_buf + j)
        p = _softmax2(s, (dist >= 0) & (dist <= WINDOW) & (j < w_buf + tq), 1).astype(BF16)
        o_win = _dot_nt(p[:, 0:w_buf], wold_ref[sq, 1].astype(BF16)) + _dot(p[:, w_buf:], wnew[:, KV_LANES:ROW_LANES])

        gates = [_split_dot(gate_ref[sq], ge_ref[br]) for br in range(N_BRANCH)]
        for r in range(GROUP):
            acc = None
            for br, o_br in enumerate((o_cmp, o_sel, o_win)):
                o = jnp.zeros((tq, KV_LANES), F32)
                for g in range(N_KV):
                    o = jnp.where(lane_group == g, o_br[g * rows_g + r * tq:g * rows_g + (r + 1) * tq], o)
                term = gates[br][:, r * KV_LANES:(r + 1) * KV_LANES] * o
                acc = term if acc is None else acc + term
            o_ref[sq, :, r * KV_LANES:(r + 1) * KV_LANES] = acc.astype(o_ref.dtype)


def _attn_paged(q, gates, kc, vc, pool, page_table, slc_new, win_old, win_new, e_sel, ge, *, nbc, half, k_sel):
    n_seq, tq, nq = q.shape
    n_pages = page_table.shape[1]
    page = pool.shape[3]
    past = n_pages * page
    w_buf = win_old.shape[3]
    seqs = DECODE_SEQS
    assert page == LANE and half == LANE and w_buf % LANE == 0 and tq % SUBLANE == 0 and N_KV * tq <= LANE and n_seq % seqs == 0
    per_seq = lambda *shape: pl.BlockSpec((seqs,) + shape, lambda b, pt: (b,) + (0,) * len(shape))
    const = lambda shape: pl.BlockSpec(shape, lambda b, pt: (0,) * len(shape))
    return pl.pallas_call(
        functools.partial(_attn_paged_kernel, n_pages=n_pages, seqs=seqs, past=past, nbc=nbc, half=half, k_sel=k_sel),
        grid_spec=pltpu.PrefetchScalarGridSpec(
            num_scalar_prefetch=1, grid=(n_seq // seqs,),
            in_specs=_page_specs(n_pages, page, seqs) + [
                per_seq(tq, nq), per_seq(tq, LANE), per_seq(KV_LANES, 2 * half), per_seq(KV_LANES, 2 * half),
                per_seq(tq, ROW_LANES), per_seq(2, KV_LANES, w_buf), per_seq(tq, ROW_LANES),
                const(e_sel.shape), const(ge.shape)],
            out_specs=per_seq(tq, nq)),
        out_shape=jax.ShapeDtypeStruct((n_seq, tq, nq), BF16),
        compiler_params=_params("arbitrary"),
        name="attn_paged",
    )(page_table, *([pool] * (seqs * n_pages)), q, gates, kc, vc, slc_new, win_old, win_new, e_sel, ge)


def _round_up(x, m):
    return -(-x // m) * m


def _block_onehot(n_rows, n_keys, t_valid):
    key = jnp.arange(n_keys)[None, :]
    return ((key // L_SEL == jnp.arange(n_rows)[:, None]) & (key < t_valid)).astype(BF16)


def _keys_minor(rows5):
    b, t = rows5.shape[:2]
    return jnp.transpose(rows5, (0, 2, 3, 4, 1)).reshape(b, 2, KV_LANES, t)


def _rows_major(kt, row):
    b, t = kt.shape[0], kt.shape[3]
    return jnp.transpose(kt.reshape((b,) + row + (t,)), (0, 4, 1, 2, 3))


def kernel(x_prompt, x_sample, cache_cmp_kv, cache_slc_kv, cache_win_kv, state_conv, page_table, norm1, a_w_in, a_conv_w, a_w_out, kv_norm, w_kv, kc_norm, ks_norm, kw_norm, b_w_qg, b_q_norm, b_w_o, norm2, w_up, w_down):
    bp, tp, d = x_prompt.shape
    bs, ts, _ = x_sample.shape
    depth = norm1.shape[0]
    n_a = a_w_in.shape[0]
    n_pool, page = cache_cmp_kv.shape[:2]
    n_pages = page_table.shape[1]
    past = n_pages * page
    w_buf = cache_win_kv.shape[1]
    row = cache_cmp_kv.shape[2:]
    nq = N_HEADS * HEAD_DIM
    assert d == nq and row == (2, N_KV, HEAD_DIM) and past % L_CMP == 0 and page % L_CMP == 0 and ts < L_CMP

    hperm = jnp.array([(g * GROUP + r) * HEAD_DIM + dd for r in range(GROUP) for g in range(N_KV) for dd in range(HEAD_DIM)])
    lane = jnp.arange(KV_LANES)
    ones_bd = (lane[:, None] // HEAD_DIM == lane[None, :] // HEAD_DIM).astype(BF16)
    tile4 = lambda v: jnp.tile(v.astype(F32), N_KV)[None, :]
    head_of_lane = (jnp.arange(nq) // HEAD_DIM % N_KV) * GROUP + jnp.arange(nq) // KV_LANES
    gcol = jnp.arange(LANE)
    ge = jnp.stack([(gcol[:, None] == head_of_lane[None, :] * N_BRANCH + br) for br in range(N_BRANCH)]).astype(BF16)
    w_in_b = a_w_in.astype(BF16)
    w_out_b = a_w_out.astype(BF16)
    w_up_b = w_up.astype(BF16)
    w_down_b = w_down.astype(BF16)
    w_kv_b = w_kv.astype(BF16)
    w_q_nat = b_w_qg[:, :, :nq].astype(BF16)
    w_q_b = w_q_nat[:, :, hperm]
    w_g_b = jnp.pad(b_w_qg[:, :, nq:], ((0, 0), (0, 0), (0, LANE - N_HEADS * N_BRANCH))).astype(BF16)
    w_g_grp = b_w_qg[:, :, nq:].reshape(b_w_qg.shape[0], d, N_KV, GROUP * N_BRANCH)
    w_g_grp = jnp.pad(w_g_grp, ((0, 0), (0, 0), (0, 0), (0, LANE - GROUP * N_BRANCH))).reshape(-1, d, N_KV * LANE).astype(BF16)
    w_o_nat = b_w_o.astype(BF16)
    w_o_b = w_o_nat[:, hperm, :]
    src = jnp.arange(KV_LANES)[None, :, None]
    dst = jnp.arange(KV_LANES)[None, None, :]
    pk = ((src == jnp.arange(N_KV)[:, None, None] * HEAD_DIM + dst) & (dst < HEAD_DIM)).astype(BF16)
    pw = pk[:, :, 0:LANE]
    g1 = norm1.astype(F32)[:, None, :]
    g2 = norm2.astype(F32)[:, None, :]
    kvn = kv_norm.astype(F32)[None, :]

    xp = x_prompt.reshape(bp * tp, d)
    xs = x_sample.reshape(bs * ts, d)
    p_conv, s_conv = [], []
    for l in range(n_a):
        xp, st = _conv_layer(xp, g1[l], w_in_b[l], a_conv_w[l], w_out_b[l], seq_len=tp)
        p_conv.append(st)
        xs, st = _conv_layer(xs, g1[l], w_in_b[l], a_conv_w[l], w_out_b[l], seq_len=ts, state=state_conv[l])
        s_conv.append(st)
        xp = _mlp_layer(xp, g2[l], w_up_b[l], w_down_b[l])
        xs = _mlp_layer(xs, g2[l], w_up_b[l], w_down_b[l])

    ks_g, kw_g, kc_g = tile4(ks_norm), tile4(kw_norm), tile4(kc_norm)
    (p_cmp_t, p_slc_t, p_win_t, kc_p, vc_p, p_sk, p_svt, p_wk, p_wvt) = _kv_rows_long(
        xp, kvn, w_kv_b, ones_bd, ks_g, kw_g, kc_g, pk, pw, seq_len=tp)
    s_cmp, s_slc, s_win = _kv_rows(xs, kvn, w_kv_b, ones_bd, ks_g, kw_g)

    nbc_p = tp // L_CMP
    nbs_p = -(-tp // L_SEL)
    half_p = _round_up(nbs_p, LANE)
    per_group = lambda a: _even_odd(a.reshape(bp, nbc_p, KV_LANES), half_p).reshape(bp, 2 * half_p, N_KV, HEAD_DIM)
    kc_p = jnp.pad(per_group(kc_p).transpose(0, 2, 1, 3), ((0, 0), (0, 0), (0, 0), (0, LANE - HEAD_DIM)))
    vct_p = per_group(vc_p).transpose(0, 2, 3, 1)

    t_all = past + ts
    nbc_s = t_all // L_CMP
    assert nbc_s * L_CMP == past
    nbs_s = -(-t_all // L_SEL)
    half_s = _round_up(nbs_s, LANE)
    pool_cmp = _keys_minor(cache_cmp_kv)
    pool_slc = _keys_minor(cache_slc_kv)
    win_old = _keys_minor(cache_win_kv)
    blk_of_key = jnp.arange(past) // L_CMP
    col_of_key = blk_of_key // 2 + half_s * (blk_of_key % 2)
    pool_mat = (col_of_key[:, None] == jnp.arange(2 * half_s)[None, :]).astype(F32) * (1.0 / L_CMP)
    pool_mat = pool_mat.astype(BF16)
    kc_s, vc_s = _compress_pages(pool_cmp, page_table, pool_mat, jnp.tile(kc_norm.astype(F32), N_KV)[:, None])
    e_s = _block_onehot(half_s, past + LANE, t_all)

    for j in range(depth - n_a):
        l = n_a + j
        qg = tile4(b_q_norm[j])
        q, gates = _qg_proj(xp, g1[l], w_q_nat[j], w_g_grp[j], ones_bd, qg)
        o = _attn_seq(q.reshape(bp, tp, nq), gates.reshape(bp, tp, N_KV * LANE), kc_p, vct_p, p_sk, p_svt, p_wk, p_wvt,
                      nbc=nbc_p, half=half_p, k_sel=min(N_SEL, nbs_p))
        xp = _mlp_layer(xp, g2[l], w_up_b[l], w_down_b[l], attn=o.reshape(bp * tp, nq), w_o=w_o_nat[j])

        q, gates = _qg_proj(xs, g1[l], w_q_b[j], w_g_b[j], ones_bd, qg)
        o = _attn_paged(q.reshape(bs, ts, nq), gates.reshape(bs, ts, LANE), kc_s, vc_s, pool_slc, page_table,
                        s_slc.reshape(bs, ts, ROW_LANES), win_old, s_win.reshape(bs, ts, ROW_LANES), e_s, ge,
                        nbc=nbc_s, half=half_s, k_sel=min(N_SEL, nbs_s))
        xs = _mlp_layer(xs, g2[l], w_up_b[l], w_down_b[l], attn=o.reshape(bs * ts, nq), w_o=w_o_b[j])

    rows5 = lambda a, b_, t_: a.reshape((b_, t_) + row)
    s_win_all = jnp.concatenate([cache_win_kv, rows5(s_win, bs, ts)], axis=1)
    return (xp.reshape(bp, tp, d), xs.reshape(bs, ts, d),
            _rows_major(p_cmp_t, row), _rows_major(p_slc_t, row), _rows_major(p_win_t[..., tp - min(WINDOW, tp):], row),
            jnp.stack(p_conv),
            rows5(s_cmp, bs, ts), rows5(s_slc, bs, ts), s_win_all[:, -w_buf:], jnp.stack(s_conv))
```

```python
import functools
import math

import jax
import jax.numpy as jnp
from jax import lax
from jax.experimental import pallas as pl
from jax.experimental.pallas import tpu as pltpu

F32 = jnp.float32
BF16 = jnp.bfloat16

N_HEADS = 16
N_KV = 4
HEAD_DIM = 64
GROUP = N_HEADS // N_KV
N_BRANCH = 3
L_CMP = 32
L_SEL = 64
N_SEL = 8
WINDOW = 512
CONV_W = 3
EPS = 1e-6
NEG = -1e30
FORCE = 1e4
LOG2E = math.log2(math.e)

KV_LANES = N_KV * HEAD_DIM
ROW_LANES = 2 * KV_LANES
LANE = 128
SUBLANE = 8
V_ROWS = HEAD_DIM + 2 * SUBLANE
ROW_TILE = 512
FF_CHUNK = 1024
SEL_TILE = 512
SWEEP_PARTS = 1
ATTN_CHUNK = 256
DECODE_SEQS = 2
V7X_VMEM_BYTES = 64 * 1024 * 1024
VMEM_LIMIT = V7X_VMEM_BYTES * 7 // 8


def _params(*sem):
    return pltpu.CompilerParams(dimension_semantics=sem, vmem_limit_bytes=VMEM_LIMIT)


def _dot(a, b):
    return jnp.dot(a, b, preferred_element_type=F32)


def _dot_nt(a, b):
    return lax.dot_general(a, b, (((1,), (1,)), ((), ())), preferred_element_type=F32)


def _rms(x, g):
    return x * lax.rsqrt(jnp.mean(x * x, axis=-1, keepdims=True) + EPS) * g


def _split_dot(x, m):
    hi = x.astype(BF16)
    lo = (x - hi.astype(F32)).astype(BF16)
    return _dot(hi, m) + _dot(lo, m)


def _head_rms(k, ones_bd, gain):
    ss = _split_dot(k * k, ones_bd)
    return k * lax.rsqrt(ss * (1.0 / HEAD_DIM) + EPS) * gain


def _softmax2(s, mask, axis):
    s = jnp.where(mask, s, NEG)
    e = jnp.exp2(s - jnp.max(s, axis=axis, keepdims=True))
    return jnp.where(mask, e, 0.0) * (1.0 / jnp.sum(e, axis=axis, keepdims=True))


def _conv_core(x, g, win_ref, cw, wout_ref, s1_fix, s2_fix):
    d = x.shape[1]
    h = _rms(x, g).astype(BF16)
    bg = _dot(h, win_ref[:, 0:d])
    cg = _dot(h, win_ref[:, d:2 * d])
    xin = _dot(h, win_ref[:, 2 * d:3 * d])
    u = cg * xin
    s1 = s1_fix(pltpu.roll(u, 1, 0))
    s2 = s2_fix(pltpu.roll(u, 2, 0))
    z = cw[0:1] * s2 + cw[1:2] * s1 + cw[2:3] * u
    y = _dot((bg * z).astype(BF16), wout_ref[...])
    return x + y, u


def _conv_long_kernel(x_ref, g_ref, win_ref, cw_ref, wout_ref, o_ref, tail_ref, carry_ref, *, tiles_per_seq):
    tm = x_ref.shape[0]

    @pl.when(pl.program_id(0) % tiles_per_seq == 0)
    def _():
        carry_ref[...] = jnp.zeros_like(carry_ref)

    prev = carry_ref[...]
    row = lax.broadcasted_iota(jnp.int32, (tm, 1), 0)
    last, before = prev[SUBLANE - 1:SUBLANE], prev[SUBLANE - 2:SUBLANE - 1]
    s1_fix = lambda r: jnp.where(row == 0, last, r)
    s2_fix = lambda r: jnp.where(row == 0, before, jnp.where(row == 1, last, r))
    out, u = _conv_core(x_ref[...], g_ref[...], win_ref, cw_ref[...], wout_ref, s1_fix, s2_fix)
    o_ref[...] = out
    carry_ref[...] = u[tm - SUBLANE:tm]
    tail_ref[0] = u[tm - SUBLANE:tm]


def _conv_short_kernel(x_ref, g_ref, win_ref, cw_ref, wout_ref, pa_ref, pb_ref, o_ref, u_ref, *, seg):
    tm = x_ref.shape[0]
    pos = lax.broadcasted_iota(jnp.int32, (tm, 1), 0) % seg
    s1_fix = lambda r: jnp.where(pos < 1, pa_ref[...], r)
    s2_fix = lambda r: jnp.where(pos < 2, pb_ref[...], r)
    out, u = _conv_core(x_ref[...], g_ref[...], win_ref, cw_ref[...], wout_ref, s1_fix, s2_fix)
    o_ref[...] = out
    u_ref[...] = u


def _conv_layer(x, g, w_in, cw, w_out, *, seq_len, state=None):
    n, d = x.shape
    tm = min(ROW_TILE, n)
    assert n % tm == 0
    n_seq = n // seq_len
    row_spec = pl.BlockSpec((tm, d), lambda i: (i, 0))
    w_specs = [pl.BlockSpec((1, d), lambda i: (0, 0)),
               pl.BlockSpec((d, 3 * d), lambda i: (0, 0)),
               pl.BlockSpec((CONV_W, d), lambda i: (0, 0)),
               pl.BlockSpec((d, d), lambda i: (0, 0))]
    if state is None:
        assert seq_len % tm == 0 and seq_len >= CONV_W - 1
        tiles_per_seq = seq_len // tm
        out, tail = pl.pallas_call(
            functools.partial(_conv_long_kernel, tiles_per_seq=tiles_per_seq),
            grid=(n // tm,),
            in_specs=[row_spec] + w_specs,
            out_specs=[row_spec, pl.BlockSpec((1, SUBLANE, d), lambda i: (i, 0, 0))],
            out_shape=[jax.ShapeDtypeStruct((n, d), F32), jax.ShapeDtypeStruct((n // tm, SUBLANE, d), F32)],
            scratch_shapes=[pltpu.VMEM((SUBLANE, d), F32)],
            compiler_params=_params("arbitrary"),
            name="conv_long",
        )(x, g, w_in, cw, w_out)
        new_state = tail.reshape(n_seq, tiles_per_seq, SUBLANE, d)[:, -1, SUBLANE - (CONV_W - 1):]
        return out, new_state
    assert tm % seq_len == 0 and seq_len >= CONV_W - 1
    zeros = jnp.zeros((n_seq, seq_len, d), F32)
    pa = zeros.at[:, 0].set(state[:, 1]).reshape(n, d)
    pb = zeros.at[:, 0].set(state[:, 0]).at[:, 1].set(state[:, 1]).reshape(n, d)
    out, u = pl.pallas_call(
        functools.partial(_conv_short_kernel, seg=seq_len),
        grid=(n // tm,),
        in_specs=[row_spec] + w_specs + [row_spec, row_spec],
        out_specs=[row_spec, row_spec],
        out_shape=[jax.ShapeDtypeStruct((n, d), F32), jax.ShapeDtypeStruct((n, d), F32)],
        compiler_params=_params("arbitrary"),
        name="conv_short",
    )(x, g, w_in, cw, w_out, pa, pb)
    new_state = u.reshape(n_seq, seq_len, d)[:, seq_len - (CONV_W - 1):]
    return out, new_state


def _mlp_body(x, g_ref, wup_ref, wdn_ref, o_ref):
    h = _rms(x, g_ref[...]).astype(BF16)
    acc = x
    for c in range(wup_ref.shape[1] // FF_CHUNK):
        a = _dot(h, wup_ref[:, c * FF_CHUNK:(c + 1) * FF_CHUNK])
        a = jnp.square(jnp.maximum(a, 0.0)).astype(BF16)
        acc = acc + _dot(a, wdn_ref[c * FF_CHUNK:(c + 1) * FF_CHUNK, :])
    o_ref[...] = acc


def _mlp_kernel(x_ref, g_ref, wup_ref, wdn_ref, o_ref):
    _mlp_body(x_ref[...], g_ref, wup_ref, wdn_ref, o_ref)


def _proj_mlp_kernel(x_ref, a_ref, wo_ref, g_ref, wup_ref, wdn_ref, o_ref):
    _mlp_body(x_ref[...] + _dot(a_ref[...], wo_ref[...]), g_ref, wup_ref, wdn_ref, o_ref)


def _mlp_layer(x, g, w_up, w_down, attn=None, w_o=None):
    n, d = x.shape
    ff = w_up.shape[1]
    assert ff % FF_CHUNK == 0
    tm = min(ROW_TILE, n)
    assert n % tm == 0
    row_spec = pl.BlockSpec((tm, d), lambda i: (i, 0))
    w_specs = [pl.BlockSpec((1, d), lambda i: (0, 0)),
               pl.BlockSpec((d, ff), lambda i: (0, 0)),
               pl.BlockSpec((ff, d), lambda i: (0, 0))]
    if attn is None:
        kern, in_specs, args = _mlp_kernel, [row_spec] + w_specs, (x, g, w_up, w_down)
    else:
        kern = _proj_mlp_kernel
        in_specs = [row_spec, row_spec, pl.BlockSpec((d, d), lambda i: (0, 0))] + w_specs
        args = (x, attn, w_o, g, w_up, w_down)
    return pl.pallas_call(
        kern, grid=(n // tm,), in_specs=in_specs, out_specs=row_spec,
        out_shape=jax.ShapeDtypeStruct((n, d), F32),
        compiler_params=_params("arbitrary"),
        name="mlp" if attn is None else "proj_mlp",
    )(*args)


def _kv_kernel(x_ref, g_ref, w_ref, ones_ref, ks_ref, kw_ref, cmp_ref, slc_ref, win_ref):
    h = _rms(x_ref[...], g_ref[...]).astype(BF16)
    kv = _dot(h, w_ref[...])
    ones_bd = ones_ref[...]
    cmp_ref[...] = kv[:, 0:ROW_LANES]
    for br, (gain_ref, f_ref) in enumerate(((ks_ref, slc_ref), (kw_ref, win_ref)), start=1):
        base = br * ROW_LANES
        f_ref[:, 0:KV_LANES] = _head_rms(kv[:, base:base + KV_LANES], ones_bd, gain_ref[...])
        f_ref[:, KV_LANES:ROW_LANES] = kv[:, base + KV_LANES:base + ROW_LANES]


def _kv_rows(x, kv_norm, w_kv, ones_bd, ks_gain, kw_gain):
    n, d = x.shape
    tm = min(ROW_TILE, n)
    assert n % tm == 0
    row_spec = pl.BlockSpec((tm, d), lambda i: (i, 0))
    kv_spec = pl.BlockSpec((tm, ROW_LANES), lambda i: (i, 0))
    const = lambda shape: pl.BlockSpec(shape, lambda i: (0, 0))
    return pl.pallas_call(
        _kv_kernel, grid=(n // tm,),
        in_specs=[row_spec, const((1, d)), const(w_kv.shape), const((KV_LANES, KV_LANES)),
                  const((1, KV_LANES)), const((1, KV_LANES))],
        out_specs=[kv_spec] * 3,
        out_shape=[jax.ShapeDtypeStruct((n, ROW_LANES), F32)] * 3,
        compiler_params=_params("arbitrary"),
        name="kv_rows",
    )(x, kv_norm, w_kv, ones_bd, ks_gain, kw_gain)


def _kv_long_kernel(x_ref, g_ref, w_ref, ones_ref, ks_ref, kw_ref, kc_gain_ref, pk_ref, pw_ref,
                    cmp_ref, slc_ref, win_ref, kc_ref, vc_ref, sk_ref, svt_ref, wk_ref, wvt_ref, *, tiles_per_seq):
    tm = x_ref.shape[0]
    pos0 = (pl.program_id(0) % tiles_per_seq) * tm
    h = _rms(x_ref[...], g_ref[...]).astype(BF16)
    kv = _dot(h, w_ref[...])
    ones_bd = ones_ref[...]

    cmp_k, cmp_v = kv[:, 0:KV_LANES], kv[:, KV_LANES:ROW_LANES]
    cmp_ref[0, 0] = cmp_k.T
    cmp_ref[0, 1] = cmp_v.T
    kc_ref[...] = _head_rms(jnp.mean(cmp_k.reshape(tm // L_CMP, L_CMP, KV_LANES), axis=1), ones_bd, kc_gain_ref[...])
    vc_ref[...] = jnp.mean(cmp_v.reshape(tm // L_CMP, L_CMP, KV_LANES), axis=1)

    blk = (pos0 + lax.broadcasted_iota(jnp.int32, (tm, 1), 0)) // L_SEL
    onehot = lax.broadcasted_iota(jnp.int32, (1, KV_LANES), 1) - HEAD_DIM == blk
    ones_rows = jnp.ones((V_ROWS - HEAD_DIM, tm), F32)
    for br, (gain_ref, f_ref) in enumerate(((ks_ref, slc_ref), (kw_ref, win_ref)), start=1):
        base = br * ROW_LANES
        k = _head_rms(kv[:, base:base + KV_LANES], ones_bd, gain_ref[...])
        vt = kv[:, base + KV_LANES:base + ROW_LANES].T
        f_ref[0, 0] = k.T
        f_ref[0, 1] = vt
        kb = k.astype(BF16)
        for g in range(N_KV):
            vt_g = jnp.concatenate([vt[g * HEAD_DIM:(g + 1) * HEAD_DIM], ones_rows], axis=0).astype(BF16)
            if br == 1:
                sk_ref[0, g] = jnp.where(onehot, 1.0, _dot(kb, pk_ref[g])).astype(BF16)
                svt_ref[0, g, 0] = vt_g
            else:
                wk_ref[0, g] = _dot(kb, pw_ref[g]).astype(BF16)
                for j in range(tm // LANE):
                    wvt_ref[0, g, j] = vt_g[:, j * LANE:(j + 1) * LANE]


def _kv_rows_long(x, kv_norm, w_kv, ones_bd, ks_gain, kw_gain, kc_gain, pk, pw, *, seq_len):
    n, d = x.shape
    tm = SEL_TILE
    assert seq_len % tm == 0 and -(-seq_len // L_SEL) <= LANE and tm % (SUBLANE * L_CMP) == 0
    n_seq, tps = n // seq_len, seq_len // tm
    nb = tm // L_CMP
    row_spec = pl.BlockSpec((tm, d), lambda i: (i, 0))
    t_spec = pl.BlockSpec((1, 2, KV_LANES, tm), lambda i: (i // tps, 0, 0, i % tps))
    blk_spec = pl.BlockSpec((nb, KV_LANES), lambda i: (i, 0))
    const = lambda shape: pl.BlockSpec(shape, lambda i: (0,) * len(shape))
    grp = lambda *tail: pl.BlockSpec((1, N_KV) + tail, lambda i: (i // tps, 0, i % tps) + (0,) * (len(tail) - 1))
    sds = jax.ShapeDtypeStruct
    return pl.pallas_call(
        functools.partial(_kv_long_kernel, tiles_per_seq=tps), grid=(n // tm,),
        in_specs=[row_spec, const((1, d)), const(w_kv.shape), const((KV_LANES, KV_LANES)),
                  const((1, KV_LANES)), const((1, KV_LANES)), const((1, KV_LANES)), const(pk.shape), const(pw.shape)],
        out_specs=[t_spec] * 3 + [blk_spec] * 2 + [
            grp(tm, KV_LANES), grp(1, V_ROWS, tm), grp(tm, LANE), grp(tm // LANE, V_ROWS, LANE)],
        out_shape=[sds((n_seq, 2, KV_LANES, seq_len), F32)] * 3 + [sds((n // L_CMP, KV_LANES), F32)] * 2 + [
            sds((n_seq, N_KV, seq_len, KV_LANES), BF16), sds((n_seq, N_KV, tps, V_ROWS, tm), BF16),
            sds((n_seq, N_KV, seq_len, LANE), BF16), sds((n_seq, N_KV, seq_len // LANE, V_ROWS, LANE), BF16)],
        compiler_params=_params("arbitrary"),
        name="kv_rows_long",
    )(x, kv_norm, w_kv, ones_bd, ks_gain, kw_gain, kc_gain, pk, pw)


def _page_specs(n_pages, page, seqs=1):
    return [pl.BlockSpec((1, 2, KV_LANES, page), lambda b, pt, s=s, p=p: (pt[seqs * b + s, p], 0, 0, 0))
            for s in range(seqs) for p in range(n_pages)]


def _compress_pages_kernel(pt_ref, *refs, n_pages, seqs):
    page_refs = refs[:seqs * n_pages]
    pool_ref, gain_ref, kc_ref, vc_ref = refs[seqs * n_pages:]
    page = page_refs[0].shape[3]
    pool = pool_ref[...].astype(F32)
    for sq in range(seqs):
        pages = page_refs[sq * n_pages:(sq + 1) * n_pages]
        rows = jnp.concatenate([pages[p][0].reshape(ROW_LANES, page) for p in range(n_pages)], axis=1)
        acc = _dot(rows, pool)
        n_col = acc.shape[1]
        k = acc[0:KV_LANES].reshape(N_KV, HEAD_DIM, n_col)
        inv = lax.rsqrt(jnp.mean(k * k, axis=1, keepdims=True) + EPS)
        kc_ref[sq] = ((k * inv).reshape(KV_LANES, n_col) * gain_ref[...]).astype(kc_ref.dtype)
        vc_ref[sq] = acc[KV_LANES:ROW_LANES].astype(vc_ref.dtype)


def _compress_pages(pool, page_table, pool_mat, kc_gain_col):
    n_seq, n_pages = page_table.shape
    page = pool.shape[3]
    n_col = pool_mat.shape[1]
    seqs = DECODE_SEQS
    assert n_seq % seqs == 0
    const = lambda shape: pl.BlockSpec(shape, lambda b, pt: (0,) * len(shape))
    out_spec = pl.BlockSpec((seqs, KV_LANES, n_col), lambda b, pt: (b, 0, 0))
    return pl.pallas_call(
        functools.partial(_compress_pages_kernel, n_pages=n_pages, seqs=seqs),
        grid_spec=pltpu.PrefetchScalarGridSpec(
            num_scalar_prefetch=1, grid=(n_seq // seqs,),
            in_specs=_page_specs(n_pages, page, seqs) + [const(pool_mat.shape), const((KV_LANES, 1))],
            out_specs=[out_spec, out_spec]),
        out_shape=[jax.ShapeDtypeStruct((n_seq, KV_LANES, n_col), BF16)] * 2,
        compiler_params=_params("arbitrary"),
        name="compress_pages",
    )(page_table, *([pool] * (seqs * n_pages)), pool_mat, kc_gain_col)


def _even_odd(blocks, half):
    b, nbc, w = blocks.shape
    out = jnp.zeros((b, 2 * half, w), BF16)
    out = out.at[:, 0:(nbc + 1) // 2].set(blocks[:, 0::2].astype(BF16))
    return out.at[:, half:half + nbc // 2].set(blocks[:, 1::2].astype(BF16))


def _qg_kernel(x_ref, g_ref, wq_ref, wg_ref, ones_ref, qgain_ref, q_ref, gate_ref):
    h = _rms(x_ref[...], g_ref[...]).astype(BF16)
    q = _dot(h, wq_ref[...])
    ones_bd = ones_ref[...]
    gain = qgain_ref[...] * (HEAD_DIM ** -0.5 * LOG2E)
    for s in range(q.shape[1] // KV_LANES):
        sl = slice(s * KV_LANES, (s + 1) * KV_LANES)
        q_ref[:, sl] = _head_rms(q[:, sl], ones_bd, gain).astype(BF16)
    gate_ref[...] = jax.nn.sigmoid(_dot(h, wg_ref[...]))


def _qg_proj(x, g, w_q, w_g, ones_bd, q_gain):
    n, d = x.shape
    tm = min(ROW_TILE, n)
    assert n % tm == 0
    nq = w_q.shape[1]
    const = lambda shape: pl.BlockSpec(shape, lambda i: (0, 0))
    return pl.pallas_call(
        _qg_kernel, grid=(n // tm,),
        in_specs=[pl.BlockSpec((tm, d), lambda i: (i, 0)), const((1, d)), const(w_q.shape), const(w_g.shape),
                  const((KV_LANES, KV_LANES)), const((1, KV_LANES))],
        out_specs=[pl.BlockSpec((tm, nq), lambda i: (i, 0)), pl.BlockSpec((tm, w_g.shape[1]), lambda i: (i, 0))],
        out_shape=[jax.ShapeDtypeStruct((n, nq), BF16), jax.ShapeDtypeStruct((n, w_g.shape[1]), F32)],
        compiler_params=_params("arbitrary"),
        name="qg_proj",
    )(x, g, w_q, w_g, ones_bd, q_gain)


def _select_blocks(imp, t, k_sel, axis):
    blk = lax.broadcasted_iota(jnp.int32, imp.shape, axis)
    cur = t // L_SEL
    forced = (blk == 0) | (blk == cur) | (blk == cur - 1)
    n_forced = 3
    if k_sel < n_forced:
        work = jnp.where(blk > cur, NEG, jnp.where(forced, FORCE, imp))
        sel = jnp.zeros(imp.shape, jnp.bool_)
        rounds = k_sel
    else:
        work = jnp.where(forced | (blk > cur), NEG, imp)
        sel = forced & (blk <= cur)
        rounds = k_sel - n_forced
    blk_f = blk.astype(F32)
    for _ in range(rounds):
        m = jnp.max(work, axis=axis, keepdims=True)
        first = jnp.min(jnp.where(work == m, blk_f, float(imp.shape[axis])), axis=axis, keepdims=True)
        pick = blk_f == first
        sel = sel | (pick & (m > 0.5 * NEG))
        work = jnp.where(pick, -jnp.inf, work)
    return sel


def _attn_seq_kernel(q_ref, gate_ref, kc_ref, vct_ref, sk_ref, svt_ref, wk_ref, wvt_ref, wbias_ref, o_ref,
                     *scratch, nbc, half, k_sel):
    tq = q_ref.shape[1]
    c0 = pl.program_id(2) * tq
    cols = GROUP * tq
    t = c0 + lax.broadcasted_iota(jnp.int32, (1, cols), 1) % tq
    t1 = t[:, 0:tq]

    qt = q_ref[0].astype(F32).T
    q64 = jnp.concatenate([qt[r * HEAD_DIM:(r + 1) * HEAD_DIM] for r in range(GROUP)], axis=1)
    zeros64 = jnp.zeros((HEAD_DIM, cols), F32)
    q_pad = jnp.concatenate([q64, zeros64], axis=0).astype(BF16)

    row = lax.broadcasted_iota(jnp.int32, (2 * half, 1), 0)
    blk = 2 * (row % half) + row // half
    p = _softmax2(_dot(kc_ref[0, 0], q_pad), ((blk + 1) * L_CMP - 1 <= t) & (blk < nbc), 0)
    o_cmp = _dot(vct_ref[0, 0], p.astype(BF16))[0:HEAD_DIM]
    ph = p[:, 0:tq]
    for r in range(1, GROUP):
        ph = ph + p[:, r * tq:(r + 1) * tq]
    sel = _select_blocks(ph[0:half] + ph[half:2 * half], t1, k_sel, 0)
    bias = jnp.where(sel, 0.0, NEG)
    q_sel = jnp.concatenate([q64, jnp.concatenate([bias] * GROUP, axis=1), zeros64], axis=0).astype(BF16)

    band = tq + WINDOW
    wb = jnp.maximum(c0 - WINDOW, 0) // LANE
    w0 = pl.multiple_of(wb * LANE, LANE)
    s_win = _dot(wk_ref[0, 0, pl.ds(w0, band), :], q_pad) + wbias_ref[0]

    ns = SWEEP_PARTS
    m_refs, acc_refs = scratch[0:ns], scratch[ns:2 * ns]
    buf_x, buf_y = scratch[2 * ns:3 * ns], scratch[3 * ns:4 * ns]
    hk = SEL_TILE // ns
    for m_ref, acc_ref in zip(m_refs, acc_refs):
        m_ref[...] = jnp.full(m_ref.shape, NEG, F32)
        acc_ref[...] = jnp.zeros(acc_ref.shape, F32)

    def scores(kt, h):
        k0 = pl.multiple_of(kt * SEL_TILE + h * hk, hk)
        return _dot(sk_ref[0, 0, pl.ds(k0, hk), :], q_sel)

    def update(s, kt, h, causal):
        m_ref, acc_ref = m_refs[h], acc_refs[h]
        if causal:
            s = jnp.where(kt * SEL_TILE + h * hk + lax.broadcasted_iota(jnp.int32, (hk, 1), 0) <= t, s, NEG)
        m_old = m_ref[...]
        m_new = jnp.maximum(m_old, jnp.max(s, axis=0, keepdims=True))
        p = jnp.exp2(s - m_new).astype(BF16)
        vt = svt_ref[0, 0, kt][:, h * hk:(h + 1) * hk]
        acc_ref[...] = jnp.exp2(m_old - m_new) * acc_ref[...] + _dot(vt, p)
        m_ref[...] = m_new

    def fill(kt, buf):
        for h in range(ns):
            buf[h][...] = scores(kt, h)

    def drain(kt, buf, causal):
        for h in range(ns):
            update(buf[h][...], kt, h, causal)

    last = (c0 + tq - 1) // SEL_TILE
    fill(0, buf_x)

    p = jnp.exp2(s_win - jnp.max(s_win, axis=0, keepdims=True)).astype(BF16)
    wvt = jnp.concatenate([wvt_ref[0, 0, wb + j] for j in range(band // LANE)], axis=1)
    o_win = _dot(wvt, p)
    o_win = o_win[0:HEAD_DIM] * (1.0 / o_win[HEAD_DIM:HEAD_DIM + 1])

    def pair(kt):
        fill(kt + 1, buf_y)
        drain(kt, buf_x, False)
        fill(kt + 2, buf_x)
        drain(kt + 1, buf_y, False)

    def quad(i, carry):
        pair(4 * i)
        pair(4 * i + 2)
        return carry

    def one_pair(i, carry):
        pair(2 * i)
        return carry

    def octet(i, carry):
        quad(2 * i, carry)
        return quad(2 * i + 1, carry)

    lax.fori_loop(0, last // 8, octet, 0)
    lax.fori_loop(2 * (last // 8), last // 4, quad, 0)
    lax.fori_loop(2 * (last // 4), last // 2, one_pair, 0)

    @pl.when(last % 2 == 0)
    def _():
        drain(last, buf_x, True)

    @pl.when(last % 2 == 1)
    def _():
        fill(last, buf_y)
        drain(last - 1, buf_x, False)
        drain(last, buf_y, True)

    m = functools.reduce(jnp.maximum, [m_ref[...] for m_ref in m_refs])
    acc = sum(jnp.exp2(m_ref[...] - m) * acc_ref[...] for m_ref, acc_ref in zip(m_refs, acc_refs))
    o_sel = acc[0:HEAD_DIM] * (1.0 / acc[HEAD_DIM:HEAD_DIM + 1])

    gt = gate_ref[0].T
    gate = lambda br: jnp.concatenate([gt[r * N_BRANCH + br:r * N_BRANCH + br + 1] for r in range(GROUP)], axis=1)
    o = gate(0) * o_cmp + gate(1) * o_sel + gate(2) * o_win
    o = jnp.concatenate([o[:, r * tq:(r + 1) * tq] for r in range(GROUP)], axis=0)
    o_ref[0] = o.T.astype(o_ref.dtype)


def _window_bias(tq):
    v = jnp.arange(WINDOW // tq + 1)[:, None, None]
    j = jnp.arange(tq + WINDOW)[None, :, None]
    i = (jnp.arange(GROUP * tq) % tq)[None, None, :]
    dist = (v * tq + i) - (jnp.maximum(v * tq - WINDOW, 0) + j)
    return jnp.where((dist >= 0) & (dist <= WINDOW), 0.0, NEG).astype(F32)


def _attn_seq(q, gates, kc, vct, sk, svt, wk, wvt, *, nbc, half, k_sel):
    b, t_len, nq = q.shape
    tq = ATTN_CHUNK
    assert tq % LANE == 0 and half == LANE and t_len % SEL_TILE == 0 and t_len >= tq + WINDOW and WINDOW % tq == 0
    cols = GROUP * tq
    band = tq + WINDOW
    n_var = WINDOW // tq
    grp = lambda *shape: pl.BlockSpec((1, 1) + shape, lambda i, g, c: (i, g) + (0,) * len(shape))
    chunk = lambda w: pl.BlockSpec((1, tq, w), lambda i, g, c: (i, c, g))
    wbias_spec = pl.BlockSpec((1, band, cols), lambda i, g, c: (jnp.minimum(c, n_var), 0, 0))
    stat = pltpu.VMEM((1, cols), F32)
    acc = pltpu.VMEM((V_ROWS, cols), F32)
    return pl.pallas_call(
        functools.partial(_attn_seq_kernel, nbc=nbc, half=half, k_sel=k_sel),
        grid=(b, N_KV, t_len // tq),
        in_specs=[chunk(KV_LANES), chunk(LANE), grp(2 * half, LANE), grp(HEAD_DIM, 2 * half),
                  grp(t_len, KV_LANES), grp(t_len // SEL_TILE, V_ROWS, SEL_TILE),
                  grp(t_len, LANE), grp(t_len // LANE, V_ROWS, LANE), wbias_spec],
        out_specs=chunk(KV_LANES),
        out_shape=jax.ShapeDtypeStruct((b, t_len, nq), BF16),
        scratch_shapes=[stat] * SWEEP_PARTS + [acc] * SWEEP_PARTS
        + [pltpu.VMEM((SEL_TILE // SWEEP_PARTS, cols), F32)] * (2 * SWEEP_PARTS),
        compiler_params=_params("arbitrary", "arbitrary", "arbitrary"),
        name="attn_seq",
    )(q, gates, kc, vct, sk, svt, wk, wvt, _window_bias(tq))


def _attn_paged_kernel(pt_ref, *refs, n_pages, seqs, past, nbc, half, k_sel):
    page_refs = refs[:seqs * n_pages]
    (q_ref, gate_ref, kc_ref, vc_ref, snew_ref, wold_ref, wnew_ref, e_ref, ge_ref, o_ref) = refs[seqs * n_pages:]
    tq = q_ref.shape[1]
    page = page_refs[0].shape[3]
    w_buf = wold_ref.shape[3]
    rows_g = GROUP * tq
    t1 = past + lax.broadcasted_iota(jnp.int32, (tq, 1), 0)
    t = jnp.concatenate([t1] * (N_KV * GROUP), axis=0)
    t_col = past + lax.broadcasted_iota(jnp.int32, (1, LANE), 1) % tq
    lane_group = lax.broadcasted_iota(jnp.int32, (1, KV_LANES), 1) // HEAD_DIM
    col = lax.broadcasted_iota(jnp.int32, (1, 2 * half), 1)
    blk = 2 * (col % half) + col // half
    cmp_mask = ((blk + 1) * L_CMP - 1 <= t) & (blk < nbc)

    def pad_rows(x):
        return jnp.concatenate([x, jnp.zeros((LANE - x.shape[0], x.shape[1]), x.dtype)], axis=0).astype(BF16)

    for sq in range(seqs):
        pages = page_refs[sq * n_pages:(sq + 1) * n_pages]
        q_slabs = [q_ref[sq, :, r * KV_LANES:(r + 1) * KV_LANES].astype(F32) for r in range(GROUP)]
        qe = jnp.concatenate([jnp.where(lane_group == g, s, 0.0) for g in range(N_KV) for s in q_slabs], axis=0).astype(BF16)

        p = _softmax2(_dot(qe, kc_ref[sq]), cmp_mask, 1)
        o_cmp = _dot_nt(p.astype(BF16), vc_ref[sq])
        imps = []
        for g in range(N_KV):
            ph = p[g * rows_g:g * rows_g + tq]
            for r in range(1, GROUP):
                ph = ph + p[g * rows_g + r * tq:g * rows_g + (r + 1) * tq]
            imps.append(ph[:, 0:half] + ph[:, half:2 * half])
        imp = jnp.concatenate(imps + [jnp.zeros((LANE - N_KV * tq, half), F32)], axis=0)
        bias = jnp.where(_select_blocks(imp.T, t_col, k_sel, 0), 0.0, NEG).T
        bias = jnp.concatenate([bias[g * tq:(g + 1) * tq] for g in range(N_KV) for _ in range(GROUP)], axis=0)

        snew = pad_rows(snew_ref[sq])
        s = jnp.concatenate([_dot(qe, pages[i][0, 0].astype(BF16)) for i in range(n_pages)]
                            + [_dot_nt(qe, snew[:, 0:KV_LANES])], axis=1) + _dot(bias.astype(BF16), e_ref[...])
        kpos = lax.broadcasted_iota(jnp.int32, (1, s.shape[1]), 1)
        p = _softmax2(s, kpos <= t, 1).astype(BF16)
        vt_all = jnp.concatenate([pages[i][0, 1].astype(BF16) for i in range(n_pages)], axis=1)
        o_sel = _dot_nt(p[:, 0:n_pages * page], vt_all) + _dot(p[:, n_pages * page:], snew[:, KV_LANES:ROW_LANES])

        wnew = pad_rows(wnew_ref[sq])
        s = jnp.concatenate([_dot(qe, wold_ref[sq, 0].astype(BF16)), _dot_nt(qe, wnew[:, 0:KV_LANES])], axis=1)
        j = lax.broadcasted_iota(jnp.int32, (1, s.shape[1]), 1)
        dist = t - (past - w_buf + j)
        p = _softmax2(s, (dist >= 0) & (dist <= WINDOW) & (j < w_buf + tq), 1).astype(BF16)
        o_win = _dot_nt(p[:, 0:w_buf], wold_ref[sq, 1].astype(BF16)) + _dot(p[:, w_buf:], wnew[:, KV_LANES:ROW_LANES])

        gates = [_split_dot(gate_ref[sq], ge_ref[br]) for br in range(N_BRANCH)]
        for r in range(GROUP):
            acc = None
            for br, o_br in enumerate((o_cmp, o_sel, o_win)):
                o = jnp.zeros((tq, KV_LANES), F32)
                for g in range(N_KV):
                    o = jnp.where(lane_group == g, o_br[g * rows_g + r * tq:g * rows_g + (r + 1) * tq], o)
                term = gates[br][:, r * KV_LANES:(r + 1) * KV_LANES] * o
                acc = term if acc is None else acc + term
            o_ref[sq, :, r * KV_LANES:(r + 1) * KV_LANES] = acc.astype(o_ref.dtype)


def _attn_paged(q, gates, kc, vc, pool, page_table, slc_new, win_old, win_new, e_sel, ge, *, nbc, half, k_sel):
    n_seq, tq, nq = q.shape
    n_pages = page_table.shape[1]
    page = pool.shape[3]
    past = n_pages * page
    w_buf = win_old.shape[3]
    seqs = DECODE_SEQS
    assert page == LANE and half == LANE and w_buf % LANE == 0 and tq % SUBLANE == 0 and N_KV * tq <= LANE and n_seq % seqs == 0
    per_seq = lambda *shape: pl.BlockSpec((seqs,) + shape, lambda b, pt: (b,) + (0,) * len(shape))
    const = lambda shape: pl.BlockSpec(shape, lambda b, pt: (0,) * len(shape))
    return pl.pallas_call(
        functools.partial(_attn_paged_kernel, n_pages=n_pages, seqs=seqs, past=past, nbc=nbc, half=half, k_sel=k_sel),
        grid_spec=pltpu.PrefetchScalarGridSpec(
            num_scalar_prefetch=1, grid=(n_seq // seqs,),
            in_specs=_page_specs(n_pages, page, seqs) + [
                per_seq(tq, nq), per_seq(tq, LANE), per_seq(KV_LANES, 2 * half), per_seq(KV_LANES, 2 * half),
                per_seq(tq, ROW_LANES), per_seq(2, KV_LANES, w_buf), per_seq(tq, ROW_LANES),
                const(e_sel.shape), const(ge.shape)],
            out_specs=per_seq(tq, nq)),
        out_shape=jax.ShapeDtypeStruct((n_seq, tq, nq), BF16),
        compiler_params=_params("arbitrary"),
        name="attn_paged",
    )(page_table, *([pool] * (seqs * n_pages)), q, gates, kc, vc, slc_new, win_old, win_new, e_sel, ge)


def _round_up(x, m):
    return -(-x // m) * m


def _block_onehot(n_rows, n_keys, t_valid):
    key = jnp.arange(n_keys)[None, :]
    return ((key // L_SEL == jnp.arange(n_rows)[:, None]) & (key < t_valid)).astype(BF16)


def _keys_minor(rows5):
    b, t = rows5.shape[:2]
    return jnp.transpose(rows5, (0, 2, 3, 4, 1)).reshape(b, 2, KV_LANES, t)


def _rows_major(kt, row):
    b, t = kt.shape[0], kt.shape[3]
    return jnp.transpose(kt.reshape((b,) + row + (t,)), (0, 4, 1, 2, 3))


def kernel(x_prompt, x_sample, cache_cmp_kv, cache_slc_kv, cache_win_kv, state_conv, page_table, norm1, a_w_in, a_conv_w, a_w_out, kv_norm, w_kv, kc_norm, ks_norm, kw_norm, b_w_qg, b_q_norm, b_w_o, norm2, w_up, w_down):
    bp, tp, d = x_prompt.shape
    bs, ts, _ = x_sample.shape
    depth = norm1.shape[0]
    n_a = a_w_in.shape[0]
    n_pool, page = cache_cmp_kv.shape[:2]
    n_pages = page_table.shape[1]
    past = n_pages * page
    w_buf = cache_win_kv.shape[1]
    row = cache_cmp_kv.shape[2:]
    nq = N_HEADS * HEAD_DIM
    assert d == nq and row == (2, N_KV, HEAD_DIM) and past % L_CMP == 0 and page % L_CMP == 0 and ts < L_CMP

    hperm = jnp.array([(g * GROUP + r) * HEAD_DIM + dd for r in range(GROUP) for g in range(N_KV) for dd in range(HEAD_DIM)])
    lane = jnp.arange(KV_LANES)
    ones_bd = (lane[:, None] // HEAD_DIM == lane[None, :] // HEAD_DIM).astype(BF16)
    tile4 = lambda v: jnp.tile(v.astype(F32), N_KV)[None, :]
    head_of_lane = (jnp.arange(nq) // HEAD_DIM % N_KV) * GROUP + jnp.arange(nq) // KV_LANES
    gcol = jnp.arange(LANE)
    ge = jnp.stack([(gcol[:, None] == head_of_lane[None, :] * N_BRANCH + br) for br in range(N_BRANCH)]).astype(BF16)
    w_in_b = a_w_in.astype(BF16)
    w_out_b = a_w_out.astype(BF16)
    w_up_b = w_up.astype(BF16)
    w_down_b = w_down.astype(BF16)
    w_kv_b = w_kv.astype(BF16)
    w_q_nat = b_w_qg[:, :, :nq].astype(BF16)
    w_q_b = w_q_nat[:, :, hperm]
    w_g_b = jnp.pad(b_w_qg[:, :, nq:], ((0, 0), (0, 0), (0, LANE - N_HEADS * N_BRANCH))).astype(BF16)
    w_g_grp = b_w_qg[:, :, nq:].reshape(b_w_qg.shape[0], d, N_KV, GROUP * N_BRANCH)
    w_g_grp = jnp.pad(w_g_grp, ((0, 0), (0, 0), (0, 0), (0, LANE - GROUP * N_BRANCH))).reshape(-1, d, N_KV * LANE).astype(BF16)
    w_o_nat = b_w_o.astype(BF16)
    w_o_b = w_o_nat[:, hperm, :]
    src = jnp.arange(KV_LANES)[None, :, None]
    dst = jnp.arange(KV_LANES)[None, None, :]
    pk = ((src == jnp.arange(N_KV)[:, None, None] * HEAD_DIM + dst) & (dst < HEAD_DIM)).astype(BF16)
    pw = pk[:, :, 0:LANE]
    g1 = norm1.astype(F32)[:, None, :]
    g2 = norm2.astype(F32)[:, None, :]
    kvn = kv_norm.astype(F32)[None, :]

    xp = x_prompt.reshape(bp * tp, d)
    xs = x_sample.reshape(bs * ts, d)
    p_conv, s_conv = [], []
    for l in range(n_a):
        xp, st = _conv_layer(xp, g1[l], w_in_b[l], a_conv_w[l], w_out_b[l], seq_len=tp)
        p_conv.append(st)
        xs, st = _conv_layer(xs, g1[l], w_in_b[l], a_conv_w[l], w_out_b[l], seq_len=ts, state=state_conv[l])
        s_conv.append(st)
        xp = _mlp_layer(xp, g2[l], w_up_b[l], w_down_b[l])
        xs = _mlp_layer(xs, g2[l], w_up_b[l], w_down_b[l])

    ks_g, kw_g, kc_g = tile4(ks_norm), tile4(kw_norm), tile4(kc_norm)
    (p_cmp_t, p_slc_t, p_win_t, kc_p, vc_p, p_sk, p_svt, p_wk, p_wvt) = _kv_rows_long(
        xp, kvn, w_kv_b, ones_bd, ks_g, kw_g, kc_g, pk, pw, seq_len=tp)
    s_cmp, s_slc, s_win = _kv_rows(xs, kvn, w_kv_b, ones_bd, ks_g, kw_g)

    nbc_p = tp // L_CMP
    nbs_p = -(-tp // L_SEL)
    half_p = _round_up(nbs_p, LANE)
    per_group = lambda a: _even_odd(a.reshape(bp, nbc_p, KV_LANES), half_p).reshape(bp, 2 * half_p, N_KV, HEAD_DIM)
    kc_p = jnp.pad(per_group(kc_p).transpose(0, 2, 1, 3), ((0, 0), (0, 0), (0, 0), (0, LANE - HEAD_DIM)))
    vct_p = per_group(vc_p).transpose(0, 2, 3, 1)

    t_all = past + ts
    nbc_s = t_all // L_CMP
    assert nbc_s * L_CMP == past
    nbs_s = -(-t_all // L_SEL)
    half_s = _round_up(nbs_s, LANE)
    pool_cmp = _keys_minor(cache_cmp_kv)
    pool_slc = _keys_minor(cache_slc_kv)
    win_old = _keys_minor(cache_win_kv)
    blk_of_key = jnp.arange(past) // L_CMP
    col_of_key = blk_of_key // 2 + half_s * (blk_of_key % 2)
    pool_mat = (col_of_key[:, None] == jnp.arange(2 * half_s)[None, :]).astype(F32) * (1.0 / L_CMP)
    pool_mat = pool_mat.astype(BF16)
    kc_s, vc_s = _compress_pages(pool_cmp, page_table, pool_mat, jnp.tile(kc_norm.astype(F32), N_KV)[:, None])
    e_s = _block_onehot(half_s, past + LANE, t_all)

    for j in range(depth - n_a):
        l = n_a + j
        qg = tile4(b_q_norm[j])
        q, gates = _qg_proj(xp, g1[l], w_q_nat[j], w_g_grp[j], ones_bd, qg)
        o = _attn_seq(q.reshape(bp, tp, nq), gates.reshape(bp, tp, N_KV * LANE), kc_p, vct_p, p_sk, p_svt, p_wk, p_wvt,
                      nbc=nbc_p, half=half_p, k_sel=min(N_SEL, nbs_p))
        xp = _mlp_layer(xp, g2[l], w_up_b[l], w_down_b[l], attn=o.reshape(bp * tp, nq), w_o=w_o_nat[j])

        q, gates = _qg_proj(xs, g1[l], w_q_b[j], w_g_b[j], ones_bd, qg)
        o = _attn_paged(q.reshape(bs, ts, nq), gates.reshape(bs, ts, LANE), kc_s, vc_s, pool_slc, page_table,
                        s_slc.reshape(bs, ts, ROW_LANES), win_old, s_win.reshape(bs, ts, ROW_LANES), e_s, ge,
                        nbc=nbc_s, half=half_s, k_sel=min(N_SEL, nbs_s))
        xs = _mlp_layer(xs, g2[l], w_up_b[l], w_down_b[l], attn=o.reshape(bs * ts, nq), w_o=w_o_b[j])

    rows5 = lambda a, b_, t_: a.reshape((b_, t_) + row)
    s_win_all = jnp.concatenate([cache_win_kv, rows5(s_win, bs, ts)], axis=1)
    return (xp.reshape(bp, tp, d), xs.reshape(bs, ts, d),
            _rows_major(p_cmp_t, row), _rows_major(p_slc_t, row), _rows_major(p_win_t[..., tp - min(WINDOW, tp):], row),
            jnp.stack(p_conv),
            rows5(s_cmp, bs, ts), rows5(s_slc, bs, ts), s_win_all[:, -w_buf:], jnp.stack(s_conv))
```
